```python
import math
import jax, jax.numpy as jnp
from jax import lax
import numpy as np

D_MODEL = 1024
BATCH = 8
SEQ = 2048
DEPTH = 2
DEC_BATCH = 128
DEC_SEQ = 8
PAST_LEN = 2048
PAGE_SIZE = 128

HEAD_DIM = 64
ML_HEADS = 4
RW_HEADS = 4
NSA_HEADS = 8
NSA_KV_HEADS = 2
NSA_GROUP = NSA_HEADS // NSA_KV_HEADS
ML_WIDTH = ML_HEADS * HEAD_DIM
RW_WIDTH = RW_HEADS * HEAD_DIM
NSA_WIDTH = NSA_HEADS * HEAD_DIM
KV_WIDTH = NSA_KV_HEADS * HEAD_DIM
MIX_WIDTH = ML_WIDTH + RW_WIDTH + NSA_WIDTH
ML_CHUNK = 64
ML_CONV = 4
ML_NORM_EPS = 1e-6
RW_LORA_W = 64
RW_LORA_A = 64
RW_LORA_G = 128
RW_GN_EPS = 64e-5
CMP_LEN = 32
CMP_STRIDE = 16
CMP_HID = 128
SEL_BLOCK = 64
SEL_TOPN = 8
WINDOW = 512
Q_BLOCK = 64
ROPE_THETA = 10000.0
D_FF = 2816
N_EXPERTS = 8
TOP_K = 2
D_EXP = 1408
RMS_EPS = 1e-6
NEG = -1e30
SEL_NEG = -1e9
FORCE_BONUS = 1e3
ML_COLS = 4 * ML_WIDTH + 2 * ML_HEADS
RW_COLS = 3 * RW_WIDTH + RW_LORA_W + RW_LORA_A + RW_LORA_G
NSA_COLS = NSA_WIDTH + 6 * KV_WIDTH + 3 * NSA_HEADS
N_IN = ML_COLS + RW_COLS + NSA_COLS

kernel_name = 'hybrid_mlstm_rwkv7_nsa_step'


def split_cols(a, sizes):
    out, s = [], 0
    for n in sizes:
        out.append(a[..., s:s + n])
        s += n
    return out


def heads(a, h):
    return a.reshape(a.shape[:2] + (h, HEAD_DIM))


def rmsnorm(x, w):
    xf = x.astype(jnp.float32)
    y = xf * lax.rsqrt(jnp.mean(xf * xf, axis=-1, keepdims=True) + RMS_EPS)
    return (y * w.astype(jnp.float32)).astype(x.dtype)


def head_norm(h, w, eps):
    hf = h.astype(jnp.float32)
    mu = jnp.mean(hf, -1, keepdims=True)
    var = jnp.mean(jnp.square(hf - mu), -1, keepdims=True)
    y = ((hf - mu) * lax.rsqrt(var + eps)).reshape(h.shape[:2] + (-1,))
    return y * w.astype(jnp.float32)


def rope(x, pos):
    half = HEAD_DIM // 2
    inv = ROPE_THETA ** (-jnp.arange(half, dtype=jnp.float32) / half)
    ang = pos.astype(jnp.float32)[:, None] * inv[None, :]
    cos = jnp.cos(ang)[None, :, None, :]
    sin = jnp.sin(ang)[None, :, None, :]
    x1 = x[..., :half].astype(jnp.float32)
    x2 = x[..., half:].astype(jnp.float32)
    return jnp.concatenate([x1 * cos - x2 * sin, x2 * cos + x1 * sin], -1).astype(x.dtype)


def masked_softmax(s, mask):
    s = jnp.where(mask, s.astype(jnp.float32), NEG)
    m = jnp.max(s, axis=-1, keepdims=True)
    p = jnp.exp(s - m) * mask
    return p / jnp.maximum(jnp.sum(p, -1, keepdims=True), 1e-30)


def last_rows(a, n):
    length = a.shape[1]
    if length >= n:
        return a[:, length - n:]
    return jnp.pad(a, ((0, 0), (n - length, 0)) + ((0, 0),) * (a.ndim - 2))


def causal_conv(x, buf, w, b):
    t = x.shape[1]
    xp = jnp.concatenate([buf.astype(x.dtype), x], 1)
    y = b
    for j in range(ML_CONV):
        y = y + xp[:, j:j + t] * w[j]
    return y, xp[:, -(ML_CONV - 1):]


def mlstm_chunkwise(q, k, v, ig, fg, C0, n0, m0):
    f32 = jnp.float32
    B, T, H, Dh = q.shape
    L = math.gcd(T, ML_CHUNK)
    NC = T // L

    def chunks(a):
        a = a.astype(f32).reshape((B, NC, L, H) + a.shape[3:])
        return jnp.moveaxis(a, (1, 3), (0, 2))

    xs = (chunks(q), chunks(k), chunks(v), chunks(ig), jax.nn.log_sigmoid(chunks(fg)))
    causal = jnp.tril(jnp.ones((L, L), bool))

    def step(carry, inp):
        C, n, m = carry
        qb, kb, vb, ib, lf = inp
        b = jnp.cumsum(lf, axis=-1)
        a = ib - b
        m_t = jnp.maximum(b + m[..., None], b + lax.cummax(a, axis=2))
        logd = jnp.where(causal, b[..., :, None] + a[..., None, :] - m_t[..., :, None], -jnp.inf)
        S = jnp.einsum('bhtd,bhsd->bhts', qb, kb) * jnp.exp(logd)
        inter = jnp.exp(b + m[..., None] - m_t)
        num = inter[..., None] * jnp.einsum('bhvd,bhtd->bhtv', C, qb) + jnp.einsum('bhts,bhsv->bhtv', S, vb)
        den = inter * jnp.einsum('bhd,bhtd->bht', n, qb) + jnp.sum(S, -1)
        h = num / jnp.maximum(jnp.abs(den), jnp.exp(-m_t))[..., None]
        m_new = m_t[..., -1]
        wgt = jnp.exp(b[..., -1:] + a - m_new[..., None])
        dec = jnp.exp(b[..., -1] + m - m_new)
        C_new = dec[..., None, None] * C + jnp.einsum('bhs,bhsv,bhsd->bhvd', wgt, vb, kb)
        n_new = dec[..., None] * n + jnp.einsum('bhs,bhsd->bhd', wgt, kb)
        return (C_new, n_new, m_new), h

    (C, n, m), hs = lax.scan(step, (C0.astype(f32), n0.astype(f32), m0.astype(f32)), xs)
    h = jnp.moveaxis(hs, (0, 2), (1, 3)).reshape(B, T, H, Dh)
    return h, C, n, m


def rwkv7_scan(r, w, k, v, kk, a, S0):
    def step(S, inp):
        rt, wt, kt, vt, kkt, at = inp
        sa = jnp.einsum('bhvk,bhk->bhv', S, kkt)
        S = S * wt[:, :, None, :] - sa[..., None] * (at * kkt)[:, :, None, :] + vt[..., None] * kt[:, :, None, :]
        return S, jnp.einsum('bhvk,bhk->bhv', S, rt)

    xs = tuple(jnp.moveaxis(t.astype(jnp.float32), 1, 0) for t in (r, w, k, v, kk, a))
    S, ys = lax.scan(step, S0.astype(jnp.float32), xs)
    return jnp.moveaxis(ys, 0, 1), S


def nsa_attend(q, q_rot, gates, kv_full, win_kv, offset, cmp_pos, cmp_w1, cmp_w2):
    B, T = q.shape[:2]
    P = kv_full.shape[1]
    Lbuf = win_kv.shape[1] - T
    dt = q.dtype
    scale = HEAD_DIM ** -0.5
    KVH, G = NSA_KV_HEADS, NSA_GROUP
    NC = (P - CMP_LEN) // CMP_STRIDE + 1
    cidx = np.arange(NC)[:, None] * CMP_STRIDE + np.arange(CMP_LEN)[None, :]
    blocks = kv_full[:, cidx, :2]
    blocks = blocks + jnp.transpose(cmp_pos, (1, 0, 2))[None, None, :, :, None, :]
    blocks = jnp.moveaxis(blocks, 2, 4).reshape(B, NC, 2, KVH, CMP_LEN * HEAD_DIM)
    hid = jax.nn.silu(jnp.einsum('bcjgf,jfe->bcjge', blocks, cmp_w1))
    ckv = jnp.einsum('bcjge,jed->bcjgd', hid, cmp_w2)
    ck, cv = ckv[:, :, 0], ckv[:, :, 1]
    cmp_end = jnp.asarray(np.arange(NC) * CMP_STRIDE + CMP_LEN - 1)
    NS = -(-P // SEL_BLOCK)
    cs = np.arange(NC) * CMP_STRIDE
    ss = np.arange(NS) * SEL_BLOCK
    ov = np.clip(np.minimum(cs[:, None] + CMP_LEN, ss[None, :] + SEL_BLOCK) - np.maximum(cs[:, None], ss[None, :]), 0, None) / CMP_LEN
    ov = jnp.asarray(ov, jnp.float32)
    slc = jnp.pad(kv_full[:, :, 2:], ((0, 0), (0, NS * SEL_BLOCK - P), (0, 0), (0, 0), (0, 0)))
    slc = slc.reshape(B, NS, SEL_BLOCK, 2, KVH, HEAD_DIM).transpose(0, 4, 1, 2, 3, 5)
    K_SEL = min(SEL_TOPN, NS)
    wpad = jnp.pad(win_kv, ((0, 0), (WINDOW, 0), (0, 0), (0, 0), (0, 0)))
    QB = math.gcd(T, Q_BLOCK)
    NB = T // QB
    qg = q.reshape(B, T, KVH, G, HEAD_DIM)
    qrg = q_rot.reshape(B, T, KVH, G, HEAD_DIM)
    gg = gates.reshape(B, T, KVH, G, 3)
    bi = jnp.arange(B)[:, None, None, None]
    gi = jnp.arange(KVH)[None, None, :, None]
    blk = jnp.arange(NS)

    def block(nb):
        qs = nb * QB
        tpos = offset + qs + jnp.arange(QB)
        qb = lax.dynamic_slice_in_dim(qg, qs, QB, 1)
        qrb = lax.dynamic_slice_in_dim(qrg, qs, QB, 1)
        s_c = jnp.einsum('bqghd,bcgd->bqghc', qb, ck) * scale
        p_c = masked_softmax(s_c, (cmp_end[None, :] <= tpos[:, None])[None, :, None, None, :])
        o_c = jnp.einsum('bqghc,bcgd->bqghd', p_c.astype(dt), cv)
        imp = jnp.einsum('bqghc,cs->bqgs', p_c, ov)
        cur = tpos // SEL_BLOCK
        avail = (blk[None, :] * SEL_BLOCK <= tpos[:, None])[None, :, None, :]
        forced = ((blk[None, :] == 0) | (blk[None, :] == cur[:, None]) | (blk[None, :] == cur[:, None] - 1))[None, :, None, :]
        score = jnp.where(avail, imp + jnp.where(forced, FORCE_BONUS, 0.0), SEL_NEG)
        top_v, top_i = lax.top_k(score, K_SEL)
        valid = top_v > 0.5 * SEL_NEG
        sk = slc[bi, gi, top_i]
        kpos = top_i[..., None] * SEL_BLOCK + jnp.arange(SEL_BLOCK)
        mask_s = valid[..., None] & (kpos <= tpos[None, :, None, None, None])
        s_s = jnp.einsum('bqghd,bqgksd->bqghks', qrb, sk[..., 0, :]) * scale
        p_s = masked_softmax(s_s.reshape(B, QB, KVH, G, K_SEL * SEL_BLOCK),
                             mask_s.reshape(B, QB, KVH, 1, K_SEL * SEL_BLOCK)).reshape(s_s.shape)
        o_s = jnp.einsum('bqghks,bqgksd->bqghd', p_s.astype(dt), sk[..., 1, :])
        wk = lax.dynamic_slice_in_dim(wpad, qs + Lbuf, WINDOW + QB, 1)
        ridx = qs + Lbuf + jnp.arange(WINDOW + QB)
        wpos = offset - Lbuf - WINDOW + ridx
        dlt = tpos[:, None] - wpos[None, :]
        mask_w = (ridx >= WINDOW)[None, :] & (dlt >= 0) & (dlt < WINDOW)
        s_w = jnp.einsum('bqghd,bjgd->bqghj', qrb, wk[:, :, 0]) * scale
        p_w = masked_softmax(s_w, mask_w[None, :, None, None, :])
        o_w = jnp.einsum('bqghj,bjgd->bqghd', p_w.astype(dt), wk[:, :, 1])
        gb = lax.dynamic_slice_in_dim(gg, qs, QB, 1)
        return gb[..., 0:1] * o_c + gb[..., 1:2] * o_s + gb[..., 2:3] * o_w

    out = lax.map(block, jnp.arange(NB))
    return jnp.moveaxis(out, 0, 1).reshape(B, T, NSA_WIDTH)


def swiglu(x, w1, w3, w2):
    return (jax.nn.silu(x @ w1) * (x @ w3)) @ w2


def setup_inputs(seed: int = 0) -> dict:
    key = jax.random.key(seed)
    ks = iter(jax.random.split(key, 64))
    f32 = jnp.float32

    def nrm(shape, scale=1.0):
        return jax.random.normal(next(ks), shape, f32) * scale

    def uni(shape, lo, hi):
        return jax.random.uniform(next(ks), shape, f32, lo, hi)

    n_pages = PAST_LEN // PAGE_SIZE
    n_pool = (DEC_BATCH * n_pages * 5) // 4
    wbuf = min(WINDOW, PAST_LEN)
    n_dense = (DEPTH + 1) // 2
    n_moe = DEPTH // 2
    page_table = jax.random.permutation(next(ks), n_pool)[:DEC_BATCH * n_pages].reshape(DEC_BATCH, n_pages).astype(jnp.int32)
    return {
        'x_prompt': nrm((BATCH, SEQ, D_MODEL)),
        'x_sample': nrm((DEC_BATCH, DEC_SEQ, D_MODEL)),
        'cache_nsa_kv': nrm((DEPTH, n_pool, PAGE_SIZE, 4, NSA_KV_HEADS, HEAD_DIM)),
        'cache_win_kv': nrm((DEPTH, DEC_BATCH, wbuf, 2, NSA_KV_HEADS, HEAD_DIM)),
        'state_mlstm_C': nrm((DEPTH, DEC_BATCH, ML_HEADS, HEAD_DIM, HEAD_DIM), 0.3),
        'state_mlstm_n': nrm((DEPTH, DEC_BATCH, ML_HEADS, HEAD_DIM), 0.3),
        'state_mlstm_m': nrm((DEPTH, DEC_BATCH, ML_HEADS), 0.5),
        'state_mlstm_conv': nrm((DEPTH, DEC_BATCH, ML_CONV - 1, 2 * ML_WIDTH)),
        'state_rwkv_S': nrm((DEPTH, DEC_BATCH, RW_HEADS, HEAD_DIM, HEAD_DIM), 0.3),
        'state_rwkv_shift': nrm((DEPTH, DEC_BATCH, RW_COLS)),
        'page_table': page_table,
        'norm_mix_w': 1.0 + nrm((DEPTH, D_MODEL), 0.1),
        'w_in': nrm((DEPTH, D_MODEL, N_IN), D_MODEL ** -0.5),
        'ml_conv_w': nrm((DEPTH, ML_CONV, 2 * ML_WIDTH), ML_CONV ** -0.5),
        'ml_conv_b': nrm((DEPTH, 2 * ML_WIDTH), 0.02),
        'ml_b_i': nrm((DEPTH, ML_HEADS), 0.1),
        'ml_b_f': jnp.linspace(3.0, 6.0, ML_HEADS, dtype=f32)[None, :] + nrm((DEPTH, ML_HEADS), 0.1),
        'ml_norm_w': 1.0 + nrm((DEPTH, ML_WIDTH), 0.1),
        'rw_mu': uni((DEPTH, RW_COLS), 0.0, 1.0),
        'rw_w0': uni((DEPTH, RW_WIDTH), -6.0, 1.0),
        'rw_w2': nrm((DEPTH, RW_LORA_W, RW_WIDTH), 0.5 * RW_LORA_W ** -0.5),
        'rw_a0': nrm((DEPTH, RW_WIDTH), 0.1),
        'rw_a2': nrm((DEPTH, RW_LORA_A, RW_WIDTH), 0.5 * RW_LORA_A ** -0.5),
        'rw_g2': nrm((DEPTH, RW_LORA_G, RW_WIDTH), RW_LORA_G ** -0.5),
        'rw_k_k': 0.85 + nrm((DEPTH, RW_WIDTH), 0.05),
        'rw_k_a': 1.0 + nrm((DEPTH, RW_WIDTH), 0.05),
        'rw_r_k': nrm((DEPTH, RW_HEADS, HEAD_DIM), 0.1),
        'rw_ln_w': 1.0 + nrm((DEPTH, RW_WIDTH), 0.1),
        'rw_ln_b': nrm((DEPTH, RW_WIDTH), 0.02),
        'nsa_cmp_pos': nrm((DEPTH, 2, CMP_LEN, HEAD_DIM), 0.1),
        'nsa_cmp_w1': nrm((DEPTH, 2, CMP_LEN * HEAD_DIM, CMP_HID), (CMP_LEN * HEAD_DIM) ** -0.5),
        'nsa_cmp_w2': nrm((DEPTH, 2, CMP_HID, HEAD_DIM), CMP_HID ** -0.5),
        'w_out': nrm((DEPTH, MIX_WIDTH, D_MODEL), MIX_WIDTH ** -0.5),
        'norm_ffn_w': 1.0 + nrm((DEPTH, D_MODEL), 0.1),
        'ffn_w1': nrm((n_dense, D_MODEL, D_FF), D_MODEL ** -0.5),
        'ffn_w3': nrm((n_dense, D_MODEL, D_FF), D_MODEL ** -0.5),
        'ffn_w2': nrm((n_dense, D_FF, D_MODEL), D_FF ** -0.5),
        'moe_router': nrm((n_moe, D_MODEL, N_EXPERTS), D_MODEL ** -0.5),
        'moe_w1': nrm((n_moe, N_EXPERTS, D_MODEL, D_EXP), D_MODEL ** -0.5),
        'moe_w3': nrm((n_moe, N_EXPERTS, D_MODEL, D_EXP), D_MODEL ** -0.5),
        'moe_w2': nrm((n_moe, N_EXPERTS, D_EXP, D_MODEL), D_EXP ** -0.5),
        'norm_final_w': 1.0 + nrm((D_MODEL,), 0.1),
    }


def reference(x_prompt, x_sample, cache_nsa_kv, cache_win_kv, state_mlstm_C, state_mlstm_n, state_mlstm_m,
              state_mlstm_conv, state_rwkv_S, state_rwkv_shift, page_table, norm_mix_w, w_in, ml_conv_w,
              ml_conv_b, ml_b_i, ml_b_f, ml_norm_w, rw_mu, rw_w0, rw_w2, rw_a0, rw_a2, rw_g2, rw_k_k, rw_k_a,
              rw_r_k, rw_ln_w, rw_ln_b, nsa_cmp_pos, nsa_cmp_w1, nsa_cmp_w2, w_out, norm_ffn_w, ffn_w1, ffn_w3,
              ffn_w2, moe_router, moe_w1, moe_w3, moe_w2, norm_final_w):
    f32 = jnp.float32
    wbuf = min(WINDOW, PAST_LEN)

    def moe(x, e_layer):
        B, T, D = x.shape
        xf = x.reshape(B * T, D)
        logits = (xf @ moe_router[e_layer]).astype(f32)
        top_v, top_i = lax.top_k(logits, TOP_K)
        gw = jax.nn.softmax(top_v, axis=-1)
        comb = jnp.sum(jax.nn.one_hot(top_i, N_EXPERTS, dtype=f32) * gw[..., None], axis=1)
        y = jnp.zeros_like(xf)
        for e in range(N_EXPERTS):
            y = y + comb[:, e:e + 1].astype(x.dtype) * swiglu(xf, moe_w1[e_layer, e], moe_w3[e_layer, e], moe_w2[e_layer, e])
        return y.reshape(B, T, D)

    def layer(l, x, offset, st):
        C0, n0, m0, conv0, S0, shift0, past_kv, win0 = st
        B, T, _ = x.shape
        dt = x.dtype
        pos = offset + jnp.arange(T)
        xn = rmsnorm(x, norm_mix_w[l])
        cols = xn @ w_in[l]
        ml_c, rw_c, ns_c = split_cols(cols, (ML_COLS, RW_COLS, NSA_COLS))
        qk_pre, ml_v, ml_o, ml_i, ml_f = split_cols(ml_c, (2 * ML_WIDTH, ML_WIDTH, ML_WIDTH, ML_HEADS, ML_HEADS))
        qk, conv_new = causal_conv(qk_pre, conv0, ml_conv_w[l], ml_conv_b[l])
        qk = jax.nn.silu(qk)
        h_ml, C, n, m = mlstm_chunkwise(heads(qk[..., :ML_WIDTH], ML_HEADS),
                                        heads(qk[..., ML_WIDTH:], ML_HEADS) * HEAD_DIM ** -0.5,
                                        heads(ml_v, ML_HEADS), ml_i + ml_b_i[l], ml_f + ml_b_f[l], C0, n0, m0)
        h_ml = head_norm(h_ml, ml_norm_w[l], ML_NORM_EPS) * jax.nn.sigmoid(ml_o.astype(f32))
        prev = jnp.concatenate([shift0[:, None].astype(rw_c.dtype), rw_c[:, :-1]], 1)
        xm = rw_c + (prev - rw_c) * rw_mu[l]
        shift_new = rw_c[:, -1]
        r, wl, k, v, al, gl = split_cols(xm, (RW_WIDTH, RW_LORA_W, RW_WIDTH, RW_WIDTH, RW_LORA_A, RW_LORA_G))
        wpre = (rw_w0[l] + jnp.tanh(wl) @ rw_w2[l]).astype(f32)
        decay = jnp.exp(-jnp.exp(-jax.nn.softplus(-wpre) - 0.5))
        a = jax.nn.sigmoid((rw_a0[l] + al @ rw_a2[l]).astype(f32))
        g = jax.nn.sigmoid(gl) @ rw_g2[l]
        kk = heads((k * rw_k_k[l]).astype(f32), RW_HEADS)
        kk = kk / jnp.maximum(jnp.sqrt(jnp.sum(kk * kk, -1, keepdims=True)), 1e-12)
        k2 = k.astype(f32) * (1.0 + (a - 1.0) * rw_k_a[l])
        rh, kh, vh = heads(r.astype(f32), RW_HEADS), heads(k2, RW_HEADS), heads(v.astype(f32), RW_HEADS)
        out, S = rwkv7_scan(rh, heads(decay, RW_HEADS), kh, vh, kk, heads(a, RW_HEADS), S0)
        bonus = jnp.sum(rh * kh * rw_r_k[l], -1, keepdims=True) * vh
        h_rw = (head_norm(out, rw_ln_w[l], RW_GN_EPS) + rw_ln_b[l] + bonus.reshape(B, T, RW_WIDTH)) * g
        nq, nkv, nwkv, ngate = split_cols(ns_c, (NSA_WIDTH, 4 * KV_WIDTH, 2 * KV_WIDTH, 3 * NSA_HEADS))
        q_raw = heads(nq, NSA_HEADS)
        q_rot = rope(q_raw, pos)
        kv4 = nkv.reshape(B, T, 4, NSA_KV_HEADS, HEAD_DIM)
        kv4 = jnp.stack([kv4[:, :, 0], kv4[:, :, 1], rope(kv4[:, :, 2], pos), kv4[:, :, 3]], 2)
        wkv = nwkv.reshape(B, T, 2, NSA_KV_HEADS, HEAD_DIM)
        wkv = jnp.stack([rope(wkv[:, :, 0], pos), wkv[:, :, 1]], 2)
        gates = jax.nn.sigmoid(ngate.reshape(B, T, NSA_HEADS, 3))
        kv_full = jnp.concatenate([past_kv.astype(dt), kv4], 1)
        win_all = jnp.concatenate([win0.astype(dt), wkv], 1)
        h_ns = nsa_attend(q_raw, q_rot, gates, kv_full, win_all, offset, nsa_cmp_pos[l], nsa_cmp_w1[l], nsa_cmp_w2[l])
        win_new = last_rows(win_all, wbuf)
        x = x + jnp.concatenate([h_ml, h_rw, h_ns], -1).astype(dt) @ w_out[l]
        xn2 = rmsnorm(x, norm_ffn_w[l])
        if l % 2 == 0:
            ff = swiglu(xn2, ffn_w1[l // 2], ffn_w3[l // 2], ffn_w2[l // 2])
        else:
            ff = moe(xn2, l // 2)
        x = x + ff
        return x, (kv4, win_new, C, n, m, conv_new, S, shift_new)

    def run(x, offset, get_state):
        news = []
        for l in range(DEPTH):
            x, nw = layer(l, x, offset, get_state(l))
            news.append(nw)
        y = rmsnorm(x, norm_final_w)
        return y, [jnp.stack([nw[i] for nw in news]) for i in range(8)]

    Bp = x_prompt.shape[0]
    dtp = x_prompt.dtype

    def prompt_state(l):
        return (jnp.zeros((Bp, ML_HEADS, HEAD_DIM, HEAD_DIM), f32), jnp.zeros((Bp, ML_HEADS, HEAD_DIM), f32),
                jnp.zeros((Bp, ML_HEADS), f32), jnp.zeros((Bp, ML_CONV - 1, 2 * ML_WIDTH), dtp),
                jnp.zeros((Bp, RW_HEADS, HEAD_DIM, HEAD_DIM), f32), jnp.zeros((Bp, RW_COLS), dtp),
                jnp.zeros((Bp, 0, 4, NSA_KV_HEADS, HEAD_DIM), dtp), jnp.zeros((Bp, 0, 2, NSA_KV_HEADS, HEAD_DIM), dtp))

    Bs = x_sample.shape[0]
    past_len = page_table.shape[1] * PAGE_SIZE

    def sample_state(l):
        past = cache_nsa_kv[l][page_table].reshape(Bs, past_len, 4, NSA_KV_HEADS, HEAD_DIM)
        return (state_mlstm_C[l], state_mlstm_n[l], state_mlstm_m[l], state_mlstm_conv[l],
                state_rwkv_S[l], state_rwkv_shift[l], past, cache_win_kv[l])

    y_prompt, pst = run(x_prompt, 0, prompt_state)
    y_sample, sst = run(x_sample, past_len, sample_state)
    p_kv, p_win, p_C, p_n, p_m, p_conv, p_S, p_shift = pst
    s_kv, s_win, s_C, s_n, s_m, s_conv, s_S, s_shift = sst
    return (y_prompt, y_sample, p_kv, s_kv, p_win, s_win, p_C, s_C, p_n, s_n, p_m, s_m,
            p_conv, s_conv, p_S, s_S, p_shift, s_shift)
```

```python
import functools
import math

import numpy as np
import jax
import jax.numpy as jnp
from jax import lax
from jax.experimental import pallas as pl
from jax.experimental.pallas import tpu as pltpu

F32 = jnp.float32
BF16 = jnp.bfloat16
HIGHEST = lax.Precision.HIGHEST

D_MODEL = 1024
HEAD_DIM = 64
ML_HEADS = 4
RW_HEADS = 4
NSA_HEADS = 8
NSA_KV_HEADS = 2
NSA_GROUP = NSA_HEADS // NSA_KV_HEADS
ML_WIDTH = ML_HEADS * HEAD_DIM
RW_WIDTH = RW_HEADS * HEAD_DIM
NSA_WIDTH = NSA_HEADS * HEAD_DIM
KV_WIDTH = NSA_KV_HEADS * HEAD_DIM
MIX_WIDTH = ML_WIDTH + RW_WIDTH + NSA_WIDTH
ML_CHUNK = 64
ML_CONV = 4
ML_NORM_EPS = 1e-6
RW_LORA_W = 64
RW_LORA_A = 64
RW_LORA_G = 128
RW_GN_EPS = 64e-5
CMP_LEN = 32
CMP_STRIDE = 16
CMP_HID = 128
SEL_BLOCK = 64
SEL_TOPN = 8
WINDOW = 512
Q_BLOCK = 64
ROPE_THETA = 10000.0
N_EXPERTS = 8
TOP_K = 2
RMS_EPS = 1e-6
NEG = -1e30
SEL_NEG = -1e9
FORCE_BONUS = 1e3
ML_COLS = 4 * ML_WIDTH + 2 * ML_HEADS
RW_COLS = 3 * RW_WIDTH + RW_LORA_W + RW_LORA_A + RW_LORA_G
NSA_COLS = NSA_WIDTH + 6 * KV_WIDTH + 3 * NSA_HEADS

LANE = 128
KV_ROW = 4 * KV_WIDTH
GRP_ROWS = CMP_STRIDE
VMEM_LIMIT = 52 * 1024 * 1024

C_ML = 0
C_RW = 1024
C_Q = 2048
C_QS = 2560
C_KV = 3072
C_SKS = 3584
C_WK = 3712
C_WV = 3840
C_WKS = 3968
C_SM = 4096
C_TOT = 4224


def _cp(sem, vmem=VMEM_LIMIT):
    return pltpu.CompilerParams(dimension_semantics=sem, vmem_limit_bytes=vmem)


def _dot(a, b, precision=None):
    return jnp.dot(a, b, preferred_element_type=F32, precision=precision)


def _dot_nt(a, b, precision=None):
    return lax.dot_general(a, b, (((1,), (1,)), ((), ())), preferred_element_type=F32, precision=precision)


def _dot_tn(a, b, precision=None):
    return lax.dot_general(a, b, (((0,), (0,)), ((), ())), preferred_element_type=F32, precision=precision)


def _sigmoid(x):
    return 1.0 / (1.0 + jnp.exp(-x))


def _silu(x):
    return x * _sigmoid(x)


def _iota(shape, dim):
    return lax.broadcasted_iota(jnp.int32, shape, dim)


def _head_ones(width):
    return (_iota((width, width), 0) // HEAD_DIM == _iota((width, width), 1) // HEAD_DIM).astype(F32)


def _masked_softmax(s, mask):
    s = jnp.where(mask, s, NEG)
    m = jnp.max(s, axis=-1, keepdims=True)
    p = jnp.where(mask, jnp.exp(s - m), 0.0)
    return p / jnp.maximum(jnp.sum(p, -1, keepdims=True), 1e-30)


def _swap_half(n_heads):
    idx = np.arange(n_heads * HEAD_DIM)
    return (idx // HEAD_DIM) * HEAD_DIM + (idx % HEAD_DIM + HEAD_DIM // 2) % HEAD_DIM


def _rw_order():
    r0, wl0 = 0, RW_WIDTH
    k0 = wl0 + RW_LORA_W
    v0 = k0 + RW_WIDTH
    al0 = v0 + RW_WIDTH
    gl0 = al0 + RW_LORA_A
    return np.concatenate([np.arange(r0, r0 + RW_WIDTH), np.arange(k0, k0 + RW_WIDTH), np.arange(v0, v0 + RW_WIDTH),
                           np.arange(wl0, wl0 + RW_LORA_W), np.arange(al0, al0 + RW_LORA_A),
                           np.arange(gl0, gl0 + RW_LORA_G)])


def _pack_cols():
    ml0, rw0, ns0 = 0, ML_COLS, ML_COLS + RW_COLS
    q0 = ns0
    kv0 = q0 + NSA_WIDTH
    sk0 = kv0 + 2 * KV_WIDTH
    wk0 = kv0 + 4 * KV_WIDTH
    wv0 = wk0 + KV_WIDTH
    gt0 = wv0 + KV_WIDTH
    small = np.concatenate([np.arange(ml0 + 4 * ML_WIDTH, ml0 + ML_COLS), np.arange(gt0, gt0 + 3 * NSA_HEADS)])
    cols = np.concatenate([
        np.arange(ml0, ml0 + 4 * ML_WIDTH),
        rw0 + _rw_order(),
        np.arange(q0, q0 + NSA_WIDTH),
        q0 + _swap_half(NSA_HEADS),
        np.arange(kv0, kv0 + 4 * KV_WIDTH),
        sk0 + _swap_half(NSA_KV_HEADS),
        np.arange(wk0, wk0 + KV_WIDTH),
        np.arange(wv0, wv0 + KV_WIDTH),
        wk0 + _swap_half(NSA_KV_HEADS),
        small,
    ])
    n_small = small.shape[0]
    assert cols.shape[0] == C_SM + n_small
    return cols, n_small


def _pack_w_in(w):
    cols, n_small = _pack_cols()
    wp = jnp.take(w, jnp.asarray(cols), axis=1)
    wp = jnp.pad(wp, ((0, 0), (0, C_TOT - C_SM - n_small)))
    return wp.astype(BF16)


def _rope_tables(pos):
    half = HEAD_DIM // 2
    inv = ROPE_THETA ** (-jnp.arange(half, dtype=F32) / half)
    ang = pos.astype(F32)[:, None] * inv[None, :]
    cos, sin = jnp.cos(ang), jnp.sin(ang)
    cos2 = jnp.concatenate([cos, cos, cos, cos], -1)
    sin2 = jnp.concatenate([-sin, sin, -sin, sin], -1)
    return cos2, sin2


def _inproj_kernel(x_ref, nw_ref, w_ref, cos_ref, sin_ref, ml_ref, rw_ref, q_ref, kv_ref, win_ref, sm_ref):
    x = x_ref[...]
    xn = x * lax.rsqrt(jnp.mean(x * x, axis=-1, keepdims=True) + RMS_EPS) * nw_ref[...]
    acc = _dot(xn.astype(BF16), w_ref[...])
    cos = cos_ref[...]
    sin = sin_ref[...]
    cos4 = jnp.concatenate([cos] * 4, axis=-1)
    sin4 = jnp.concatenate([sin] * 4, axis=-1)
    ml_ref[...] = acc[:, C_ML:C_ML + 1024]
    rw_ref[...] = acc[:, C_RW:C_RW + 1024]
    q_ref[:, 0:NSA_WIDTH] = acc[:, C_Q:C_Q + NSA_WIDTH]
    q_ref[:, NSA_WIDTH:2 * NSA_WIDTH] = acc[:, C_Q:C_Q + NSA_WIDTH] * cos4 + acc[:, C_QS:C_QS + NSA_WIDTH] * sin4
    kv_ref[:, 0:2 * KV_WIDTH] = acc[:, C_KV:C_KV + 2 * KV_WIDTH]
    kv_ref[:, 2 * KV_WIDTH:3 * KV_WIDTH] = (acc[:, C_KV + 2 * KV_WIDTH:C_KV + 3 * KV_WIDTH] * cos
                                            + acc[:, C_SKS:C_SKS + KV_WIDTH] * sin)
    kv_ref[:, 3 * KV_WIDTH:4 * KV_WIDTH] = acc[:, C_KV + 3 * KV_WIDTH:C_KV + 4 * KV_WIDTH]
    win_ref[:, 0:KV_WIDTH] = acc[:, C_WK:C_WK + KV_WIDTH] * cos + acc[:, C_WKS:C_WKS + KV_WIDTH] * sin
    win_ref[:, KV_WIDTH:2 * KV_WIDTH] = acc[:, C_WV:C_WV + KV_WIDTH]
    sm_ref[...] = acc[:, C_SM:C_SM + LANE]


def _inproj(x2, norm_w, w_packed, cos2, sin2, T):
    N = x2.shape[0]
    tm = min(256, N)
    if T >= tm:
        assert T % tm == 0
        nt = T // tm
    else:
        assert tm % T == 0
        cos2 = jnp.tile(cos2, (tm // T, 1))
        sin2 = jnp.tile(sin2, (tm // T, 1))
        nt = 1
    row = lambda w: pl.BlockSpec((tm, w), lambda i: (i, 0))
    tab = pl.BlockSpec((tm, LANE), lambda i: (i % nt, 0))
    widths = (1024, 1024, 2 * NSA_WIDTH, KV_ROW, 2 * KV_WIDTH, LANE)
    return pl.pallas_call(
        _inproj_kernel,
        grid=(N // tm,),
        in_specs=[row(D_MODEL), pl.BlockSpec((1, D_MODEL), lambda i: (0, 0)),
                  pl.BlockSpec((D_MODEL, C_TOT), lambda i: (0, 0)), tab, tab],
        out_specs=[row(w) for w in widths],
        out_shape=[jax.ShapeDtypeStruct((N, w), F32) for w in widths],
        compiler_params=_cp(("parallel",)),
        name="inproj",
    )(x2, norm_w.reshape(1, D_MODEL), w_packed, cos2, sin2)


def _mlstm_kernel(ml_ref, sm_ref, conv0_ref, c0_ref, n0_ref, m0_ref, cw_ref, cb_ref, gb_ref, nw_ref,
                  h_ref, c_ref, n_ref, m_ref, conv_ref, xp_ref, *, L):
    c = pl.program_id(1)
    nc = pl.num_programs(1)
    H = ML_HEADS
    PAD = 8

    @pl.when(c == 0)
    def _():
        c_ref[...] = c0_ref[...]
        n_ref[...] = n0_ref[...]
        m_ref[...] = m0_ref[...]
        xp_ref[PAD - (ML_CONV - 1):PAD, :] = conv0_ref[0]

    xp_ref[PAD:PAD + L, :] = ml_ref[:, 0:2 * ML_WIDTH]
    y = jnp.broadcast_to(cb_ref[...], (L, 2 * ML_WIDTH))
    for j in range(ML_CONV):
        y = y + xp_ref[PAD - (ML_CONV - 1) + j:PAD - (ML_CONV - 1) + j + L, :] * cw_ref[j:j + 1, :]
    tail = xp_ref[PAD + L - (ML_CONV - 1):PAD + L, :]
    xp_ref[PAD - (ML_CONV - 1):PAD, :] = tail
    conv_ref[0] = tail
    qk = _silu(y)

    gates = sm_ref[...] + gb_ref[...]
    lf = jnp.minimum(gates, 0.0) - jnp.log1p(jnp.exp(-jnp.abs(gates)))
    tril = (_iota((L, L), 1) <= _iota((L, L), 0))
    eye = (_iota((L, L), 1) == _iota((L, L), 0))
    bcum = _dot(tril.astype(F32), lf, HIGHEST)
    m_prev_all = m_ref[0]
    m_row = m_prev_all

    outs = []
    for h in range(H):
        lo = h * HEAD_DIM
        qh = qk[:, lo:lo + HEAD_DIM]
        kh = qk[:, ML_WIDTH + lo:ML_WIDTH + lo + HEAD_DIM] * (HEAD_DIM ** -0.5)
        vh = ml_ref[:, 2 * ML_WIDTH + lo:2 * ML_WIDTH + lo + HEAD_DIM]
        oh = ml_ref[:, 3 * ML_WIDTH + lo:3 * ML_WIDTH + lo + HEAD_DIM]
        icol = gates[:, h:h + 1]
        bcol = bcum[:, H + h:H + h + 1]
        acol = icol - bcol
        arow = jnp.sum(jnp.where(eye, acol, 0.0), axis=0, keepdims=True)
        m_prev = m_prev_all[:, h:h + 1]
        dlog = jnp.where(tril, bcol + arow, -jnp.inf)
        m_t = jnp.maximum(bcol + m_prev, jnp.max(dlog, axis=-1, keepdims=True))
        dm = jnp.exp(dlog - m_t)
        C = c_ref[0, h]
        nrow = n_ref[0, h:h + 1, :]
        S = _dot_nt(qh, kh, HIGHEST) * dm
        inter = jnp.exp(bcol + m_prev - m_t)
        num = inter * _dot_nt(qh, C, HIGHEST) + _dot(S, vh, HIGHEST)
        den = inter * jnp.sum(qh * nrow, axis=-1, keepdims=True) + jnp.sum(S, axis=-1, keepdims=True)
        hh = num / jnp.maximum(jnp.abs(den), jnp.exp(-m_t))
        m_new = m_t[L - 1:L, :]
        b_last = bcol[L - 1:L, :]
        wgt = jnp.exp(b_last + acol - m_new)
        dec = jnp.exp(b_last + m_prev - m_new)
        c_ref[0, h] = dec * C + _dot_tn(vh * wgt, kh, HIGHEST)
        n_ref[0, h:h + 1, :] = dec * nrow + jnp.sum(wgt * kh, axis=0, keepdims=True)
        m_row = jnp.where(_iota((1, LANE), 1) == h, m_new, m_row)
        mu = jnp.mean(hh, axis=-1, keepdims=True)
        var = jnp.mean(jnp.square(hh - mu), axis=-1, keepdims=True)
        hn = (hh - mu) * lax.rsqrt(var + ML_NORM_EPS) * nw_ref[:, lo:lo + HEAD_DIM]
        outs.append(hn * _sigmoid(oh))
    m_ref[0] = m_row
    h_ref[...] = jnp.concatenate(outs, axis=-1)


def _mlstm(ml, sm, conv0, C0, n0, m0, conv_w, conv_b, b_i, b_f, norm_w, B, T):
    L = math.gcd(T, ML_CHUNK)
    NC = T // L
    N = B * T
    gate_bias = jnp.zeros((1, LANE), F32).at[0, 0:ML_HEADS].set(b_i).at[0, ML_HEADS:2 * ML_HEADS].set(b_f)
    m0p = jnp.pad(m0.reshape(B, 1, ML_HEADS), ((0, 0), (0, 0), (0, LANE - ML_HEADS)))
    per_b = lambda shape: pl.BlockSpec((1,) + shape, lambda b, c: (b,) + (0,) * len(shape))
    const = lambda shape: pl.BlockSpec(shape, lambda b, c: (0,) * len(shape))
    rows = lambda w: pl.BlockSpec((L, w), lambda b, c: (b * NC + c, 0))
    state_shapes = [(ML_HEADS, HEAD_DIM, HEAD_DIM), (ML_HEADS, HEAD_DIM), (1, LANE), (ML_CONV - 1, 2 * ML_WIDTH)]
    h, C, n, m, conv = pl.pallas_call(
        functools.partial(_mlstm_kernel, L=L),
        grid=(B, NC),
        in_specs=[rows(1024), rows(LANE), per_b(state_shapes[3]), per_b(state_shapes[0]), per_b(state_shapes[1]),
                  per_b(state_shapes[2]), const((ML_CONV, 2 * ML_WIDTH)), const((1, 2 * ML_WIDTH)),
                  const((1, LANE)), const((1, ML_WIDTH))],
        out_specs=[rows(ML_WIDTH)] + [per_b(s) for s in state_shapes],
        out_shape=[jax.ShapeDtypeStruct((N, ML_WIDTH), F32)]
                  + [jax.ShapeDtypeStruct((B,) + s, F32) for s in state_shapes],
        scratch_shapes=[pltpu.VMEM((L + 8, 2 * ML_WIDTH), F32)],
        compiler_params=_cp(("parallel", "arbitrary")),
        name="mlstm",
    )(ml, sm, conv0, C0, n0, m0p, conv_w, conv_b.reshape(1, -1), gate_bias, norm_w.reshape(1, -1))
    return h, C, n, m[:, 0, :ML_HEADS], conv


def _rwkv_prep_kernel(rw_ref, shift0_ref, mu_ref, w0_ref, w2_ref, a0_ref, a2_ref, g2_ref, kk_ref, ka_ref, rk_ref,
                      p_ref, gb_ref, shift_ref, carry_ref, *, tt):
    c = pl.program_id(1)
    W = RW_WIDTH

    @pl.when(c == 0)
    def _():
        carry_ref[...] = shift0_ref[0]

    x = rw_ref[...]
    rolled = pltpu.roll(x, 1, 0)
    prev = jnp.where(_iota((tt, 1), 0) == 0, carry_ref[...], rolled)
    last = x[tt - 1:tt, :]
    carry_ref[...] = last
    shift_ref[0] = last
    xm = x + (prev - x) * mu_ref[...]
    r = xm[:, 0:W]
    k = xm[:, W:2 * W]
    v = xm[:, 2 * W:3 * W]
    wl = xm[:, 3 * W:3 * W + RW_LORA_W]
    al = xm[:, 3 * W + RW_LORA_W:3 * W + RW_LORA_W + RW_LORA_A]
    gl = xm[:, 3 * W + RW_LORA_W + RW_LORA_A:]
    wpre = w0_ref[...] + _dot(jnp.tanh(wl), w2_ref[...], HIGHEST)
    softplus_neg = jnp.maximum(-wpre, 0.0) + jnp.log1p(jnp.exp(-jnp.abs(wpre)))
    decay = jnp.exp(-jnp.exp(-softplus_neg - 0.5))
    a = _sigmoid(a0_ref[...] + _dot(al, a2_ref[...], HIGHEST))
    g = _dot(_sigmoid(gl), g2_ref[...], HIGHEST)
    ones = _head_ones(W)
    kk = k * kk_ref[...]
    kk = kk / jnp.maximum(jnp.sqrt(_dot(kk * kk, ones, HIGHEST)), 1e-12)
    k2 = k * (1.0 + (a - 1.0) * ka_ref[...])
    bonus = _dot(r * k2 * rk_ref[...], ones, HIGHEST) * v
    p_ref[:, 0:W] = r
    p_ref[:, W:2 * W] = decay
    p_ref[:, 2 * W:3 * W] = k2
    p_ref[:, 3 * W:4 * W] = v
    p_ref[:, 4 * W:5 * W] = kk
    p_ref[:, 5 * W:6 * W] = a * kk
    gb_ref[:, 0:W] = g
    gb_ref[:, W:2 * W] = bonus


def _rwkv_scan_kernel(p_ref, s0_ref, o_ref, s_ref, *, Bb, Tc):
    c = pl.program_id(1)
    D = HEAD_DIM
    NP = RW_HEADS // 2

    @pl.when(c == 0)
    def _():
        s_ref[...] = s0_ref[...]

    lane = _iota((D, LANE), 1)
    first = lane < D
    diag = (lane % D) == _iota((D, LANE), 0)

    def pair_sum(x):
        s0 = jnp.sum(jnp.where(first, x, 0.0), axis=1, keepdims=True)
        s1 = jnp.sum(jnp.where(first, 0.0, x), axis=1, keepdims=True)
        return jnp.where(first, s0, s1)

    def step(t, carry):
        for b in range(Bb):
            x = p_ref[b, t]
            for p in range(NP):
                r, w, k, v, kk, bb = (x[j * NP + p:j * NP + p + 1, :] for j in range(6))
                S = s_ref[b, p]
                vcol = pair_sum(jnp.where(diag, v, 0.0))
                sa = pair_sum(S * kk)
                S = S * w - sa * bb + vcol * k
                s_ref[b, p] = S
                y = pair_sum(S * r)
                o_ref[b, t, p:p + 1, :] = jnp.sum(jnp.where(diag, y, 0.0), axis=0, keepdims=True)
        return carry

    lax.fori_loop(0, Tc, step, 0)


def _rwkv(rw, shift0p, S0, mu_p, w0, w2, a0, a2, g2, k_k, k_a, r_k, B, T):
    N = B * T
    W = RW_WIDTH
    tt = min(T, 256)
    ntt = T // tt
    const = lambda shape: pl.BlockSpec(shape, lambda b, c: (0,) * len(shape))
    rows = lambda w: pl.BlockSpec((tt, w), lambda b, c: (b * ntt + c, 0))
    vec = lambda a: a.reshape(1, -1)
    p, gb, shift = pl.pallas_call(
        functools.partial(_rwkv_prep_kernel, tt=tt),
        grid=(B, ntt),
        in_specs=[rows(1024), pl.BlockSpec((1, 1, 1024), lambda b, c: (b, 0, 0)), const((1, 1024)),
                  const((1, W)), const((RW_LORA_W, W)), const((1, W)), const((RW_LORA_A, W)), const((RW_LORA_G, W)),
                  const((1, W)), const((1, W)), const((1, W))],
        out_specs=[rows(6 * W), rows(2 * W), pl.BlockSpec((1, 1, 1024), lambda b, c: (b, 0, 0))],
        out_shape=[jax.ShapeDtypeStruct((N, 6 * W), F32), jax.ShapeDtypeStruct((N, 2 * W), F32),
                   jax.ShapeDtypeStruct((B, 1, 1024), F32)],
        scratch_shapes=[pltpu.VMEM((1, 1024), F32)],
        compiler_params=_cp(("parallel", "arbitrary")),
        name="rwkv_prep",
    )(rw, shift0p.reshape(B, 1, 1024), vec(mu_p), vec(w0), w2, vec(a0), a2, g2, vec(k_k), vec(k_a), vec(r_k))

    Bb = 8 if B % 8 == 0 else B
    Tc = min(T, 128)
    NP = RW_HEADS // 2
    blk = lambda r: pl.BlockSpec((Bb, Tc, r, LANE), lambda b, c: (b, c, 0, 0))
    st = pl.BlockSpec((Bb, NP, HEAD_DIM, LANE), lambda b, c: (b, 0, 0, 0))
    S0p = S0.reshape(B, NP, 2, HEAD_DIM, HEAD_DIM).transpose(0, 1, 3, 2, 4).reshape(B, NP, HEAD_DIM, LANE)
    y, Sp = pl.pallas_call(
        functools.partial(_rwkv_scan_kernel, Bb=Bb, Tc=Tc),
        grid=(B // Bb, T // Tc),
        in_specs=[blk(6 * NP), st],
        out_specs=[blk(NP), st],
        out_shape=[jax.ShapeDtypeStruct((B, T, NP, LANE), F32), jax.ShapeDtypeStruct(S0p.shape, F32)],
        compiler_params=_cp(("parallel", "arbitrary")),
        name="rwkv_scan",
    )(p.reshape(B, T, 6 * NP, LANE), S0p)
    S = Sp.reshape(B, NP, HEAD_DIM, 2, HEAD_DIM).transpose(0, 1, 3, 2, 4).reshape(S0.shape)
    return y.reshape(N, W), gb, S, shift[:, 0]


def _gather_kernel(pt_ref, page_ref, new_ref, o_ref, *, n_pages, T):
    p = pl.program_id(1)

    @pl.when(p < n_pages)
    def _():
        o_ref[...] = page_ref[...]

    @pl.when(p == n_pages)
    def _():
        o_ref[...] = jnp.zeros_like(o_ref)
        o_ref[0, 0:T, :] = new_ref[0]


def _gather_pages(cache, page_ids, new_rows, B, T):
    _, page, _ = cache.shape
    n_pages = page_ids.shape[1]
    assert T <= page

    def page_map(b, p, pt):
        return (pt[b * n_pages + jnp.minimum(p, n_pages - 1)], 0, 0)

    return pl.pallas_call(
        functools.partial(_gather_kernel, n_pages=n_pages, T=T),
        grid_spec=pltpu.PrefetchScalarGridSpec(
            num_scalar_prefetch=1,
            grid=(B, n_pages + 1),
            in_specs=[pl.BlockSpec((1, page, KV_ROW), page_map),
                      pl.BlockSpec((1, T, KV_ROW), lambda b, p, pt: (b, 0, 0))],
            out_specs=pl.BlockSpec((1, page, KV_ROW), lambda b, p, pt: (b, p, 0)),
        ),
        out_shape=jax.ShapeDtypeStruct((B, (n_pages + 1) * page, KV_ROW), F32),
        compiler_params=_cp(("parallel", "arbitrary")),
        name="gather_pages",
    )(page_ids.reshape(-1), cache, new_rows)


def _compress_kernel(x_ref, posl_ref, posh_ref, wl_ref, wh_ref, w2_ref, o_ref, *, G):
    half = 2 * KV_WIDTH
    pieces = [x_ref[0, :, r * KV_ROW:r * KV_ROW + half] for r in range(GRP_ROWS)]
    xg = jnp.concatenate(pieces, axis=-1)
    lo = _dot((xg + posl_ref[...]).astype(BF16), wl_ref[...])
    hi = _dot((xg + posh_ref[...]).astype(BF16), wh_ref[...])
    hid = _silu(lo + pltpu.roll(hi, G - 1, 0))
    o_ref[0] = _dot(hid.astype(BF16), w2_ref[...])


def _compress(kv_rows, cmp_pos, cmp_w1, cmp_w2, B, G):
    rows = kv_rows.shape[1]
    xg = kv_rows.reshape(B, rows // GRP_ROWS, GRP_ROWS * KV_ROW)
    eye2 = jnp.eye(2, dtype=F32)
    w1 = cmp_w1.reshape(2, CMP_LEN, HEAD_DIM, CMP_HID)
    wbig = jnp.einsum('jrde,aj,bg->rabdjge', w1, eye2, eye2)
    wbig = wbig.reshape(CMP_LEN, 2 * KV_WIDTH, 4 * CMP_HID).astype(BF16)
    w_lo = wbig[:GRP_ROWS].reshape(GRP_ROWS * 2 * KV_WIDTH, 4 * CMP_HID)
    w_hi = wbig[GRP_ROWS:].reshape(GRP_ROWS * 2 * KV_WIDTH, 4 * CMP_HID)
    w2big = jnp.einsum('jed,aj,bg->abejgd', cmp_w2, eye2, eye2).reshape(4 * CMP_HID, 2 * KV_WIDTH).astype(BF16)
    pos = jnp.broadcast_to(cmp_pos.transpose(1, 0, 2)[:, :, None, :], (CMP_LEN, 2, NSA_KV_HEADS, HEAD_DIM))
    pos_lo = pos[:GRP_ROWS].reshape(1, -1)
    pos_hi = pos[GRP_ROWS:].reshape(1, -1)
    K = GRP_ROWS * 2 * KV_WIDTH
    const = lambda shape: pl.BlockSpec(shape, lambda b: (0,) * len(shape))
    return pl.pallas_call(
        functools.partial(_compress_kernel, G=G),
        grid=(B,),
        in_specs=[pl.BlockSpec((1, G, GRP_ROWS * KV_ROW), lambda b: (b, 0, 0)), const((1, K)), const((1, K)),
                  const((K, 4 * CMP_HID)), const((K, 4 * CMP_HID)), const((4 * CMP_HID, 2 * KV_WIDTH))],
        out_specs=pl.BlockSpec((1, G, 2 * KV_WIDTH), lambda b: (b, 0, 0)),
        out_shape=jax.ShapeDtypeStruct((B, G, 2 * KV_WIDTH), F32),
        compiler_params=_cp(("parallel",)),
        name="nsa_compress",
    )(xg, pos_lo, pos_hi, w_lo, w_hi, w2big)


def _nsa_kernel(q_ref, sm_ref, ckv_ref, kv_ref, win_ref, ov_ref, ex_ref, o_ref, *,
                QB, NS, P, WN, offset, win_pos0, win_stride_q):
    nb = pl.program_id(1)
    D = HEAD_DIM
    G = NSA_GROUP
    scale = D ** -0.5
    qs = nb * QB
    R = G * QB
    tq = offset + qs + _iota((QB, 1), 0)
    tq_r = offset + qs + (_iota((R, 1), 0) % QB)
    gates = _sigmoid(sm_ref[...])
    NCP = ckv_ref.shape[1]
    cmp_end = _iota((1, NCP), 1) * CMP_STRIDE + (CMP_LEN - 1)
    mask_c = cmp_end <= tq_r
    blk = _iota((1, LANE), 1)
    kpos = _iota((1, P), 1)
    if win_stride_q:
        wstart = pl.multiple_of(jnp.maximum(qs - WINDOW, 0), 8)
    else:
        wstart = 0
    wpos = win_pos0 + wstart + _iota((1, WN), 1)
    dlt = tq_r - wpos
    mask_w = (dlt >= 0) & (dlt < WINDOW)

    outs = []
    for g in range(NSA_KV_HEADS):
        def stack(base):
            return jnp.concatenate([q_ref[:, base + (g * G + h) * D:base + (g * G + h + 1) * D] for h in range(G)], 0)

        def gate_stack(j):
            cols = [gates[:, 2 * ML_HEADS + 3 * (g * G + h) + j:2 * ML_HEADS + 3 * (g * G + h) + j + 1] for h in range(G)]
            return jnp.concatenate(cols, 0)

        q_raw = stack(0).astype(BF16)
        q_rot = stack(NSA_WIDTH).astype(BF16)
        ck = ckv_ref[0, :, g * D:(g + 1) * D]
        cv = ckv_ref[0, :, KV_WIDTH + g * D:KV_WIDTH + (g + 1) * D]
        p_c = _masked_softmax(_dot_nt(q_raw, ck.astype(BF16)) * scale, mask_c)
        o_c = _dot(p_c.astype(BF16), cv.astype(BF16))
        psum = p_c[0:QB]
        for h in range(1, G):
            psum = psum + p_c[h * QB:(h + 1) * QB]
        imp = _dot(psum, ov_ref[...], HIGHEST)
        cur = tq // SEL_BLOCK
        avail = blk * SEL_BLOCK <= tq
        forced = (blk == 0) | (blk == cur) | (blk == cur - 1)
        score = jnp.where(avail, imp + jnp.where(forced, FORCE_BONUS, 0.0), SEL_NEG)
        cnt = jnp.zeros((QB, LANE), jnp.int32)
        for i in range(NS):
            col = score[:, i:i + 1]
            beats = (col > score) | ((col == score) & (blk > i))
            cnt = cnt + beats.astype(jnp.int32)
        sel = ((cnt < min(SEL_TOPN, NS)) & (score > 0.5 * SEL_NEG)).astype(BF16)
        selk = _dot(sel, ex_ref[...])
        vis = jnp.where(kpos <= tq, selk, 0.0)
        mask_s = jnp.concatenate([vis] * G, 0) > 0.5
        sk = kv_ref[0, :, g * D:(g + 1) * D]
        sv = kv_ref[0, :, KV_WIDTH + g * D:KV_WIDTH + (g + 1) * D]
        p_s = _masked_softmax(_dot_nt(q_rot, sk.astype(BF16)) * scale, mask_s)
        o_s = _dot(p_s.astype(BF16), sv.astype(BF16))
        wk = win_ref[0, pl.ds(wstart, WN), g * D:(g + 1) * D]
        wv = win_ref[0, pl.ds(wstart, WN), KV_WIDTH + g * D:KV_WIDTH + (g + 1) * D]
        p_w = _masked_softmax(_dot_nt(q_rot, wk.astype(BF16)) * scale, mask_w)
        o_w = _dot(p_w.astype(BF16), wv.astype(BF16))
        o = gate_stack(0) * o_c + gate_stack(1) * o_s + gate_stack(2) * o_w
        outs.extend(o[h * QB:(h + 1) * QB] for h in range(G))
    o_ref[...] = jnp.concatenate(outs, axis=-1)


def _nsa_attend(qq, sm, ckv, kv_rows, win_rows, B, T, offset, p_len, win_pos0, win_band):
    N = B * T
    QB = math.gcd(T, Q_BLOCK)
    NB = T // QB
    P = kv_rows.shape[1]
    NS = -(-p_len // SEL_BLOCK)
    NC = (p_len - CMP_LEN) // CMP_STRIDE + 1
    NCP = ckv.shape[1]
    assert NS <= LANE and NC <= NCP and P % LANE == 0
    LW = win_rows.shape[1]
    WN = WINDOW + QB if win_band else LW
    cs = np.arange(NCP) * CMP_STRIDE
    ss = np.arange(LANE) * SEL_BLOCK
    ov = np.clip(np.minimum(cs[:, None] + CMP_LEN, ss[None, :] + SEL_BLOCK) - np.maximum(cs[:, None], ss[None, :]),
                 0, None) / CMP_LEN
    ov[NC:, :] = 0.0
    ov[:, NS:] = 0.0
    expand = (np.arange(P)[None, :] // SEL_BLOCK == np.arange(LANE)[:, None]).astype(np.float32)
    const = lambda shape: pl.BlockSpec(shape, lambda b, nb: (0,) * len(shape))
    rows = lambda w: pl.BlockSpec((QB, w), lambda b, nb: (b * NB + nb, 0))
    return pl.pallas_call(
        functools.partial(_nsa_kernel, QB=QB, NS=NS, P=P, WN=WN, offset=offset, win_pos0=win_pos0,
                          win_stride_q=win_band),
        grid=(B, NB),
        in_specs=[rows(2 * NSA_WIDTH), rows(LANE),
                  pl.BlockSpec((1, NCP, 2 * KV_WIDTH), lambda b, nb: (b, 0, 0)),
                  pl.BlockSpec((1, P, 2 * KV_WIDTH), lambda b, nb: (b, 0, 1)),
                  pl.BlockSpec((1, LW, 2 * KV_WIDTH), lambda b, nb: (b, 0, 0)),
                  const((NCP, LANE)), const((LANE, P))],
        out_specs=rows(NSA_WIDTH),
        out_shape=jax.ShapeDtypeStruct((N, NSA_WIDTH), F32),
        compiler_params=_cp(("parallel", "arbitrary")),
        name="nsa_attend",
    )(qq, sm, ckv, kv_rows, win_rows, jnp.asarray(ov, F32), jnp.asarray(expand, BF16))


def _outproj_kernel(x_ref, hml_ref, yrw_ref, gb_ref, hns_ref, lnw_ref, lnb_ref, w_ref, o_ref):
    W = RW_WIDTH
    y = yrw_ref[...]
    ones = _head_ones(W) * (1.0 / HEAD_DIM)
    mu = _dot(y, ones, HIGHEST)
    var = _dot(jnp.square(y - mu), ones, HIGHEST)
    yn = (y - mu) * lax.rsqrt(var + RW_GN_EPS) * lnw_ref[...]
    h_rw = (yn + lnb_ref[...] + gb_ref[:, W:2 * W]) * gb_ref[:, 0:W]
    acc = _dot(hml_ref[...].astype(BF16), w_ref[0:ML_WIDTH, :])
    acc = acc + _dot(h_rw.astype(BF16), w_ref[ML_WIDTH:ML_WIDTH + W, :])
    acc = acc + _dot(hns_ref[...].astype(BF16), w_ref[ML_WIDTH + W:, :])
    o_ref[...] = x_ref[...] + acc


def _outproj(x2, h_ml, y_rw, gb, h_ns, ln_w, ln_b, w_out):
    N = x2.shape[0]
    tm = min(512, N)
    row = lambda w: pl.BlockSpec((tm, w), lambda i: (i, 0))
    const = lambda shape: pl.BlockSpec(shape, lambda i: (0,) * len(shape))
    return pl.pallas_call(
        _outproj_kernel,
        grid=(N // tm,),
        in_specs=[row(D_MODEL), row(ML_WIDTH), row(RW_WIDTH), row(2 * RW_WIDTH), row(NSA_WIDTH),
                  const((1, RW_WIDTH)), const((1, RW_WIDTH)), const((MIX_WIDTH, D_MODEL))],
        out_specs=row(D_MODEL),
        out_shape=jax.ShapeDtypeStruct((N, D_MODEL), F32),
        compiler_params=_cp(("parallel",)),
        name="outproj",
    )(x2, h_ml, y_rw, gb, h_ns, ln_w.reshape(1, -1), ln_b.reshape(1, -1), w_out.astype(BF16))


def _rms(x, w):
    return x * lax.rsqrt(jnp.mean(x * x, axis=-1, keepdims=True) + RMS_EPS) * w


def _ffn_kernel(x_ref, nw_ref, w1_ref, w3_ref, w2_ref, fw_ref, o_ref, xn_ref, *, final_norm):
    j = pl.program_id(1)

    @pl.when(j == 0)
    def _():
        x = x_ref[...]
        xn_ref[...] = _rms(x, nw_ref[...]).astype(BF16)
        o_ref[...] = x

    xn = xn_ref[...]
    hid = _silu(_dot(xn, w1_ref[...])) * _dot(xn, w3_ref[...])
    o_ref[...] += _dot(hid.astype(BF16), w2_ref[...])

    if final_norm:
        @pl.when(j == pl.num_programs(1) - 1)
        def _():
            o_ref[...] = _rms(o_ref[...], fw_ref[...])


def _ffn(x2, norm_w, w1, w3, w2, final_w):
    N = x2.shape[0]
    d_ff = w1.shape[1]
    tm = min(512, N)
    nj = 2 if (d_ff // 2) % LANE == 0 else 1
    tf = d_ff // nj
    final_norm = final_w is not None
    fw = (final_w if final_norm else jnp.ones((D_MODEL,), F32)).reshape(1, D_MODEL)
    return pl.pallas_call(
        functools.partial(_ffn_kernel, final_norm=final_norm),
        grid=(N // tm, nj),
        in_specs=[pl.BlockSpec((tm, D_MODEL), lambda i, j: (i, 0)), pl.BlockSpec((1, D_MODEL), lambda i, j: (0, 0)),
                  pl.BlockSpec((D_MODEL, tf), lambda i, j: (0, j)), pl.BlockSpec((D_MODEL, tf), lambda i, j: (0, j)),
                  pl.BlockSpec((tf, D_MODEL), lambda i, j: (j, 0)), pl.BlockSpec((1, D_MODEL), lambda i, j: (0, 0))],
        out_specs=pl.BlockSpec((tm, D_MODEL), lambda i, j: (i, 0)),
        out_shape=jax.ShapeDtypeStruct((N, D_MODEL), F32),
        scratch_shapes=[pltpu.VMEM((tm, D_MODEL), BF16)],
        compiler_params=_cp(("parallel", "arbitrary")),
        name="ffn",
    )(x2, norm_w.reshape(1, D_MODEL), w1.astype(BF16), w3.astype(BF16), w2.astype(BF16), fw)


def _moe_kernel(x_ref, nw_ref, rt_ref, w1_ref, w3_ref, w2_ref, fw_ref, o_ref, xn_ref, comb_ref, *, final_norm):
    e = pl.program_id(1)

    @pl.when(e == 0)
    def _():
        x = x_ref[...]
        xn = _rms(x, nw_ref[...])
        xn_ref[...] = xn.astype(BF16)
        o_ref[...] = x
        lane = _iota((1, LANE), 1).astype(F32)
        logits = jnp.where(lane < N_EXPERTS, _dot(xn, rt_ref[...], HIGHEST), -jnp.inf)
        m1 = jnp.max(logits, axis=-1, keepdims=True)
        i1 = jnp.min(jnp.where(logits == m1, lane, float(LANE)), axis=-1, keepdims=True)
        rest = jnp.where(lane == i1, -jnp.inf, logits)
        m2 = jnp.max(rest, axis=-1, keepdims=True)
        i2 = jnp.min(jnp.where(rest == m2, lane, float(LANE)), axis=-1, keepdims=True)
        e2 = jnp.exp(m2 - m1)
        den = 1.0 + e2
        comb_ref[...] = jnp.where(lane == i1, 1.0 / den, 0.0) + jnp.where(lane == i2, e2 / den, 0.0)

    xn = xn_ref[...]
    comb = comb_ref[...]
    ce = jnp.sum(jnp.where(_iota((1, LANE), 1) == e, comb, 0.0), axis=-1, keepdims=True)
    hid = _silu(_dot(xn, w1_ref[0])) * _dot(xn, w3_ref[0])
    o_ref[...] += ce * _dot(hid.astype(BF16), w2_ref[0])

    if final_norm:
        @pl.when(e == pl.num_programs(1) - 1)
        def _():
            o_ref[...] = _rms(o_ref[...], fw_ref[...])


def _moe(x2, norm_w, router, w1, w3, w2, final_w):
    N = x2.shape[0]
    E, _, d_exp = w1.shape
    assert E == N_EXPERTS and TOP_K == 2
    tm = min(512, N)
    final_norm = final_w is not None
    fw = (final_w if final_norm else jnp.ones((D_MODEL,), F32)).reshape(1, D_MODEL)
    rt = jnp.pad(router, ((0, 0), (0, LANE - E)))
    return pl.pallas_call(
        functools.partial(_moe_kernel, final_norm=final_norm),
        grid=(N // tm, E),
        in_specs=[pl.BlockSpec((tm, D_MODEL), lambda i, e: (i, 0)), pl.BlockSpec((1, D_MODEL), lambda i, e: (0, 0)),
                  pl.BlockSpec((D_MODEL, LANE), lambda i, e: (0, 0)),
                  pl.BlockSpec((1, D_MODEL, d_exp), lambda i, e: (e, 0, 0)),
                  pl.BlockSpec((1, D_MODEL, d_exp), lambda i, e: (e, 0, 0)),
                  pl.BlockSpec((1, d_exp, D_MODEL), lambda i, e: (e, 0, 0)),
                  pl.BlockSpec((1, D_MODEL), lambda i, e: (0, 0))],
        out_specs=pl.BlockSpec((tm, D_MODEL), lambda i, e: (i, 0)),
        out_shape=jax.ShapeDtypeStruct((N, D_MODEL), F32),
        scratch_shapes=[pltpu.VMEM((tm, D_MODEL), BF16), pltpu.VMEM((tm, LANE), F32)],
        compiler_params=_cp(("parallel", "arbitrary")),
        name="moe",
    )(x2, norm_w.reshape(1, D_MODEL), rt, w1.astype(BF16), w3.astype(BF16), w2.astype(BF16), fw)


def _layer(l, x2, B, T, offset, st, P, final_w):
    C0, n0, m0, conv0, S0, shift0, cache, page_ids, win0 = st
    rw_perm = jnp.asarray(_rw_order())
    pos = offset + jnp.arange(T)
    cos2, sin2 = _rope_tables(pos)
    ml, rw, qq, kv4, wrow, sm = _inproj(x2, P['norm_mix_w'][l], _pack_w_in(P['w_in'][l]), cos2, sin2, T)

    h_ml, C, n, m, conv = _mlstm(ml, sm, conv0, C0, n0, m0, P['ml_conv_w'][l], P['ml_conv_b'][l],
                                 P['ml_b_i'][l], P['ml_b_f'][l], P['ml_norm_w'][l], B, T)

    y_rw, gb, S, shift_p = _rwkv(rw, shift0[:, rw_perm], S0, P['rw_mu'][l][rw_perm], P['rw_w0'][l], P['rw_w2'][l],
                                 P['rw_a0'][l], P['rw_a2'][l], P['rw_g2'][l], P['rw_k_k'][l], P['rw_k_a'][l],
                                 P['rw_r_k'][l].reshape(-1), B, T)
    shift_new = jnp.zeros_like(shift_p).at[:, rw_perm].set(shift_p)

    kv3 = kv4.reshape(B, T, KV_ROW)
    w3 = wrow.reshape(B, T, 2 * KV_WIDTH)
    if cache is None:
        p_len = T
        kv_rows = kv3
        win_all = w3
        win_rows, win_pos0, win_band = w3, 0, True
    else:
        p_len = page_ids.shape[1] * cache.shape[1] + T
        kv_rows = _gather_pages(cache, page_ids, kv3, B, T)
        win_all = jnp.concatenate([win0, w3], axis=1)
        lw = win_all.shape[1]
        win_rows = jnp.pad(win_all, ((0, 0), (0, -lw % LANE), (0, 0)))
        win_pos0, win_band = offset - win0.shape[1], False
    n_cmp = (p_len - CMP_LEN) // CMP_STRIDE + 1
    G = -(-(n_cmp + 1) // 8) * 8
    assert G * GRP_ROWS <= kv_rows.shape[1]
    ckv = _compress(kv_rows, P['nsa_cmp_pos'][l], P['nsa_cmp_w1'][l], P['nsa_cmp_w2'][l], B, G)
    h_ns = _nsa_attend(qq, sm, ckv, kv_rows, win_rows, B, T, offset, p_len, win_pos0, win_band)
    wbuf = P['wbuf']
    lw = win_all.shape[1]
    if lw >= wbuf:
        win_new = win_all[:, lw - wbuf:]
    else:
        win_new = jnp.pad(win_all, ((0, 0), (wbuf - lw, 0), (0, 0)))

    x2 = _outproj(x2, h_ml, y_rw, gb, h_ns, P['rw_ln_w'][l], P['rw_ln_b'][l], P['w_out'][l])
    if l % 2 == 0:
        x2 = _ffn(x2, P['norm_ffn_w'][l], P['ffn_w1'][l // 2], P['ffn_w3'][l // 2], P['ffn_w2'][l // 2], final_w)
    else:
        x2 = _moe(x2, P['norm_ffn_w'][l], P['moe_router'][l // 2], P['moe_w1'][l // 2], P['moe_w3'][l // 2],
                  P['moe_w2'][l // 2], final_w)
    new = (kv3.reshape(B, T, 4, NSA_KV_HEADS, HEAD_DIM), win_new.reshape(B, wbuf, 2, NSA_KV_HEADS, HEAD_DIM),
           C, n, m, conv, S, shift_new)
    return x2, new


def _run(x, offset, get_state, P, depth):
    B, T, _ = x.shape
    x2 = x.reshape(B * T, D_MODEL)
    news = []
    for l in range(depth):
        x2, nw = _layer(l, x2, B, T, offset, get_state(l), P, P['norm_final_w'] if l == depth - 1 else None)
        news.append(nw)
    return x2.reshape(B, T, D_MODEL), [jnp.stack([nw[i] for nw in news]) for i in range(8)]


def kernel(x_prompt, x_sample, cache_nsa_kv, cache_win_kv, state_mlstm_C, state_mlstm_n, state_mlstm_m, state_mlstm_conv, state_rwkv_S, state_rwkv_shift, page_table, norm_mix_w, w_in, ml_conv_w, ml_conv_b, ml_b_i, ml_b_f, ml_norm_w, rw_mu, rw_w0, rw_w2, rw_a0, rw_a2, rw_g2, rw_k_k, rw_k_a, rw_r_k, rw_ln_w, rw_ln_b, nsa_cmp_pos, nsa_cmp_w1, nsa_cmp_w2, w_out, norm_ffn_w, ffn_w1, ffn_w3, ffn_w2, moe_router, moe_w1, moe_w3, moe_w2, norm_final_w):
    depth = w_in.shape[0]
    Bp = x_prompt.shape[0]
    Bs = x_sample.shape[0]
    n_pool, page = cache_nsa_kv.shape[1], cache_nsa_kv.shape[2]
    past_len = page_table.shape[1] * page
    wbuf = min(WINDOW, past_len)
    P = dict(norm_mix_w=norm_mix_w, w_in=w_in, ml_conv_w=ml_conv_w, ml_conv_b=ml_conv_b, ml_b_i=ml_b_i, ml_b_f=ml_b_f,
             ml_norm_w=ml_norm_w, rw_mu=rw_mu, rw_w0=rw_w0, rw_w2=rw_w2, rw_a0=rw_a0, rw_a2=rw_a2, rw_g2=rw_g2,
             rw_k_k=rw_k_k, rw_k_a=rw_k_a, rw_r_k=rw_r_k, rw_ln_w=rw_ln_w, rw_ln_b=rw_ln_b, nsa_cmp_pos=nsa_cmp_pos,
             nsa_cmp_w1=nsa_cmp_w1, nsa_cmp_w2=nsa_cmp_w2, w_out=w_out, norm_ffn_w=norm_ffn_w, ffn_w1=ffn_w1,
             ffn_w3=ffn_w3, ffn_w2=ffn_w2, moe_router=moe_router, moe_w1=moe_w1, moe_w3=moe_w3, moe_w2=moe_w2,
             norm_final_w=norm_final_w, wbuf=wbuf)

    def prompt_state(l):
        z = lambda *s: jnp.zeros((Bp,) + s, F32)
        return (z(ML_HEADS, HEAD_DIM, HEAD_DIM), z(ML_HEADS, HEAD_DIM), z(ML_HEADS), z(ML_CONV - 1, 2 * ML_WIDTH),
                z(RW_HEADS, HEAD_DIM, HEAD_DIM), z(RW_COLS), None, None, None)

    def sample_state(l):
        return (state_mlstm_C[l], state_mlstm_n[l], state_mlstm_m[l], state_mlstm_conv[l], state_rwkv_S[l],
                state_rwkv_shift[l], cache_nsa_kv.reshape(depth * n_pool, page, KV_ROW), page_table + l * n_pool,
                cache_win_kv[l].reshape(Bs, -1, 2 * KV_WIDTH))

    y_prompt, pst = _run(x_prompt, 0, prompt_state, P, depth)
    y_sample, sst = _run(x_sample, past_len, sample_state, P, depth)
    p_kv, p_win, p_C, p_n, p_m, p_conv, p_S, p_shift = pst
    s_kv, s_win, s_C, s_n, s_m, s_conv, s_S, s_shift = sst
    return (y_prompt, y_sample, p_kv, s_kv, p_win, s_win, p_C, s_C, p_n, s_n, p_m, s_m,
            p_conv, s_conv, p_S, s_S, p_shift, s_shift)
```

```python
import functools
import math

import numpy as np
import jax
import jax.numpy as jnp
from jax import lax
from jax.experimental import pallas as pl
from jax.experimental.pallas import tpu as pltpu

F32 = jnp.float32
BF16 = jnp.bfloat16
HIGHEST = lax.Precision.HIGHEST

D_MODEL = 1024
HEAD_DIM = 64
ML_HEADS = 4
RW_HEADS = 4
NSA_HEADS = 8
NSA_KV_HEADS = 2
NSA_GROUP = NSA_HEADS // NSA_KV_HEADS
ML_WIDTH = ML_HEADS * HEAD_DIM
RW_WIDTH = RW_HEADS * HEAD_DIM
NSA_WIDTH = NSA_HEADS * HEAD_DIM
KV_WIDTH = NSA_KV_HEADS * HEAD_DIM
MIX_WIDTH = ML_WIDTH + RW_WIDTH + NSA_WIDTH
ML_CHUNK = 64
ML_CONV = 4
ML_NORM_EPS = 1e-6
RW_LORA_W = 64
RW_LORA_A = 64
RW_LORA_G = 128
RW_GN_EPS = 64e-5
RW_CHUNK = 64
CMP_LEN = 32
CMP_STRIDE = 16
CMP_HID = 128
SEL_BLOCK = 64
SEL_TOPN = 8
WINDOW = 512
Q_BLOCK = 64
ROPE_THETA = 10000.0
N_EXPERTS = 8
TOP_K = 2
RMS_EPS = 1e-6
NEG = -1e30
SEL_NEG = -1e9
FORCE_BONUS = 1e3
ML_COLS = 4 * ML_WIDTH + 2 * ML_HEADS
RW_COLS = 3 * RW_WIDTH + RW_LORA_W + RW_LORA_A + RW_LORA_G
NSA_COLS = NSA_WIDTH + 6 * KV_WIDTH + 3 * NSA_HEADS

LANE = 128
KV_ROW = 4 * KV_WIDTH
GRP_ROWS = CMP_STRIDE
VMEM_LIMIT = 52 * 1024 * 1024

C_ML = 0
C_RW = 1024
C_Q = 2048
C_QS = 2560
C_KV = 3072
C_SKS = 3584
C_WK = 3712
C_WV = 3840
C_WKS = 3968
C_SM = 4096
C_TOT = 4224


def _cp(sem, vmem=VMEM_LIMIT):
    return pltpu.CompilerParams(dimension_semantics=sem, vmem_limit_bytes=vmem)


def _dot(a, b, precision=None):
    return jnp.dot(a, b, preferred_element_type=F32, precision=precision)


def _dot_nt(a, b, precision=None):
    return lax.dot_general(a, b, (((1,), (1,)), ((), ())), preferred_element_type=F32, precision=precision)


def _dot_tn(a, b, precision=None):
    return lax.dot_general(a, b, (((0,), (0,)), ((), ())), preferred_element_type=F32, precision=precision)


def _sigmoid(x):
    return 1.0 / (1.0 + jnp.exp(-x))


def _silu(x):
    return x * _sigmoid(x)


def _iota(shape, dim):
    return lax.broadcasted_iota(jnp.int32, shape, dim)


def _head_ones(width):
    return (_iota((width, width), 0) // HEAD_DIM == _iota((width, width), 1) // HEAD_DIM).astype(F32)


def _masked_softmax(s, mask):
    s = jnp.where(mask, s, NEG)
    m = jnp.max(s, axis=-1, keepdims=True)
    p = jnp.where(mask, jnp.exp(s - m), 0.0)
    return p / jnp.maximum(jnp.sum(p, -1, keepdims=True), 1e-30)


def _swap_half(n_heads):
    idx = np.arange(n_heads * HEAD_DIM)
    return (idx // HEAD_DIM) * HEAD_DIM + (idx % HEAD_DIM + HEAD_DIM // 2) % HEAD_DIM


def _rw_order():
    r0, wl0 = 0, RW_WIDTH
    k0 = wl0 + RW_LORA_W
    v0 = k0 + RW_WIDTH
    al0 = v0 + RW_WIDTH
    gl0 = al0 + RW_LORA_A
    return np.concatenate([np.arange(r0, r0 + RW_WIDTH), np.arange(k0, k0 + RW_WIDTH), np.arange(v0, v0 + RW_WIDTH),
                           np.arange(wl0, wl0 + RW_LORA_W), np.arange(al0, al0 + RW_LORA_A),
                           np.arange(gl0, gl0 + RW_LORA_G)])


def _pack_cols():
    ml0, rw0, ns0 = 0, ML_COLS, ML_COLS + RW_COLS
    q0 = ns0
    kv0 = q0 + NSA_WIDTH
    sk0 = kv0 + 2 * KV_WIDTH
    wk0 = kv0 + 4 * KV_WIDTH
    wv0 = wk0 + KV_WIDTH
    gt0 = wv0 + KV_WIDTH
    small = np.concatenate([np.arange(ml0 + 4 * ML_WIDTH, ml0 + ML_COLS), np.arange(gt0, gt0 + 3 * NSA_HEADS)])
    cols = np.concatenate([
        np.arange(ml0, ml0 + 4 * ML_WIDTH),
        rw0 + _rw_order(),
        np.arange(q0, q0 + NSA_WIDTH),
        q0 + _swap_half(NSA_HEADS),
        np.arange(kv0, kv0 + 4 * KV_WIDTH),
        sk0 + _swap_half(NSA_KV_HEADS),
        np.arange(wk0, wk0 + KV_WIDTH),
        np.arange(wv0, wv0 + KV_WIDTH),
        wk0 + _swap_half(NSA_KV_HEADS),
        small,
    ])
    n_small = small.shape[0]
    assert cols.shape[0] == C_SM + n_small
    return cols, n_small


def _pack_w_in(w):
    cols, n_small = _pack_cols()
    wp = jnp.take(w, jnp.asarray(cols), axis=1)
    wp = jnp.pad(wp, ((0, 0), (0, C_TOT - C_SM - n_small)))
    return wp.astype(BF16)


def _rope_tables(pos):
    half = HEAD_DIM // 2
    inv = ROPE_THETA ** (-jnp.arange(half, dtype=F32) / half)
    ang = pos.astype(F32)[:, None] * inv[None, :]
    cos, sin = jnp.cos(ang), jnp.sin(ang)
    cos2 = jnp.concatenate([cos, cos, cos, cos], -1)
    sin2 = jnp.concatenate([-sin, sin, -sin, sin], -1)
    return cos2, sin2


def _inproj_kernel(x_ref, nw_ref, w_ref, cos_ref, sin_ref, ml_ref, rw_ref, q_ref, kv_ref, win_ref, sm_ref):
    x = x_ref[...]
    xn = x * lax.rsqrt(jnp.mean(x * x, axis=-1, keepdims=True) + RMS_EPS) * nw_ref[...]
    acc = _dot(xn.astype(BF16), w_ref[...])
    cos = cos_ref[...]
    sin = sin_ref[...]
    cos4 = jnp.concatenate([cos] * 4, axis=-1)
    sin4 = jnp.concatenate([sin] * 4, axis=-1)
    ml_ref[...] = acc[:, C_ML:C_ML + 1024]
    rw_ref[...] = acc[:, C_RW:C_RW + 1024]
    q_ref[:, 0:NSA_WIDTH] = acc[:, C_Q:C_Q + NSA_WIDTH]
    q_ref[:, NSA_WIDTH:2 * NSA_WIDTH] = acc[:, C_Q:C_Q + NSA_WIDTH] * cos4 + acc[:, C_QS:C_QS + NSA_WIDTH] * sin4
    kv_ref[:, 0:2 * KV_WIDTH] = acc[:, C_KV:C_KV + 2 * KV_WIDTH]
    kv_ref[:, 2 * KV_WIDTH:3 * KV_WIDTH] = (acc[:, C_KV + 2 * KV_WIDTH:C_KV + 3 * KV_WIDTH] * cos
                                            + acc[:, C_SKS:C_SKS + KV_WIDTH] * sin)
    kv_ref[:, 3 * KV_WIDTH:4 * KV_WIDTH] = acc[:, C_KV + 3 * KV_WIDTH:C_KV + 4 * KV_WIDTH]
    win_ref[:, 0:KV_WIDTH] = acc[:, C_WK:C_WK + KV_WIDTH] * cos + acc[:, C_WKS:C_WKS + KV_WIDTH] * sin
    win_ref[:, KV_WIDTH:2 * KV_WIDTH] = acc[:, C_WV:C_WV + KV_WIDTH]
    sm_ref[...] = acc[:, C_SM:C_SM + LANE]


def _inproj(x2, norm_w, w_packed, cos2, sin2, T):
    N = x2.shape[0]
    tm = min(256, N)
    if T >= tm:
        assert T % tm == 0
        nt = T // tm
    else:
        assert tm % T == 0
        cos2 = jnp.tile(cos2, (tm // T, 1))
        sin2 = jnp.tile(sin2, (tm // T, 1))
        nt = 1
    row = lambda w: pl.BlockSpec((tm, w), lambda i: (i, 0))
    tab = pl.BlockSpec((tm, LANE), lambda i: (i % nt, 0))
    widths = (1024, 1024, 2 * NSA_WIDTH, KV_ROW, 2 * KV_WIDTH, LANE)
    return pl.pallas_call(
        _inproj_kernel,
        grid=(N // tm,),
        in_specs=[row(D_MODEL), pl.BlockSpec((1, D_MODEL), lambda i: (0, 0)),
                  pl.BlockSpec((D_MODEL, C_TOT), lambda i: (0, 0)), tab, tab],
        out_specs=[row(w) for w in widths],
        out_shape=[jax.ShapeDtypeStruct((N, w), F32) for w in widths],
        compiler_params=_cp(("parallel",)),
        name="inproj",
    )(x2, norm_w.reshape(1, D_MODEL), w_packed, cos2, sin2)


def _mlstm_kernel(ml_ref, sm_ref, conv0_ref, c0_ref, n0_ref, m0_ref, cw_ref, cb_ref, gb_ref, nw_ref,
                  h_ref, c_ref, n_ref, m_ref, conv_ref, xp_ref, *, Bb, L):
    c = pl.program_id(1)
    H, D = ML_HEADS, HEAD_DIM
    PAD = 8
    KC = ML_CONV - 1

    @pl.when(c == 0)
    def _():
        c_ref[...] = c0_ref[...]
        n_ref[...] = n0_ref[...]
        m_ref[...] = m0_ref[...]
        xp_ref[:, PAD - KC:PAD, :] = conv0_ref[...]

    tril = (_iota((L, L), 1) <= _iota((L, L), 0))
    eye = (_iota((L, L), 1) == _iota((L, L), 0))
    tril_b = jnp.where(tril, 1.0, 0.0).astype(BF16)

    bs = [dict(b=b) for b in range(Bb)]
    for s in bs:
        b = s['b']
        xp_ref[b, PAD:PAD + L, :] = ml_ref[b, :, 0:2 * ML_WIDTH]
    for s in bs:
        b = s['b']
        y = jnp.broadcast_to(cb_ref[...], (L, 2 * ML_WIDTH))
        for j in range(ML_CONV):
            y = y + xp_ref[b, PAD - KC + j:PAD - KC + j + L, :] * cw_ref[j:j + 1, :]
        s['qk'] = _silu(y)
        gates = sm_ref[b] + gb_ref[...]
        lf = jnp.minimum(gates, 0.0) - jnp.log1p(jnp.exp(-jnp.abs(gates)))
        lf_hi = lf.astype(BF16)
        lf_lo = (lf - lf_hi.astype(F32)).astype(BF16)
        s['gates'] = gates
        s['bcum'] = _dot(tril_b, lf_hi) + _dot(tril_b, lf_lo)
    for s in bs:
        b = s['b']
        tail = xp_ref[b, PAD + L - KC:PAD + L, :]
        xp_ref[b, PAD - KC:PAD, :] = tail
        conv_ref[b] = tail
    for s in bs:
        s['b_all'] = pltpu.roll(s['bcum'], LANE - H, 1)
        s['a_all'] = s['gates'] - s['b_all']
        s['cmax'] = s['a_all']
    step = 1
    while step < L:
        for s in bs:
            s['cmax'] = jnp.where(_iota((L, 1), 0) >= step, jnp.maximum(s['cmax'], pltpu.roll(s['cmax'], step, 0)),
                                  s['cmax'])
        step *= 2
    units = []
    for s in bs:
        b, b_all, a_all = s['b'], s['b_all'], s['a_all']
        m_prev = m_ref[b]
        m_t = b_all + jnp.maximum(m_prev, s['cmax'])
        m_new = m_t[L - 1:L, :]
        b_last = b_all[L - 1:L, :]
        m_ref[b] = m_new
        inter = jnp.exp(b_all + m_prev - m_t)
        wgt = jnp.exp(b_last + a_all - m_new)
        dec = jnp.exp(b_last + m_prev - m_new)
        ccol = b_all - m_t
        einv = jnp.exp(-m_t)
        qk = s['qk']
        for h in range(H):
            lo = h * D
            col = lambda x: x[:, h:h + 1]
            units.append(dict(b=b, h=h, lo=lo, q=qk[:, lo:lo + D], k=qk[:, ML_WIDTH + lo:ML_WIDTH + lo + D] * (D ** -0.5),
                              v=ml_ref[b, :, 2 * ML_WIDTH + lo:2 * ML_WIDTH + lo + D], acol=col(a_all),
                              ccol=col(ccol), inter=col(inter), wgt=col(wgt), dec=col(dec), einv=col(einv)))

    ones_col = jnp.ones((L, D), BF16)
    for u in units:
        u['arow'] = jnp.sum(jnp.where(eye, u['acol'], 0.0), axis=0, keepdims=True)
    for u in units:
        u['dm'] = jnp.exp(jnp.where(tril, u['ccol'] + u['arow'], -jnp.inf))
    for u in units:
        C = c_ref[u['b'], u['h']]
        qb, kb = u['q'].astype(BF16), u['k'].astype(BF16)
        if 2 * L == LANE:
            m1 = _dot_nt(qb, jnp.concatenate([kb, C.astype(BF16)], axis=0))
            u['s_raw'], u['qc'] = m1[:, 0:L], m1[:, L:]
        else:
            u['s_raw'], u['qc'] = _dot_nt(qb, kb), _dot_nt(qb, C.astype(BF16))
        c_ref[u['b'], u['h']] = u['dec'] * C + _dot_tn((u['v'] * u['wgt']).astype(BF16), kb)
        nrow = n_ref[u['b'], u['h']:u['h'] + 1, :]
        u['qn'] = jnp.sum(u['q'] * nrow, axis=-1, keepdims=True)
        n_ref[u['b'], u['h']:u['h'] + 1, :] = u['dec'] * nrow + jnp.sum(u['wgt'] * u['k'], axis=0, keepdims=True)
    for u in units:
        S = (u['s_raw'] * u['dm']).astype(BF16)
        sv = _dot(S, jnp.concatenate([u['v'].astype(BF16), ones_col], axis=1))
        den = u['inter'] * u['qn'] + sv[:, D:D + 1]
        u['hh'] = (u['inter'] * u['qc'] + sv[:, 0:D]) / jnp.maximum(jnp.abs(den), u['einv'])
    ones = _head_ones(ML_WIDTH) * (1.0 / D)
    for s in bs:
        s['hh'] = jnp.concatenate([u['hh'] for u in units if u['b'] == s['b']], axis=-1)
        s['mu'] = _dot(s['hh'], ones, HIGHEST)
    for s in bs:
        s['var'] = _dot(jnp.square(s['hh'] - s['mu']), ones, HIGHEST)
    for s in bs:
        b = s['b']
        hn = (s['hh'] - s['mu']) * lax.rsqrt(s['var'] + ML_NORM_EPS) * nw_ref[...]
        h_ref[b] = hn * _sigmoid(ml_ref[b, :, 3 * ML_WIDTH:4 * ML_WIDTH])


def _mlstm(ml, sm, conv0, C0, n0, m0, conv_w, conv_b, b_i, b_f, norm_w, B, T):
    L = math.gcd(T, ML_CHUNK)
    NC = T // L
    N = B * T
    Bb = 8 if B % 8 == 0 else B
    gate_bias = jnp.zeros((1, LANE), F32).at[0, 0:ML_HEADS].set(b_i).at[0, ML_HEADS:2 * ML_HEADS].set(b_f)
    m0p = jnp.pad(m0.reshape(B, 1, ML_HEADS), ((0, 0), (0, 0), (0, LANE - ML_HEADS)))
    per_b = lambda shape: pl.BlockSpec((Bb,) + shape, lambda b, c: (b,) + (0,) * len(shape))
    const = lambda shape: pl.BlockSpec(shape, lambda b, c: (0,) * len(shape))
    rows = lambda w: pl.BlockSpec((Bb, L, w), lambda b, c: (b, c, 0))
    state_shapes = [(ML_HEADS, HEAD_DIM, HEAD_DIM), (ML_HEADS, HEAD_DIM), (1, LANE), (ML_CONV - 1, 2 * ML_WIDTH)]
    h, C, n, m, conv = pl.pallas_call(
        functools.partial(_mlstm_kernel, Bb=Bb, L=L),
        grid=(B // Bb, NC),
        in_specs=[rows(1024), rows(LANE), per_b(state_shapes[3]), per_b(state_shapes[0]), per_b(state_shapes[1]),
                  per_b(state_shapes[2]), const((ML_CONV, 2 * ML_WIDTH)), const((1, 2 * ML_WIDTH)),
                  const((1, LANE)), const((1, ML_WIDTH))],
        out_specs=[rows(ML_WIDTH)] + [per_b(s) for s in state_shapes],
        out_shape=[jax.ShapeDtypeStruct((B, T, ML_WIDTH), F32)]
                  + [jax.ShapeDtypeStruct((B,) + s, F32) for s in state_shapes],
        scratch_shapes=[pltpu.VMEM((Bb, L + 8, 2 * ML_WIDTH), F32)],
        compiler_params=_cp(("parallel", "arbitrary")),
        name="mlstm",
    )(ml.reshape(B, T, 1024), sm.reshape(B, T, LANE), conv0, C0, n0, m0p, conv_w, conv_b.reshape(1, -1), gate_bias,
      norm_w.reshape(1, -1))
    return h.reshape(N, ML_WIDTH), C, n, m[:, 0, :ML_HEADS], conv


def _rwkv_prep_kernel(rw_ref, shift0_ref, mu_ref, w0_ref, w2_ref, a0_ref, a2_ref, g2_ref, kk_ref, ka_ref, rk_ref,
                      p_ref, gb_ref, shift_ref, carry_ref, *, tt):
    c = pl.program_id(1)
    W = RW_WIDTH

    @pl.when(c == 0)
    def _():
        carry_ref[...] = shift0_ref[0]

    x = rw_ref[...]
    rolled = pltpu.roll(x, 1, 0)
    prev = jnp.where(_iota((tt, 1), 0) == 0, carry_ref[...], rolled)
    last = x[tt - 1:tt, :]
    carry_ref[...] = last
    shift_ref[0] = last
    xm = x + (prev - x) * mu_ref[...]
    r = xm[:, 0:W]
    k = xm[:, W:2 * W]
    v = xm[:, 2 * W:3 * W]
    wl = xm[:, 3 * W:3 * W + RW_LORA_W]
    al = xm[:, 3 * W + RW_LORA_W:3 * W + RW_LORA_W + RW_LORA_A]
    gl = xm[:, 3 * W + RW_LORA_W + RW_LORA_A:]
    wpre = w0_ref[...] + _dot(jnp.tanh(wl), w2_ref[...], HIGHEST)
    softplus_neg = jnp.maximum(-wpre, 0.0) + jnp.log1p(jnp.exp(-jnp.abs(wpre)))
    decay = jnp.exp(-jnp.exp(-softplus_neg - 0.5))
    a = _sigmoid(a0_ref[...] + _dot(al, a2_ref[...], HIGHEST))
    g = _dot(_sigmoid(gl), g2_ref[...], HIGHEST)
    ones = _head_ones(W)
    kk = k * kk_ref[...]
    kk = kk / jnp.maximum(jnp.sqrt(_dot(kk * kk, ones, HIGHEST)), 1e-12)
    k2 = k * (1.0 + (a - 1.0) * ka_ref[...])
    bonus = _dot(r * k2 * rk_ref[...], ones, HIGHEST) * v
    p_ref[:, 0:W] = r
    p_ref[:, W:2 * W] = decay
    p_ref[:, 2 * W:3 * W] = k2
    p_ref[:, 3 * W:4 * W] = v
    p_ref[:, 4 * W:5 * W] = kk
    p_ref[:, 5 * W:6 * W] = a * kk
    gb_ref[:, 0:W] = g
    gb_ref[:, W:2 * W] = bonus


def _rwkv_chunk_kernel(p_ref, s0_ref, o_ref, s_ref, *, Bb, L):
    c = pl.program_id(1)
    W, D = RW_WIDTH, HEAD_DIM

    @pl.when(c == 0)
    def _():
        s_ref[...] = s0_ref[...]

    row, col = _iota((L, L), 0), _iota((L, L), 1)
    incl = col <= row
    strict = col < row
    tril_b = jnp.where(incl, 1.0, 0.0).astype(BF16)
    mm = lambda a, b: _dot(a.astype(BF16), b.astype(BF16))
    mm_nt = lambda a, b: _dot_nt(a.astype(BF16), b.astype(BF16))
    mm_tn = lambda a, b: _dot_tn(a.astype(BF16), b.astype(BF16))

    units = []
    for b in range(Bb):
        x = p_ref[b]
        r, w, k, v, kk, bb = (x[:, j * W:(j + 1) * W] for j in range(6))
        logw = jnp.log(w)
        logw_hi = logw.astype(BF16)
        logw_lo = (logw - logw_hi.astype(F32)).astype(BF16)
        cum = _dot(tril_b, logw_hi) + _dot(tril_b, logw_lo)
        g = jnp.exp(cum)
        ginv = jnp.exp(-cum)
        khat = kk * jnp.exp(cum - logw)
        rhat = r * g
        kd = k * ginv
        bd = bb * ginv
        glast = g[L - 1:L, :]
        for h in range(RW_HEADS):
            sl = slice(h * D, (h + 1) * D)
            units.append(dict(b=b, h=h, sl=sl, khat=khat[:, sl].astype(BF16), rhat=rhat[:, sl].astype(BF16),
                              kd=kd[:, sl], bd=bd[:, sl], v=v[:, sl].astype(BF16), gl=glast[:, sl]))

    for u in units:
        S0 = s_ref[u['b'], u['h']]
        kdb, bdb = u['kd'].astype(BF16), u['bd'].astype(BF16)
        if 2 * L == LANE:
            lhs = jnp.concatenate([u['khat'], u['rhat']], axis=0)
            m1 = _dot_nt(lhs, jnp.concatenate([bdb, kdb], axis=0))
            m2 = _dot_nt(lhs, S0.astype(BF16))
            q_raw, p_raw, aub_raw, avk_raw = m1[0:L, 0:L], m1[0:L, L:], m1[L:, 0:L], m1[L:, L:]
            ks0, u['rs0'] = m2[0:L], m2[L:]
        else:
            q_raw, p_raw = _dot_nt(u['khat'], bdb), _dot_nt(u['khat'], kdb)
            aub_raw, avk_raw = _dot_nt(u['rhat'], bdb), _dot_nt(u['rhat'], kdb)
            ks0, u['rs0'] = _dot_nt(u['khat'], S0.astype(BF16)), _dot_nt(u['rhat'], S0.astype(BF16))
        u['Q'] = jnp.where(strict, q_raw, 0.0).astype(BF16)
        u['P'] = jnp.where(strict, p_raw, 0.0).astype(BF16)
        u['avk'] = jnp.where(incl, avk_raw, 0.0).astype(BF16)
        u['aub'] = jnp.where(incl, aub_raw, 0.0).astype(BF16)
        u['X'] = ks0
    for u in units:
        u['X'] = u['X'] + _dot(u['P'], u['v'])
    for u in units:
        u['X'] = u['X'] - mm(u['Q'], u['X'])
    n = 2
    while n < L:
        for u in units:
            u['Q'] = _dot(u['Q'], u['Q']).astype(BF16)
        for u in units:
            u['X'] = u['X'] + mm(u['Q'], u['X'])
        n *= 2
    for u in units:
        xb = u['X'].astype(BF16)
        o_ref[u['b'], :, u['sl']] = u['rs0'] + _dot(u['avk'], u['v']) - _dot(u['aub'], xb)
        gl = u['gl']
        S0 = s_ref[u['b'], u['h']]
        s_ref[u['b'], u['h']] = S0 * gl + _dot_tn(u['v'], (u['kd'] * gl).astype(BF16)) - _dot_tn(xb, (u['bd'] * gl).astype(BF16))


def _rwkv(rw, shift0p, S0, mu_p, w0, w2, a0, a2, g2, k_k, k_a, r_k, B, T):
    N = B * T
    W = RW_WIDTH
    tt = min(T, 256)
    ntt = T // tt
    const = lambda shape: pl.BlockSpec(shape, lambda b, c: (0,) * len(shape))
    rows = lambda w: pl.BlockSpec((tt, w), lambda b, c: (b * ntt + c, 0))
    vec = lambda a: a.reshape(1, -1)
    p, gb, shift = pl.pallas_call(
        functools.partial(_rwkv_prep_kernel, tt=tt),
        grid=(B, ntt),
        in_specs=[rows(1024), pl.BlockSpec((1, 1, 1024), lambda b, c: (b, 0, 0)), const((1, 1024)),
                  const((1, W)), const((RW_LORA_W, W)), const((1, W)), const((RW_LORA_A, W)), const((RW_LORA_G, W)),
                  const((1, W)), const((1, W)), const((1, W))],
        out_specs=[rows(6 * W), rows(2 * W), pl.BlockSpec((1, 1, 1024), lambda b, c: (b, 0, 0))],
        out_shape=[jax.ShapeDtypeStruct((N, 6 * W), F32), jax.ShapeDtypeStruct((N, 2 * W), F32),
                   jax.ShapeDtypeStruct((B, 1, 1024), F32)],
        scratch_shapes=[pltpu.VMEM((1, 1024), F32)],
        compiler_params=_cp(("parallel", "arbitrary")),
        name="rwkv_prep",
    )(rw, shift0p.reshape(B, 1, 1024), vec(mu_p), vec(w0), w2, vec(a0), a2, g2, vec(k_k), vec(k_a), vec(r_k))

    Bb = 8 if B % 8 == 0 else B
    L = math.gcd(T, RW_CHUNK)
    blk = lambda w: pl.BlockSpec((Bb, L, w), lambda b, c: (b, c, 0))
    st = pl.BlockSpec((Bb, RW_HEADS, HEAD_DIM, HEAD_DIM), lambda b, c: (b, 0, 0, 0))
    y, S = pl.pallas_call(
        functools.partial(_rwkv_chunk_kernel, Bb=Bb, L=L),
        grid=(B // Bb, T // L),
        in_specs=[blk(6 * W), st],
        out_specs=[blk(W), st],
        out_shape=[jax.ShapeDtypeStruct((B, T, W), F32), jax.ShapeDtypeStruct(S0.shape, F32)],
        compiler_params=_cp(("parallel", "arbitrary")),
        name="rwkv_scan",
    )(p.reshape(B, T, 6 * W), S0)
    return y.reshape(N, W), gb, S, shift[:, 0]


def _compress_kernel(x_ref, posl_ref, posh_ref, wl_ref, wh_ref, w2_ref, o_ref, *, G):
    half = 2 * KV_WIDTH
    pieces = [x_ref[0, :, r * KV_ROW:r * KV_ROW + half] for r in range(GRP_ROWS)]
    xg = jnp.concatenate(pieces, axis=-1)
    lo = _dot((xg + posl_ref[...]).astype(BF16), wl_ref[...])
    hi = _dot((xg + posh_ref[...]).astype(BF16), wh_ref[...])
    hid = _silu(lo + pltpu.roll(hi, G - 1, 0))
    o_ref[0] = _dot(hid.astype(BF16), w2_ref[...])


def _pack_cmp_weights(cmp_pos, cmp_w1, cmp_w2):
    eye2 = jnp.eye(2, dtype=F32)
    w1 = cmp_w1.reshape(2, CMP_LEN, HEAD_DIM, CMP_HID)
    wbig = jnp.einsum('jrde,aj,bg->rabdjge', w1, eye2, eye2)
    wbig = wbig.reshape(CMP_LEN, 2 * KV_WIDTH, 4 * CMP_HID).astype(BF16)
    w_lo = wbig[:GRP_ROWS].reshape(GRP_ROWS * 2 * KV_WIDTH, 4 * CMP_HID)
    w_hi = wbig[GRP_ROWS:].reshape(GRP_ROWS * 2 * KV_WIDTH, 4 * CMP_HID)
    w2big = jnp.einsum('jed,aj,bg->abejgd', cmp_w2, eye2, eye2).reshape(4 * CMP_HID, 2 * KV_WIDTH).astype(BF16)
    pos = jnp.broadcast_to(cmp_pos.transpose(1, 0, 2)[:, :, None, :], (CMP_LEN, 2, NSA_KV_HEADS, HEAD_DIM))
    pos_lo = pos[:GRP_ROWS].reshape(1, -1)
    pos_hi = pos[GRP_ROWS:].reshape(1, -1)
    return w_lo, w_hi, w2big, pos_lo, pos_hi


def _overlap_map(n_rows, n_cmp, n_sel):
    cs = np.arange(n_rows) * CMP_STRIDE
    ss = np.arange(LANE) * SEL_BLOCK
    ov = np.clip(np.minimum(cs[:, None] + CMP_LEN, ss[None, :] + SEL_BLOCK) - np.maximum(cs[:, None], ss[None, :]),
                 0, None) / CMP_LEN
    ov[n_cmp:, :] = 0.0
    ov[:, n_sel:] = 0.0
    return ov


def _compress(kv_rows, cmp_pos, cmp_w1, cmp_w2, B, G):
    rows = kv_rows.shape[1]
    xg = kv_rows.reshape(B, rows // GRP_ROWS, GRP_ROWS * KV_ROW)
    w_lo, w_hi, w2big, pos_lo, pos_hi = _pack_cmp_weights(cmp_pos, cmp_w1, cmp_w2)
    K = GRP_ROWS * 2 * KV_WIDTH
    const = lambda shape: pl.BlockSpec(shape, lambda b: (0,) * len(shape))
    return pl.pallas_call(
        functools.partial(_compress_kernel, G=G),
        grid=(B,),
        in_specs=[pl.BlockSpec((1, G, GRP_ROWS * KV_ROW), lambda b: (b, 0, 0)), const((1, K)), const((1, K)),
                  const((K, 4 * CMP_HID)), const((K, 4 * CMP_HID)), const((4 * CMP_HID, 2 * KV_WIDTH))],
        out_specs=pl.BlockSpec((1, G, 2 * KV_WIDTH), lambda b: (b, 0, 0)),
        out_shape=jax.ShapeDtypeStruct((B, G, 2 * KV_WIDTH), F32),
        compiler_params=_cp(("parallel",)),
        name="nsa_compress",
    )(xg, pos_lo, pos_hi, w_lo, w_hi, w2big)


def _softmax_pv(s, bias, v_aug, *, QB, v_keys_minor):
    G, D = NSA_GROUP, HEAD_DIM
    K = s.shape[-1]
    s3 = s.reshape(G, QB, K) + bias[None]
    p = jnp.exp(s3 - jnp.max(s3, axis=-1, keepdims=True)).astype(BF16).reshape(G * QB, K)
    oa = _dot_nt(p, v_aug) if v_keys_minor else _dot(p, v_aug)
    return oa[:, 0:D] / jnp.maximum(oa[:, D:D + 1], 1e-30)


def _cmp_select(q_raw, ck, cv, ov, tq, tq_r, QB, NS, n_rank):
    G = NSA_GROUP
    cmp_end = _iota((1, ck.shape[0]), 1) * CMP_STRIDE + (CMP_LEN - 1)
    p_c = _masked_softmax(_dot_nt(q_raw, ck.astype(BF16)), cmp_end <= tq_r)
    o_c = _dot(p_c.astype(BF16), cv.astype(BF16))
    psum = p_c[0:QB]
    for h in range(1, G):
        psum = psum + p_c[h * QB:(h + 1) * QB]
    imp = _dot(psum, ov, HIGHEST)
    blk = _iota((1, LANE), 1)
    cur = tq // SEL_BLOCK
    avail = blk * SEL_BLOCK <= tq
    forced = (blk == 0) | (blk == cur) | (blk == cur - 1)
    score = jnp.where(avail, imp + jnp.where(forced, FORCE_BONUS, 0.0), SEL_NEG)
    cnt = jnp.zeros((QB, LANE), F32)
    for i in range(n_rank):
        col = score[:, i:i + 1]
        cnt = cnt + jnp.where((col > score) | ((col >= score) & (blk > i)), 1.0, 0.0)
    sel = ((cnt < min(SEL_TOPN, NS)) & (score > 0.5 * SEL_NEG)).astype(BF16)
    return o_c, sel


def _nsa_kernel(q_ref, sm_ref, ckv_ref, kv_ref, win_ref, ov_ref, ex_ref, o_ref, skb, svb, wkb, wvb, *,
                QB, NB, NS, P, WN, offset, win_pos0, win_band, nvar):
    nb = pl.program_id(1)
    D = HEAD_DIM
    G = NSA_GROUP
    R = G * QB

    @pl.when(nb == 0)
    def _():
        for g in range(NSA_KV_HEADS):
            skb[g] = kv_ref[0, :, g * D:(g + 1) * D].astype(BF16)
            svb[g, :, 0:D] = kv_ref[0, :, KV_WIDTH + g * D:KV_WIDTH + (g + 1) * D].astype(BF16)
            svb[g, :, D:2 * D] = jnp.ones((P, D), BF16)
            wkb[g] = win_ref[0, :, g * D:(g + 1) * D].astype(BF16)
            wvb[g, :, 0:D] = win_ref[0, :, KV_WIDTH + g * D:KV_WIDTH + (g + 1) * D].astype(BF16)
            wvb[g, :, D:2 * D] = jnp.ones((wvb.shape[1], D), BF16)

    softmax_pv = functools.partial(_softmax_pv, QB=QB, v_keys_minor=False)

    def body(PS):
        qs = nb * QB
        tq = offset + qs + _iota((QB, 1), 0)
        tq_r = offset + qs + (_iota((R, 1), 0) % QB)
        gates = _sigmoid(sm_ref[...])
        kpos = _iota((1, PS), 1)
        if win_band:
            wstart = pl.multiple_of(jnp.maximum(qs - WINDOW, 0), 16)
        else:
            wstart = 0
        dlt = tq - (win_pos0 + wstart + _iota((1, WN), 1))
        bias_w = jnp.where((dlt >= 0) & (dlt < WINDOW), 0.0, NEG)
        qv = q_ref[...] * (D ** -0.5)

        outs = []
        for g in range(NSA_KV_HEADS):
            def stack(base):
                return jnp.concatenate([qv[:, base + (g * G + h) * D:base + (g * G + h + 1) * D] for h in range(G)], 0)

            def gate_stack(j):
                c0 = 2 * ML_HEADS + 3 * g * G + j
                return jnp.concatenate([gates[:, c0 + 3 * h:c0 + 3 * h + 1] for h in range(G)], 0)

            q_raw = stack(0).astype(BF16)
            q_rot = stack(NSA_WIDTH).astype(BF16)
            ck = ckv_ref[0, :, g * D:(g + 1) * D]
            cv = ckv_ref[0, :, KV_WIDTH + g * D:KV_WIDTH + (g + 1) * D]
            o_c, sel = _cmp_select(q_raw, ck, cv, ov_ref[...], tq, tq_r, QB, NS, min(NS, PS // SEL_BLOCK))
            selk = _dot(sel, ex_ref[:, 0:PS])
            bias_s = jnp.where((selk > 0.5) & (kpos <= tq), 0.0, NEG)
            o_s = softmax_pv(_dot_nt(q_rot, skb[g, 0:PS, :]), bias_s, svb[g, 0:PS, :])
            o_w = softmax_pv(_dot_nt(q_rot, wkb[g, pl.ds(wstart, WN), :]), bias_w, wvb[g, pl.ds(wstart, WN), :])
            o = gate_stack(0) * o_c + gate_stack(1) * o_s + gate_stack(2) * o_w
            outs.extend(o[h * QB:(h + 1) * QB] for h in range(G))
        o_ref[...] = jnp.concatenate(outs, axis=-1)

    for v in range(nvar):
        pl.when(nb // (NB // nvar) == v)(functools.partial(body, P * (v + 1) // nvar))


def _nsa_attend(qq, sm, ckv, kv_rows, win_rows, B, T, offset, p_len, win_pos0, win_band):
    N = B * T
    QB = math.gcd(T, Q_BLOCK)
    NB = T // QB
    P = kv_rows.shape[1]
    NS = -(-p_len // SEL_BLOCK)
    NC = (p_len - CMP_LEN) // CMP_STRIDE + 1
    NCP = ckv.shape[1]
    assert NS <= LANE and NC <= NCP and P % LANE == 0
    LW = win_rows.shape[1]
    WN = WINDOW + QB if win_band else LW
    ov = _overlap_map(NCP, NC, NS)
    expand = (np.arange(P)[None, :] // SEL_BLOCK == np.arange(LANE)[:, None]).astype(np.float32)
    const = lambda shape: pl.BlockSpec(shape, lambda b, nb: (0,) * len(shape))
    rows = lambda w: pl.BlockSpec((QB, w), lambda b, nb: (b * NB + nb, 0))
    causal_only = offset == 0 and P == T
    nvar = 4 if (causal_only and NB % 4 == 0 and (P // 4) % LANE == 0) else 1
    return pl.pallas_call(
        functools.partial(_nsa_kernel, QB=QB, NB=NB, NS=NS, P=P, WN=WN, offset=offset, win_pos0=win_pos0,
                          win_band=win_band, nvar=nvar),
        grid=(B, NB),
        in_specs=[rows(2 * NSA_WIDTH), rows(LANE),
                  pl.BlockSpec((1, NCP, 2 * KV_WIDTH), lambda b, nb: (b, 0, 0)),
                  pl.BlockSpec((1, P, 2 * KV_WIDTH), lambda b, nb: (b, 0, 1)),
                  pl.BlockSpec((1, LW, 2 * KV_WIDTH), lambda b, nb: (b, 0, 0)),
                  const((NCP, LANE)), const((LANE, P))],
        out_specs=rows(NSA_WIDTH),
        out_shape=jax.ShapeDtypeStruct((N, NSA_WIDTH), F32),
        scratch_shapes=[pltpu.VMEM((NSA_KV_HEADS, P, HEAD_DIM), BF16), pltpu.VMEM((NSA_KV_HEADS, P, 2 * HEAD_DIM), BF16),
                        pltpu.VMEM((NSA_KV_HEADS, LW, HEAD_DIM), BF16),
                        pltpu.VMEM((NSA_KV_HEADS, LW, 2 * HEAD_DIM), BF16)],
        compiler_params=_cp(("parallel", "arbitrary")),
        name="nsa_attend",
    )(qq, sm, ckv, kv_rows, win_rows, jnp.asarray(ov, F32), jnp.asarray(expand, BF16))


def _nsa_paged_kernel(pt_ref, *refs, n_pages, page, T, NS, offset, lwin):
    del pt_ref
    page_refs = refs[:n_pages]
    (q_ref, sm_ref, new_ref, wcache_ref, wnew_ref, posl_ref, posh_ref, wl_ref, wh_ref, w2_ref, ov_ref, ex_ref,
     o_ref, x_ref) = refs[n_pages:]
    D, G = HEAD_DIM, NSA_GROUP
    QB = T
    R = G * QB
    P = (n_pages + 1) * page
    NG = n_pages * page // GRP_ROWS

    for i in range(n_pages):
        for j in range(2):
            for g in range(NSA_KV_HEADS):
                x_ref[j, i * page:(i + 1) * page, g * D:(g + 1) * D] = page_refs[i][0, j, g].T
    xg = jnp.concatenate([x_ref.at[j][pl.ds(r, NG, stride=GRP_ROWS), :] for r in range(GRP_ROWS) for j in range(2)],
                         axis=-1)
    lo = _dot((xg + posl_ref[...]).astype(BF16), wl_ref[...])
    hi = _dot((xg + posh_ref[...]).astype(BF16), wh_ref[...])
    ckv = _dot(_silu(lo + pltpu.roll(hi, NG - 1, 0)).astype(BF16), w2_ref[...])

    tq = offset + _iota((QB, 1), 0)
    tq_r = offset + (_iota((R, 1), 0) % QB)
    gates = _sigmoid(sm_ref[...])
    qv = q_ref[...] * (D ** -0.5)
    kpos = _iota((1, P), 1)
    dlt = tq - (offset - lwin + _iota((1, lwin + page), 1))
    bias_w = jnp.where((dlt >= 0) & (dlt < WINDOW), 0.0, NEG)
    zrows = lambda w: jnp.zeros((page - T, w), F32)
    new_t = jnp.concatenate([new_ref[...], zrows(KV_ROW)], axis=0)
    wnew_t = jnp.concatenate([wnew_ref[...], zrows(2 * KV_WIDTH)], axis=0)
    softmax_pv = functools.partial(_softmax_pv, QB=QB, v_keys_minor=True)

    outs = []
    for g in range(NSA_KV_HEADS):
        def stack(base):
            return jnp.concatenate([qv[:, base + (g * G + h) * D:base + (g * G + h + 1) * D] for h in range(G)], 0)

        def gate_stack(j):
            c0 = 2 * ML_HEADS + 3 * g * G + j
            return jnp.concatenate([gates[:, c0 + 3 * h:c0 + 3 * h + 1] for h in range(G)], 0)

        def new_tile(x, c0):
            return x[:, c0:c0 + D].T.astype(BF16)

        q_raw = stack(0).astype(BF16)
        q_rot = stack(NSA_WIDTH).astype(BF16)
        o_c, sel = _cmp_select(q_raw, ckv[:, g * D:(g + 1) * D], ckv[:, KV_WIDTH + g * D:KV_WIDTH + (g + 1) * D],
                               ov_ref[...], tq, tq_r, QB, NS, NS)
        bias_s = jnp.where((_dot(sel, ex_ref[...]) > 0.5) & (kpos <= tq), 0.0, NEG)
        kt = jnp.concatenate([page_refs[i][0, 2, g].astype(BF16) for i in range(n_pages)]
                             + [new_tile(new_t, 2 * KV_WIDTH + g * D)], axis=1)
        vt = jnp.concatenate([page_refs[i][0, 3, g].astype(BF16) for i in range(n_pages)]
                             + [new_tile(new_t, 3 * KV_WIDTH + g * D)], axis=1)
        o_s = softmax_pv(_dot(q_rot, kt), bias_s, jnp.concatenate([vt, jnp.ones_like(vt)], axis=0))
        wkt = jnp.concatenate([wcache_ref[0, 0, g].astype(BF16), new_tile(wnew_t, g * D)], axis=1)
        wvt = jnp.concatenate([wcache_ref[0, 1, g].astype(BF16), new_tile(wnew_t, KV_WIDTH + g * D)], axis=1)
        o_w = softmax_pv(_dot(q_rot, wkt), bias_w, jnp.concatenate([wvt, jnp.ones_like(wvt)], axis=0))
        o = gate_stack(0) * o_c + gate_stack(1) * o_s + gate_stack(2) * o_w
        outs.extend(o[h * QB:(h + 1) * QB] for h in range(G))
    o_ref[...] = jnp.concatenate(outs, axis=-1)


def _nsa_paged(qq, sm, kv_new, win_new, cache_t, page_ids, wcache_t, win_base, cmp_pos, cmp_w1, cmp_w2, B, T, offset):
    N = B * T
    n_pages = page_ids.shape[1]
    page = cache_t.shape[-1]
    lwin = wcache_t.shape[-1]
    p_len = n_pages * page + T
    P = (n_pages + 1) * page
    NS = -(-p_len // SEL_BLOCK)
    n_cmp = (p_len - CMP_LEN) // CMP_STRIDE + 1
    NG = n_pages * page // GRP_ROWS
    assert T % 8 == 0 and T <= page and NS <= LANE and n_cmp + 1 == NG and lwin % LANE == 0 and NG % 8 == 0
    w_lo, w_hi, w2big, pos_lo, pos_hi = _pack_cmp_weights(cmp_pos, cmp_w1, cmp_w2)
    ov = _overlap_map(NG, n_cmp, NS)
    expand = (np.arange(P)[None, :] // SEL_BLOCK == np.arange(LANE)[:, None]).astype(np.float32)
    K = GRP_ROWS * 2 * KV_WIDTH

    def page_map(i, b, pt):
        return (pt[b * n_pages + i], 0, 0, 0, 0)

    const = lambda shape: pl.BlockSpec(shape, lambda b, pt: (0,) * len(shape))
    rows = lambda w: pl.BlockSpec((T, w), lambda b, pt: (b, 0))
    in_specs = [pl.BlockSpec((1, 4, NSA_KV_HEADS, HEAD_DIM, page), functools.partial(page_map, i)) for i in range(n_pages)]
    in_specs += [rows(2 * NSA_WIDTH), rows(LANE), rows(KV_ROW),
                 pl.BlockSpec((1, 2, NSA_KV_HEADS, HEAD_DIM, lwin), lambda b, pt: (win_base + b, 0, 0, 0, 0)),
                 rows(2 * KV_WIDTH), const((1, K)), const((1, K)), const((K, 4 * CMP_HID)), const((K, 4 * CMP_HID)),
                 const((4 * CMP_HID, 2 * KV_WIDTH)), const((NG, LANE)), const((LANE, P))]
    return pl.pallas_call(
        functools.partial(_nsa_paged_kernel, n_pages=n_pages, page=page, T=T, NS=NS, offset=offset, lwin=lwin),
        grid_spec=pltpu.PrefetchScalarGridSpec(
            num_scalar_prefetch=1, grid=(B,), in_specs=in_specs, out_specs=rows(NSA_WIDTH),
            scratch_shapes=[pltpu.VMEM((2, n_pages * page, KV_WIDTH), F32)]),
        out_shape=jax.ShapeDtypeStruct((N, NSA_WIDTH), F32),
        compiler_params=_cp(("parallel",)),
        name="nsa_paged",
    )(page_ids.reshape(-1), *([cache_t] * n_pages), qq, sm, kv_new, wcache_t, win_new, pos_lo, pos_hi, w_lo, w_hi,
      w2big, jnp.asarray(ov, F32), jnp.asarray(expand, BF16))


def _outproj_kernel(x_ref, hml_ref, yrw_ref, gb_ref, hns_ref, lnw_ref, lnb_ref, w_ref, o_ref):
    W = RW_WIDTH
    y = yrw_ref[...]
    ones = _head_ones(W) * (1.0 / HEAD_DIM)
    mu = _dot(y, ones, HIGHEST)
    var = _dot(jnp.square(y - mu), ones, HIGHEST)
    yn = (y - mu) * lax.rsqrt(var + RW_GN_EPS) * lnw_ref[...]
    h_rw = (yn + lnb_ref[...] + gb_ref[:, W:2 * W]) * gb_ref[:, 0:W]
    acc = _dot(hml_ref[...].astype(BF16), w_ref[0:ML_WIDTH, :])
    acc = acc + _dot(h_rw.astype(BF16), w_ref[ML_WIDTH:ML_WIDTH + W, :])
    acc = acc + _dot(hns_ref[...].astype(BF16), w_ref[ML_WIDTH + W:, :])
    o_ref[...] = x_ref[...] + acc


def _outproj(x2, h_ml, y_rw, gb, h_ns, ln_w, ln_b, w_out):
    N = x2.shape[0]
    tm = min(512, N)
    row = lambda w: pl.BlockSpec((tm, w), lambda i: (i, 0))
    const = lambda shape: pl.BlockSpec(shape, lambda i: (0,) * len(shape))
    return pl.pallas_call(
        _outproj_kernel,
        grid=(N // tm,),
        in_specs=[row(D_MODEL), row(ML_WIDTH), row(RW_WIDTH), row(2 * RW_WIDTH), row(NSA_WIDTH),
                  const((1, RW_WIDTH)), const((1, RW_WIDTH)), const((MIX_WIDTH, D_MODEL))],
        out_specs=row(D_MODEL),
        out_shape=jax.ShapeDtypeStruct((N, D_MODEL), F32),
        compiler_params=_cp(("parallel",)),
        name="outproj",
    )(x2, h_ml, y_rw, gb, h_ns, ln_w.reshape(1, -1), ln_b.reshape(1, -1), w_out.astype(BF16))


def _rms(x, w):
    return x * lax.rsqrt(jnp.mean(x * x, axis=-1, keepdims=True) + RMS_EPS) * w


def _ffn_kernel(x_ref, nw_ref, w1_ref, w3_ref, w2_ref, fw_ref, o_ref, xn_ref, *, final_norm):
    j = pl.program_id(1)

    @pl.when(j == 0)
    def _():
        x = x_ref[...]
        xn_ref[...] = _rms(x, nw_ref[...]).astype(BF16)
        o_ref[...] = x

    xn = xn_ref[...]
    hid = _silu(_dot(xn, w1_ref[...])) * _dot(xn, w3_ref[...])
    o_ref[...] += _dot(hid.astype(BF16), w2_ref[...])

    if final_norm:
        @pl.when(j == pl.num_programs(1) - 1)
        def _():
            o_ref[...] = _rms(o_ref[...], fw_ref[...])


def _ffn(x2, norm_w, w1, w3, w2, final_w):
    N = x2.shape[0]
    d_ff = w1.shape[1]
    tm = min(512, N)
    nj = 2 if (d_ff // 2) % LANE == 0 else 1
    tf = d_ff // nj
    final_norm = final_w is not None
    fw = (final_w if final_norm else jnp.ones((D_MODEL,), F32)).reshape(1, D_MODEL)
    return pl.pallas_call(
        functools.partial(_ffn_kernel, final_norm=final_norm),
        grid=(N // tm, nj),
        in_specs=[pl.BlockSpec((tm, D_MODEL), lambda i, j: (i, 0)), pl.BlockSpec((1, D_MODEL), lambda i, j: (0, 0)),
                  pl.BlockSpec((D_MODEL, tf), lambda i, j: (0, j)), pl.BlockSpec((D_MODEL, tf), lambda i, j: (0, j)),
                  pl.BlockSpec((tf, D_MODEL), lambda i, j: (j, 0)), pl.BlockSpec((1, D_MODEL), lambda i, j: (0, 0))],
        out_specs=pl.BlockSpec((tm, D_MODEL), lambda i, j: (i, 0)),
        out_shape=jax.ShapeDtypeStruct((N, D_MODEL), F32),
        scratch_shapes=[pltpu.VMEM((tm, D_MODEL), BF16)],
        compiler_params=_cp(("parallel", "arbitrary")),
        name="ffn",
    )(x2, norm_w.reshape(1, D_MODEL), w1.astype(BF16), w3.astype(BF16), w2.astype(BF16), fw)


def _moe_kernel(x_ref, nw_ref, rt_ref, w1_ref, w3_ref, w2_ref, fw_ref, o_ref, xn_ref, comb_ref, *, final_norm):
    e = pl.program_id(1)

    @pl.when(e == 0)
    def _():
        x = x_ref[...]
        xn = _rms(x, nw_ref[...])
        xn_ref[...] = xn.astype(BF16)
        o_ref[...] = x
        lane = _iota((1, LANE), 1).astype(F32)
        logits = jnp.where(lane < N_EXPERTS, _dot(xn, rt_ref[...], HIGHEST), -jnp.inf)
        m1 = jnp.max(logits, axis=-1, keepdims=True)
        i1 = jnp.min(jnp.where(logits == m1, lane, float(LANE)), axis=-1, keepdims=True)
        rest = jnp.where(lane == i1, -jnp.inf, logits)
        m2 = jnp.max(rest, axis=-1, keepdims=True)
        i2 = jnp.min(jnp.where(rest == m2, lane, float(LANE)), axis=-1, keepdims=True)
        e2 = jnp.exp(m2 - m1)
        den = 1.0 + e2
        comb_ref[...] = jnp.where(lane == i1, 1.0 / den, 0.0) + jnp.where(lane == i2, e2 / den, 0.0)

    xn = xn_ref[...]
    comb = comb_ref[...]
    ce = jnp.sum(jnp.where(_iota((1, LANE), 1) == e, comb, 0.0), axis=-1, keepdims=True)
    hid = _silu(_dot(xn, w1_ref[0])) * _dot(xn, w3_ref[0])
    o_ref[...] += ce * _dot(hid.astype(BF16), w2_ref[0])

    if final_norm:
        @pl.when(e == pl.num_programs(1) - 1)
        def _():
            o_ref[...] = _rms(o_ref[...], fw_ref[...])


def _moe(x2, norm_w, router, w1, w3, w2, final_w):
    N = x2.shape[0]
    E, _, d_exp = w1.shape
    assert E == N_EXPERTS and TOP_K == 2
    tm = min(512, N)
    final_norm = final_w is not None
    fw = (final_w if final_norm else jnp.ones((D_MODEL,), F32)).reshape(1, D_MODEL)
    rt = jnp.pad(router, ((0, 0), (0, LANE - E)))
    return pl.pallas_call(
        functools.partial(_moe_kernel, final_norm=final_norm),
        grid=(N // tm, E),
        in_specs=[pl.BlockSpec((tm, D_MODEL), lambda i, e: (i, 0)), pl.BlockSpec((1, D_MODEL), lambda i, e: (0, 0)),
                  pl.BlockSpec((D_MODEL, LANE), lambda i, e: (0, 0)),
                  pl.BlockSpec((1, D_MODEL, d_exp), lambda i, e: (e, 0, 0)),
                  pl.BlockSpec((1, D_MODEL, d_exp), lambda i, e: (e, 0, 0)),
                  pl.BlockSpec((1, d_exp, D_MODEL), lambda i, e: (e, 0, 0)),
                  pl.BlockSpec((1, D_MODEL), lambda i, e: (0, 0))],
        out_specs=pl.BlockSpec((tm, D_MODEL), lambda i, e: (i, 0)),
        out_shape=jax.ShapeDtypeStruct((N, D_MODEL), F32),
        scratch_shapes=[pltpu.VMEM((tm, D_MODEL), BF16), pltpu.VMEM((tm, LANE), F32)],
        compiler_params=_cp(("parallel", "arbitrary")),
        name="moe",
    )(x2, norm_w.reshape(1, D_MODEL), rt, w1.astype(BF16), w3.astype(BF16), w2.astype(BF16), fw)


def _layer(l, x2, B, T, offset, st, P, final_w):
    C0, n0, m0, conv0, S0, shift0, cache, page_ids, win0 = st
    rw_perm = jnp.asarray(_rw_order())
    pos = offset + jnp.arange(T)
    cos2, sin2 = _rope_tables(pos)
    ml, rw, qq, kv4, wrow, sm = _inproj(x2, P['norm_mix_w'][l], _pack_w_in(P['w_in'][l]), cos2, sin2, T)

    h_ml, C, n, m, conv = _mlstm(ml, sm, conv0, C0, n0, m0, P['ml_conv_w'][l], P['ml_conv_b'][l],
                                 P['ml_b_i'][l], P['ml_b_f'][l], P['ml_norm_w'][l], B, T)

    y_rw, gb, S, shift_p = _rwkv(rw, shift0[:, rw_perm], S0, P['rw_mu'][l][rw_perm], P['rw_w0'][l], P['rw_w2'][l],
                                 P['rw_a0'][l], P['rw_a2'][l], P['rw_g2'][l], P['rw_k_k'][l], P['rw_k_a'][l],
                                 P['rw_r_k'][l].reshape(-1), B, T)
    shift_new = jnp.zeros_like(shift_p).at[:, rw_perm].set(shift_p)

    kv3 = kv4.reshape(B, T, KV_ROW)
    w3 = wrow.reshape(B, T, 2 * KV_WIDTH)
    cmp_w = (P['nsa_cmp_pos'][l], P['nsa_cmp_w1'][l], P['nsa_cmp_w2'][l])
    if cache is None:
        win_all = w3
        n_cmp = (T - CMP_LEN) // CMP_STRIDE + 1
        G = -(-(n_cmp + 1) // 8) * 8
        assert G * GRP_ROWS <= T
        ckv = _compress(kv3, *cmp_w, B, G)
        h_ns = _nsa_attend(qq, sm, ckv, kv3, w3, B, T, offset, T, 0, True)
    else:
        win_all = jnp.concatenate([win0['rows'], w3], axis=1)
        h_ns = _nsa_paged(qq, sm, kv4, wrow, cache, page_ids, win0['key_minor'], win0['base'], *cmp_w, B, T, offset)
    wbuf = P['wbuf']
    lw = win_all.shape[1]
    if lw >= wbuf:
        win_new = win_all[:, lw - wbuf:]
    else:
        win_new = jnp.pad(win_all, ((0, 0), (wbuf - lw, 0), (0, 0)))

    x2 = _outproj(x2, h_ml, y_rw, gb, h_ns, P['rw_ln_w'][l], P['rw_ln_b'][l], P['w_out'][l])
    if l % 2 == 0:
        x2 = _ffn(x2, P['norm_ffn_w'][l], P['ffn_w1'][l // 2], P['ffn_w3'][l // 2], P['ffn_w2'][l // 2], final_w)
    else:
        x2 = _moe(x2, P['norm_ffn_w'][l], P['moe_router'][l // 2], P['moe_w1'][l // 2], P['moe_w3'][l // 2],
                  P['moe_w2'][l // 2], final_w)
    new = (kv3.reshape(B, T, 4, NSA_KV_HEADS, HEAD_DIM), win_new.reshape(B, wbuf, 2, NSA_KV_HEADS, HEAD_DIM),
           C, n, m, conv, S, shift_new)
    return x2, new


def _run(x, offset, get_state, P, depth):
    B, T, _ = x.shape
    x2 = x.reshape(B * T, D_MODEL)
    news = []
    for l in range(depth):
        x2, nw = _layer(l, x2, B, T, offset, get_state(l), P, P['norm_final_w'] if l == depth - 1 else None)
        news.append(nw)
    return x2.reshape(B, T, D_MODEL), [jnp.stack([nw[i] for nw in news]) for i in range(8)]


def kernel(x_prompt, x_sample, cache_nsa_kv, cache_win_kv, state_mlstm_C, state_mlstm_n, state_mlstm_m, state_mlstm_conv, state_rwkv_S, state_rwkv_shift, page_table, norm_mix_w, w_in, ml_conv_w, ml_conv_b, ml_b_i, ml_b_f, ml_norm_w, rw_mu, rw_w0, rw_w2, rw_a0, rw_a2, rw_g2, rw_k_k, rw_k_a, rw_r_k, rw_ln_w, rw_ln_b, nsa_cmp_pos, nsa_cmp_w1, nsa_cmp_w2, w_out, norm_ffn_w, ffn_w1, ffn_w3, ffn_w2, moe_router, moe_w1, moe_w3, moe_w2, norm_final_w):
    depth = w_in.shape[0]
    Bp = x_prompt.shape[0]
    Bs = x_sample.shape[0]
    n_pool, page = cache_nsa_kv.shape[1], cache_nsa_kv.shape[2]
    past_len = page_table.shape[1] * page
    wbuf = min(WINDOW, past_len)
    P = dict(norm_mix_w=norm_mix_w, w_in=w_in, ml_conv_w=ml_conv_w, ml_conv_b=ml_conv_b, ml_b_i=ml_b_i, ml_b_f=ml_b_f,
             ml_norm_w=ml_norm_w, rw_mu=rw_mu, rw_w0=rw_w0, rw_w2=rw_w2, rw_a0=rw_a0, rw_a2=rw_a2, rw_g2=rw_g2,
             rw_k_k=rw_k_k, rw_k_a=rw_k_a, rw_r_k=rw_r_k, rw_ln_w=rw_ln_w, rw_ln_b=rw_ln_b, nsa_cmp_pos=nsa_cmp_pos,
             nsa_cmp_w1=nsa_cmp_w1, nsa_cmp_w2=nsa_cmp_w2, w_out=w_out, norm_ffn_w=norm_ffn_w, ffn_w1=ffn_w1,
             ffn_w3=ffn_w3, ffn_w2=ffn_w2, moe_router=moe_router, moe_w1=moe_w1, moe_w3=moe_w3, moe_w2=moe_w2,
             norm_final_w=norm_final_w, wbuf=wbuf)

    def prompt_state(l):
        z = lambda *s: jnp.zeros((Bp,) + s, F32)
        return (z(ML_HEADS, HEAD_DIM, HEAD_DIM), z(ML_HEADS, HEAD_DIM), z(ML_HEADS), z(ML_CONV - 1, 2 * ML_WIDTH),
                z(RW_HEADS, HEAD_DIM, HEAD_DIM), z(RW_COLS), None, None, None)

    cache_t = jnp.transpose(cache_nsa_kv, (0, 1, 3, 4, 5, 2)).reshape(depth * n_pool, 4, NSA_KV_HEADS, HEAD_DIM, page)
    wcache_t = jnp.transpose(cache_win_kv, (0, 1, 3, 4, 5, 2)).reshape(depth * Bs, 2, NSA_KV_HEADS, HEAD_DIM, -1)

    def sample_state(l):
        win = dict(rows=cache_win_kv[l].reshape(Bs, -1, 2 * KV_WIDTH), key_minor=wcache_t, base=l * Bs)
        return (state_mlstm_C[l], state_mlstm_n[l], state_mlstm_m[l], state_mlstm_conv[l], state_rwkv_S[l],
                state_rwkv_shift[l], cache_t, page_table + l * n_pool, win)

    y_prompt, pst = _run(x_prompt, 0, prompt_state, P, depth)
    y_sample, sst = _run(x_sample, past_len, sample_state, P, depth)
    p_kv, p_win, p_C, p_n, p_m, p_conv, p_S, p_shift = pst
    s_kv, s_win, s_C, s_n, s_m, s_conv, s_S, s_shift = sst
    return (y_prompt, y_sample, p_kv, s_kv, p_win, s_win, p_C, s_C, p_n, s_n, p_m, s_m,
            p_conv, s_conv, p_S, s_S, p_shift, s_shift)
```

```python
import functools
import math

import numpy as np
import jax
import jax.numpy as jnp
from jax import lax
from jax.experimental import pallas as pl
from jax.experimental.pallas import tpu as pltpu

F32 = jnp.float32
BF16 = jnp.bfloat16
HIGHEST = lax.Precision.HIGHEST

D_MODEL = 1024
HEAD_DIM = 64
ML_HEADS = 4
RW_HEADS = 4
NSA_HEADS = 8
NSA_KV_HEADS = 2
NSA_GROUP = NSA_HEADS // NSA_KV_HEADS
ML_WIDTH = ML_HEADS * HEAD_DIM
RW_WIDTH = RW_HEADS * HEAD_DIM
NSA_WIDTH = NSA_HEADS * HEAD_DIM
KV_WIDTH = NSA_KV_HEADS * HEAD_DIM
MIX_WIDTH = ML_WIDTH + RW_WIDTH + NSA_WIDTH
ML_CHUNK = 64
ML_CONV = 4
ML_NORM_EPS = 1e-6
RW_LORA_W = 64
RW_LORA_A = 64
RW_LORA_G = 128
RW_GN_EPS = 64e-5
RW_CHUNK = 64
PREP_ROWS = 256
CMP_LEN = 32
CMP_STRIDE = 16
CMP_HID = 128
SEL_BLOCK = 64
SEL_TOPN = 8
WINDOW = 512
Q_BLOCK = 64
ROPE_THETA = 10000.0
N_EXPERTS = 8
TOP_K = 2
RMS_EPS = 1e-6
NEG = -1e30
SEL_NEG = -1e9
FORCE_BONUS = 1e3
ML_COLS = 4 * ML_WIDTH + 2 * ML_HEADS
RW_COLS = 3 * RW_WIDTH + RW_LORA_W + RW_LORA_A + RW_LORA_G
NSA_COLS = NSA_WIDTH + 6 * KV_WIDTH + 3 * NSA_HEADS

LANE = 128
KV_ROW = 4 * KV_WIDTH
GRP_ROWS = CMP_STRIDE
VMEM_LIMIT = 52 * 1024 * 1024

C_ML = 0
C_RW = 1024
C_Q = 2048
C_QS = 2560
C_KV = 3072
C_SKS = 3584
C_WK = 3712
C_WV = 3840
C_WKS = 3968
C_SM = 4096
C_TOT = 4224


def _cp(sem, vmem=VMEM_LIMIT):
    return pltpu.CompilerParams(dimension_semantics=sem, vmem_limit_bytes=vmem)


def _dot(a, b, precision=None):
    return jnp.dot(a, b, preferred_element_type=F32, precision=precision)


def _dot_nt(a, b, precision=None):
    return lax.dot_general(a, b, (((1,), (1,)), ((), ())), preferred_element_type=F32, precision=precision)


def _dot_tn(a, b, precision=None):
    return lax.dot_general(a, b, (((0,), (0,)), ((), ())), preferred_element_type=F32, precision=precision)


def _sigmoid(x):
    return 1.0 / (1.0 + jnp.exp(-x))


def _silu(x):
    return x * _sigmoid(x)


def _iota(shape, dim):
    return lax.broadcasted_iota(jnp.int32, shape, dim)


def _head_ones(width):
    return (_iota((width, width), 0) // HEAD_DIM == _iota((width, width), 1) // HEAD_DIM).astype(F32)


def _masked_softmax(s, mask):
    s = jnp.where(mask, s, NEG)
    m = jnp.max(s, axis=-1, keepdims=True)
    p = jnp.where(mask, jnp.exp(s - m), 0.0)
    return p / jnp.maximum(jnp.sum(p, -1, keepdims=True), 1e-30)


def _swap_half(n_heads):
    idx = np.arange(n_heads * HEAD_DIM)
    return (idx // HEAD_DIM) * HEAD_DIM + (idx % HEAD_DIM + HEAD_DIM // 2) % HEAD_DIM


def _rw_order():
    r0, wl0 = 0, RW_WIDTH
    k0 = wl0 + RW_LORA_W
    v0 = k0 + RW_WIDTH
    al0 = v0 + RW_WIDTH
    gl0 = al0 + RW_LORA_A
    return np.concatenate([np.arange(r0, r0 + RW_WIDTH), np.arange(k0, k0 + RW_WIDTH), np.arange(v0, v0 + RW_WIDTH),
                           np.arange(wl0, wl0 + RW_LORA_W), np.arange(al0, al0 + RW_LORA_A),
                           np.arange(gl0, gl0 + RW_LORA_G)])


def _pack_cols():
    ml0, rw0, ns0 = 0, ML_COLS, ML_COLS + RW_COLS
    q0 = ns0
    kv0 = q0 + NSA_WIDTH
    sk0 = kv0 + 2 * KV_WIDTH
    wk0 = kv0 + 4 * KV_WIDTH
    wv0 = wk0 + KV_WIDTH
    gt0 = wv0 + KV_WIDTH
    small = np.concatenate([np.arange(ml0 + 4 * ML_WIDTH, ml0 + ML_COLS), np.arange(gt0, gt0 + 3 * NSA_HEADS)])
    cols = np.concatenate([
        np.arange(ml0, ml0 + 4 * ML_WIDTH),
        rw0 + _rw_order(),
        np.arange(q0, q0 + NSA_WIDTH),
        q0 + _swap_half(NSA_HEADS),
        np.arange(kv0, kv0 + 4 * KV_WIDTH),
        sk0 + _swap_half(NSA_KV_HEADS),
        np.arange(wk0, wk0 + KV_WIDTH),
        np.arange(wv0, wv0 + KV_WIDTH),
        wk0 + _swap_half(NSA_KV_HEADS),
        small,
    ])
    n_small = small.shape[0]
    assert cols.shape[0] == C_SM + n_small
    return cols, n_small


def _pack_w_in(w):
    cols, n_small = _pack_cols()
    wp = jnp.take(w, jnp.asarray(cols), axis=1)
    wp = jnp.pad(wp, ((0, 0), (0, C_TOT - C_SM - n_small)))
    return wp.astype(BF16)


def _rope_tables(pos):
    half = HEAD_DIM // 2
    inv = ROPE_THETA ** (-jnp.arange(half, dtype=F32) / half)
    ang = pos.astype(F32)[:, None] * inv[None, :]
    cos, sin = jnp.cos(ang), jnp.sin(ang)
    cos2 = jnp.concatenate([cos, cos, cos, cos], -1)
    sin2 = jnp.concatenate([-sin, sin, -sin, sin], -1)
    return cos2, sin2


def _inproj_kernel(x_ref, nw_ref, w_ref, cos_ref, sin_ref, ml_ref, rw_ref, q_ref, kv_ref, win_ref, sm_ref, ck_ref,
                   cv_ref):
    x = x_ref[...]
    xn = x * lax.rsqrt(jnp.mean(x * x, axis=-1, keepdims=True) + RMS_EPS) * nw_ref[...]
    acc = _dot(xn.astype(BF16), w_ref[...])
    cos = cos_ref[...]
    sin = sin_ref[...]
    cos4 = jnp.concatenate([cos] * 4, axis=-1)
    sin4 = jnp.concatenate([sin] * 4, axis=-1)
    ml_ref[...] = acc[:, C_ML:C_ML + 1024]
    rw_ref[...] = acc[:, C_RW:C_RW + 1024]
    q_ref[:, 0:NSA_WIDTH] = acc[:, C_Q:C_Q + NSA_WIDTH]
    q_ref[:, NSA_WIDTH:2 * NSA_WIDTH] = acc[:, C_Q:C_Q + NSA_WIDTH] * cos4 + acc[:, C_QS:C_QS + NSA_WIDTH] * sin4
    kv_ref[:, 0:2 * KV_WIDTH] = acc[:, C_KV:C_KV + 2 * KV_WIDTH]
    kv_ref[:, 2 * KV_WIDTH:3 * KV_WIDTH] = (acc[:, C_KV + 2 * KV_WIDTH:C_KV + 3 * KV_WIDTH] * cos
                                            + acc[:, C_SKS:C_SKS + KV_WIDTH] * sin)
    kv_ref[:, 3 * KV_WIDTH:4 * KV_WIDTH] = acc[:, C_KV + 3 * KV_WIDTH:C_KV + 4 * KV_WIDTH]
    win_ref[:, 0:KV_WIDTH] = acc[:, C_WK:C_WK + KV_WIDTH] * cos + acc[:, C_WKS:C_WKS + KV_WIDTH] * sin
    win_ref[:, KV_WIDTH:2 * KV_WIDTH] = acc[:, C_WV:C_WV + KV_WIDTH]
    sm_ref[...] = acc[:, C_SM:C_SM + LANE]
    ck_ref[...] = acc[:, C_KV:C_KV + KV_WIDTH]
    cv_ref[...] = acc[:, C_KV + KV_WIDTH:C_KV + 2 * KV_WIDTH]


def _inproj(x2, norm_w, w_packed, cos2, sin2, T):
    N = x2.shape[0]
    tm = min(256, N)
    if T >= tm:
        assert T % tm == 0
        nt = T // tm
    else:
        assert tm % T == 0
        cos2 = jnp.tile(cos2, (tm // T, 1))
        sin2 = jnp.tile(sin2, (tm // T, 1))
        nt = 1
    row = lambda w: pl.BlockSpec((tm, w), lambda i: (i, 0))
    tab = pl.BlockSpec((tm, LANE), lambda i: (i % nt, 0))
    widths = (1024, 1024, 2 * NSA_WIDTH, KV_ROW, 2 * KV_WIDTH, LANE, KV_WIDTH, KV_WIDTH)
    return pl.pallas_call(
        _inproj_kernel,
        grid=(N // tm,),
        in_specs=[row(D_MODEL), pl.BlockSpec((1, D_MODEL), lambda i: (0, 0)),
                  pl.BlockSpec((D_MODEL, C_TOT), lambda i: (0, 0)), tab, tab],
        out_specs=[row(w) for w in widths],
        out_shape=[jax.ShapeDtypeStruct((N, w), F32) for w in widths],
        compiler_params=_cp(("parallel",)),
        name="inproj",
    )(x2, norm_w.reshape(1, D_MODEL), w_packed, cos2, sin2)


def _mlstm_kernel(ml_ref, sm_ref, conv0_ref, c0_ref, n0_ref, m0_ref, cw_ref, cb_ref, gb_ref, nw_ref,
                  h_ref, c_ref, n_ref, m_ref, conv_ref, xp_ref, *, Bb, L):
    c = pl.program_id(1)
    H, D = ML_HEADS, HEAD_DIM
    PAD = 8
    KC = ML_CONV - 1

    @pl.when(c == 0)
    def _():
        c_ref[...] = c0_ref[...]
        n_ref[...] = n0_ref[...]
        m_ref[...] = m0_ref[...]
        xp_ref[:, PAD - KC:PAD, :] = conv0_ref[...]

    tril = (_iota((L, L), 1) <= _iota((L, L), 0))
    eye = (_iota((L, L), 1) == _iota((L, L), 0))
    tril_b = jnp.where(tril, 1.0, 0.0).astype(BF16)

    bs = [dict(b=b) for b in range(Bb)]
    for s in bs:
        b = s['b']
        xp_ref[b, PAD:PAD + L, :] = ml_ref[b, :, 0:2 * ML_WIDTH]
    for s in bs:
        b = s['b']
        y = jnp.broadcast_to(cb_ref[...], (L, 2 * ML_WIDTH))
        for j in range(ML_CONV):
            y = y + xp_ref[b, PAD - KC + j:PAD - KC + j + L, :] * cw_ref[j:j + 1, :]
        s['qk'] = _silu(y)
        gates = sm_ref[b] + gb_ref[...]
        lf = jnp.minimum(gates, 0.0) - jnp.log1p(jnp.exp(-jnp.abs(gates)))
        lf_hi = lf.astype(BF16)
        lf_lo = (lf - lf_hi.astype(F32)).astype(BF16)
        s['gates'] = gates
        s['bcum'] = _dot(tril_b, lf_hi) + _dot(tril_b, lf_lo)
    for s in bs:
        b = s['b']
        tail = xp_ref[b, PAD + L - KC:PAD + L, :]
        xp_ref[b, PAD - KC:PAD, :] = tail
        conv_ref[b] = tail
    for s in bs:
        s['b_all'] = pltpu.roll(s['bcum'], LANE - H, 1)
        s['a_all'] = s['gates'] - s['b_all']
        s['cmax'] = s['a_all']
    step = 1
    while step < L:
        for s in bs:
            s['cmax'] = jnp.where(_iota((L, 1), 0) >= step, jnp.maximum(s['cmax'], pltpu.roll(s['cmax'], step, 0)),
                                  s['cmax'])
        step *= 2
    units = []
    for s in bs:
        b, b_all, a_all = s['b'], s['b_all'], s['a_all']
        m_prev = m_ref[b]
        m_t = b_all + jnp.maximum(m_prev, s['cmax'])
        m_new = m_t[L - 1:L, :]
        b_last = b_all[L - 1:L, :]
        m_ref[b] = m_new
        inter = jnp.exp(b_all + m_prev - m_t)
        wgt = jnp.exp(b_last + a_all - m_new)
        dec = jnp.exp(b_last + m_prev - m_new)
        ccol = b_all - m_t
        einv = jnp.exp(-m_t)
        qk = s['qk']
        for h in range(H):
            lo = h * D
            col = lambda x: x[:, h:h + 1]
            units.append(dict(b=b, h=h, lo=lo, q=qk[:, lo:lo + D], k=qk[:, ML_WIDTH + lo:ML_WIDTH + lo + D] * (D ** -0.5),
                              v=ml_ref[b, :, 2 * ML_WIDTH + lo:2 * ML_WIDTH + lo + D], acol=col(a_all),
                              ccol=col(ccol), inter=col(inter), wgt=col(wgt), dec=col(dec), einv=col(einv)))

    ones_col = jnp.ones((L, D), BF16)
    for u in units:
        u['arow'] = jnp.sum(jnp.where(eye, u['acol'], 0.0), axis=0, keepdims=True)
    for u in units:
        u['dm'] = jnp.exp(jnp.where(tril, u['ccol'] + u['arow'], -jnp.inf))
    for u in units:
        C = c_ref[u['b'], u['h']]
        qb, kb = u['q'].astype(BF16), u['k'].astype(BF16)
        if 2 * L == LANE:
            m1 = _dot_nt(qb, jnp.concatenate([kb, C.astype(BF16)], axis=0))
            u['s_raw'], u['qc'] = m1[:, 0:L], m1[:, L:]
        else:
            u['s_raw'], u['qc'] = _dot_nt(qb, kb), _dot_nt(qb, C.astype(BF16))
        c_ref[u['b'], u['h']] = u['dec'] * C + _dot_tn((u['v'] * u['wgt']).astype(BF16), kb)
        nrow = n_ref[u['b'], u['h']:u['h'] + 1, :]
        u['qn'] = jnp.sum(u['q'] * nrow, axis=-1, keepdims=True)
        n_ref[u['b'], u['h']:u['h'] + 1, :] = u['dec'] * nrow + jnp.sum(u['wgt'] * u['k'], axis=0, keepdims=True)
    for u in units:
        S = (u['s_raw'] * u['dm']).astype(BF16)
        sv = _dot(S, jnp.concatenate([u['v'].astype(BF16), ones_col], axis=1))
        den = u['inter'] * u['qn'] + sv[:, D:D + 1]
        u['hh'] = (u['inter'] * u['qc'] + sv[:, 0:D]) / jnp.maximum(jnp.abs(den), u['einv'])
    ones = _head_ones(ML_WIDTH) * (1.0 / D)
    for s in bs:
        s['hh'] = jnp.concatenate([u['hh'] for u in units if u['b'] == s['b']], axis=-1)
        s['mu'] = _dot(s['hh'], ones, HIGHEST)
    for s in bs:
        s['var'] = _dot(jnp.square(s['hh'] - s['mu']), ones, HIGHEST)
    for s in bs:
        b = s['b']
        hn = (s['hh'] - s['mu']) * lax.rsqrt(s['var'] + ML_NORM_EPS) * nw_ref[...]
        h_ref[b] = hn * _sigmoid(ml_ref[b, :, 3 * ML_WIDTH:4 * ML_WIDTH])


def _mlstm(ml, sm, conv0, C0, n0, m0, conv_w, conv_b, b_i, b_f, norm_w, B, T):
    L = math.gcd(T, ML_CHUNK)
    NC = T // L
    N = B * T
    Bb = 8 if B % 8 == 0 else B
    gate_bias = jnp.zeros((1, LANE), F32).at[0, 0:ML_HEADS].set(b_i).at[0, ML_HEADS:2 * ML_HEADS].set(b_f)
    m0p = jnp.pad(m0.reshape(B, 1, ML_HEADS), ((0, 0), (0, 0), (0, LANE - ML_HEADS)))
    per_b = lambda shape: pl.BlockSpec((Bb,) + shape, lambda b, c: (b,) + (0,) * len(shape))
    const = lambda shape: pl.BlockSpec(shape, lambda b, c: (0,) * len(shape))
    rows = lambda w: pl.BlockSpec((Bb, L, w), lambda b, c: (b, c, 0))
    state_shapes = [(ML_HEADS, HEAD_DIM, HEAD_DIM), (ML_HEADS, HEAD_DIM), (1, LANE), (ML_CONV - 1, 2 * ML_WIDTH)]
    h, C, n, m, conv = pl.pallas_call(
        functools.partial(_mlstm_kernel, Bb=Bb, L=L),
        grid=(B // Bb, NC),
        in_specs=[rows(1024), rows(LANE), per_b(state_shapes[3]), per_b(state_shapes[0]), per_b(state_shapes[1]),
                  per_b(state_shapes[2]), const((ML_CONV, 2 * ML_WIDTH)), const((1, 2 * ML_WIDTH)),
                  const((1, LANE)), const((1, ML_WIDTH))],
        out_specs=[rows(ML_WIDTH)] + [per_b(s) for s in state_shapes],
        out_shape=[jax.ShapeDtypeStruct((B, T, ML_WIDTH), F32)]
                  + [jax.ShapeDtypeStruct((B,) + s, F32) for s in state_shapes],
        scratch_shapes=[pltpu.VMEM((Bb, L + 8, 2 * ML_WIDTH), F32)],
        compiler_params=_cp(("parallel", "arbitrary")),
        name="mlstm",
    )(ml.reshape(B, T, 1024), sm.reshape(B, T, LANE), conv0, C0, n0, m0p, conv_w, conv_b.reshape(1, -1), gate_bias,
      norm_w.reshape(1, -1))
    return h.reshape(N, ML_WIDTH), C, n, m[:, 0, :ML_HEADS], conv


def _rwkv_prep_kernel(rw_ref, shift0_ref, mu_ref, w0_ref, w2_ref, a0_ref, a2_ref, g2_ref, kk_ref, ka_ref, rk_ref,
                      p_ref, gb_ref, shift_ref, carry_ref, *, Bb, tt):
    c = pl.program_id(1)
    W = RW_WIDTH
    C = rw_ref.shape[1]

    @pl.when(c == 0)
    def _():
        carry_ref[...] = shift0_ref[...]

    x = rw_ref[...]
    carry_rows = jnp.broadcast_to(carry_ref[...], (Bb, tt, C)).reshape(Bb * tt, C)
    prev = jnp.where(_iota((Bb * tt, 1), 0) % tt == 0, carry_rows, pltpu.roll(x, 1, 0))
    last = x.reshape(Bb, tt, C)[:, tt - 1:tt, :]
    carry_ref[...] = last
    shift_ref[...] = last
    xm = x + (prev - x) * mu_ref[...]
    r = xm[:, 0:W]
    k = xm[:, W:2 * W]
    v = xm[:, 2 * W:3 * W]
    wl = xm[:, 3 * W:3 * W + RW_LORA_W]
    al = xm[:, 3 * W + RW_LORA_W:3 * W + RW_LORA_W + RW_LORA_A]
    gl = xm[:, 3 * W + RW_LORA_W + RW_LORA_A:]
    wpre = w0_ref[...] + _dot(jnp.tanh(wl), w2_ref[...], HIGHEST)
    softplus_neg = jnp.maximum(-wpre, 0.0) + jnp.log1p(jnp.exp(-jnp.abs(wpre)))
    decay = jnp.exp(-jnp.exp(-softplus_neg - 0.5))
    a = _sigmoid(a0_ref[...] + _dot(al, a2_ref[...], HIGHEST))
    g = _dot(_sigmoid(gl), g2_ref[...], HIGHEST)
    ones = _head_ones(W)
    kk = k * kk_ref[...]
    kk = kk / jnp.maximum(jnp.sqrt(_dot(kk * kk, ones, HIGHEST)), 1e-12)
    k2 = k * (1.0 + (a - 1.0) * ka_ref[...])
    bonus = _dot(r * k2 * rk_ref[...], ones, HIGHEST) * v
    p_ref[:, 0:W] = r
    p_ref[:, W:2 * W] = decay
    p_ref[:, 2 * W:3 * W] = k2
    p_ref[:, 3 * W:4 * W] = v
    p_ref[:, 4 * W:5 * W] = kk
    p_ref[:, 5 * W:6 * W] = a * kk
    gb_ref[:, 0:W] = g
    gb_ref[:, W:2 * W] = bonus


def _rwkv_chunk_kernel(p_ref, s0_ref, o_ref, s_ref, *, Bb, L):
    c = pl.program_id(1)
    W, D = RW_WIDTH, HEAD_DIM

    @pl.when(c == 0)
    def _():
        s_ref[...] = s0_ref[...]

    row, col = _iota((L, L), 0), _iota((L, L), 1)
    incl = col <= row
    strict = col < row
    tril_b = jnp.where(incl, 1.0, 0.0).astype(BF16)
    mm = lambda a, b: _dot(a.astype(BF16), b.astype(BF16))
    mm_nt = lambda a, b: _dot_nt(a.astype(BF16), b.astype(BF16))
    mm_tn = lambda a, b: _dot_tn(a.astype(BF16), b.astype(BF16))

    units = []
    for b in range(Bb):
        x = p_ref[b]
        r, w, k, v, kk, bb = (x[:, j * W:(j + 1) * W] for j in range(6))
        logw = jnp.log(w)
        logw_hi = logw.astype(BF16)
        logw_lo = (logw - logw_hi.astype(F32)).astype(BF16)
        cum = _dot(tril_b, logw_hi) + _dot(tril_b, logw_lo)
        g = jnp.exp(cum)
        ginv = jnp.exp(-cum)
        khat = kk * jnp.exp(cum - logw)
        rhat = r * g
        kd = k * ginv
        bd = bb * ginv
        glast = g[L - 1:L, :]
        for h in range(RW_HEADS):
            sl = slice(h * D, (h + 1) * D)
            units.append(dict(b=b, h=h, sl=sl, khat=khat[:, sl].astype(BF16), rhat=rhat[:, sl].astype(BF16),
                              kd=kd[:, sl], bd=bd[:, sl], v=v[:, sl].astype(BF16), gl=glast[:, sl]))

    for u in units:
        S0 = s_ref[u['b'], u['h']]
        kdb, bdb = u['kd'].astype(BF16), u['bd'].astype(BF16)
        if 2 * L == LANE:
            lhs = jnp.concatenate([u['khat'], u['rhat']], axis=0)
            m1 = _dot_nt(lhs, jnp.concatenate([bdb, kdb], axis=0))
            m2 = _dot_nt(lhs, S0.astype(BF16))
            q_raw, p_raw, aub_raw, avk_raw = m1[0:L, 0:L], m1[0:L, L:], m1[L:, 0:L], m1[L:, L:]
            ks0, u['rs0'] = m2[0:L], m2[L:]
        else:
            q_raw, p_raw = _dot_nt(u['khat'], bdb), _dot_nt(u['khat'], kdb)
            aub_raw, avk_raw = _dot_nt(u['rhat'], bdb), _dot_nt(u['rhat'], kdb)
            ks0, u['rs0'] = _dot_nt(u['khat'], S0.astype(BF16)), _dot_nt(u['rhat'], S0.astype(BF16))
        u['Q'] = jnp.where(strict, q_raw, 0.0).astype(BF16)
        u['P'] = jnp.where(strict, p_raw, 0.0).astype(BF16)
        u['avk'] = jnp.where(incl, avk_raw, 0.0).astype(BF16)
        u['aub'] = jnp.where(incl, aub_raw, 0.0).astype(BF16)
        u['X'] = ks0
    for u in units:
        u['X'] = u['X'] + _dot(u['P'], u['v'])
    for u in units:
        u['X'] = u['X'] - mm(u['Q'], u['X'])
    n = 2
    while n < L:
        for u in units:
            u['Q'] = _dot(u['Q'], u['Q']).astype(BF16)
        for u in units:
            u['X'] = u['X'] + mm(u['Q'], u['X'])
        n *= 2
    for u in units:
        xb = u['X'].astype(BF16)
        o_ref[u['b'], :, u['sl']] = u['rs0'] + _dot(u['avk'], u['v']) - _dot(u['aub'], xb)
        gl = u['gl']
        S0 = s_ref[u['b'], u['h']]
        s_ref[u['b'], u['h']] = S0 * gl + _dot_tn(u['v'], (u['kd'] * gl).astype(BF16)) - _dot_tn(xb, (u['bd'] * gl).astype(BF16))


def _rwkv(rw, shift0p, S0, mu_p, w0, w2, a0, a2, g2, k_k, k_a, r_k, B, T):
    N = B * T
    W = RW_WIDTH
    tt = min(T, PREP_ROWS)
    ntt = T // tt
    Bp = math.gcd(B, PREP_ROWS // tt)
    assert ntt == 1 or Bp == 1
    const = lambda shape: pl.BlockSpec(shape, lambda b, c: (0,) * len(shape))
    rows = lambda w: pl.BlockSpec((Bp * tt, w), lambda b, c: (b * ntt + c, 0))
    per_seq = pl.BlockSpec((Bp, 1, 1024), lambda b, c: (b, 0, 0))
    vec = lambda a: a.reshape(1, -1)
    p, gb, shift = pl.pallas_call(
        functools.partial(_rwkv_prep_kernel, Bb=Bp, tt=tt),
        grid=(B // Bp, ntt),
        in_specs=[rows(1024), per_seq, const((1, 1024)),
                  const((1, W)), const((RW_LORA_W, W)), const((1, W)), const((RW_LORA_A, W)), const((RW_LORA_G, W)),
                  const((1, W)), const((1, W)), const((1, W))],
        out_specs=[rows(6 * W), rows(2 * W), per_seq],
        out_shape=[jax.ShapeDtypeStruct((N, 6 * W), F32), jax.ShapeDtypeStruct((N, 2 * W), F32),
                   jax.ShapeDtypeStruct((B, 1, 1024), F32)],
        scratch_shapes=[pltpu.VMEM((Bp, 1, 1024), F32)],
        compiler_params=_cp(("parallel", "arbitrary")),
        name="rwkv_prep",
    )(rw, shift0p.reshape(B, 1, 1024), vec(mu_p), vec(w0), w2, vec(a0), a2, g2, vec(k_k), vec(k_a), vec(r_k))

    Bb = 8 if B % 8 == 0 else B
    L = math.gcd(T, RW_CHUNK)
    blk = lambda w: pl.BlockSpec((Bb, L, w), lambda b, c: (b, c, 0))
    st = pl.BlockSpec((Bb, RW_HEADS, HEAD_DIM, HEAD_DIM), lambda b, c: (b, 0, 0, 0))
    y, S = pl.pallas_call(
        functools.partial(_rwkv_chunk_kernel, Bb=Bb, L=L),
        grid=(B // Bb, T // L),
        in_specs=[blk(6 * W), st],
        out_specs=[blk(W), st],
        out_shape=[jax.ShapeDtypeStruct((B, T, W), F32), jax.ShapeDtypeStruct(S0.shape, F32)],
        compiler_params=_cp(("parallel", "arbitrary")),
        name="rwkv_scan",
    )(p.reshape(B, T, 6 * W), S0)
    return y.reshape(N, W), gb, S, shift[:, 0]


def _compress_slab(xg, j, posl_ref, posh_ref, wl_ref, wh_ref, w2_ref):
    G = xg.shape[0]
    lo = _dot((xg + posl_ref[j]).astype(BF16), wl_ref[j])
    hi = _dot((xg + posh_ref[j]).astype(BF16), wh_ref[j])
    hid = _silu(lo + pltpu.roll(hi, G - 1, 0))
    return _dot(hid.astype(BF16), w2_ref[j])


def _compress_kernel(xk_ref, xv_ref, posl_ref, posh_ref, wl_ref, wh_ref, w2_ref, o_ref, *, G):
    for j, x_ref in enumerate((xk_ref, xv_ref)):
        xg = jnp.concatenate([x_ref.at[0][pl.ds(r, G, stride=GRP_ROWS), :] for r in range(GRP_ROWS)], axis=-1)
        o_ref[0, :, j * KV_WIDTH:(j + 1) * KV_WIDTH] = _compress_slab(xg, j, posl_ref, posh_ref, wl_ref, wh_ref, w2_ref)


def _pack_cmp_weights(cmp_pos, cmp_w1, cmp_w2):
    eye = jnp.eye(NSA_KV_HEADS, dtype=F32)
    w1 = cmp_w1.reshape(2, CMP_LEN, HEAD_DIM, CMP_HID)
    wbig = jnp.einsum('jrde,bg->jrbdge', w1, eye).reshape(2, CMP_LEN, KV_WIDTH, NSA_KV_HEADS * CMP_HID).astype(BF16)
    w_lo = wbig[:, :GRP_ROWS].reshape(2, GRP_ROWS * KV_WIDTH, NSA_KV_HEADS * CMP_HID)
    w_hi = wbig[:, GRP_ROWS:].reshape(2, GRP_ROWS * KV_WIDTH, NSA_KV_HEADS * CMP_HID)
    w2big = jnp.einsum('jed,bg->jbegd', cmp_w2, eye).reshape(2, NSA_KV_HEADS * CMP_HID, KV_WIDTH).astype(BF16)
    pos = jnp.broadcast_to(cmp_pos[:, :, None, :], (2, CMP_LEN, NSA_KV_HEADS, HEAD_DIM))
    pos_lo = pos[:, :GRP_ROWS].reshape(2, 1, -1)
    pos_hi = pos[:, GRP_ROWS:].reshape(2, 1, -1)
    return w_lo, w_hi, w2big, pos_lo, pos_hi


_CMP_K = GRP_ROWS * KV_WIDTH
_CMP_SPECS = [(2, 1, _CMP_K), (2, 1, _CMP_K), (2, _CMP_K, NSA_KV_HEADS * CMP_HID), (2, _CMP_K, NSA_KV_HEADS * CMP_HID),
              (2, NSA_KV_HEADS * CMP_HID, KV_WIDTH)]


def _overlap_map(n_rows, n_cmp, n_sel):
    cs = np.arange(n_rows) * CMP_STRIDE
    ss = np.arange(LANE) * SEL_BLOCK
    ov = np.clip(np.minimum(cs[:, None] + CMP_LEN, ss[None, :] + SEL_BLOCK) - np.maximum(cs[:, None], ss[None, :]),
                 0, None) / CMP_LEN
    ov[n_cmp:, :] = 0.0
    ov[:, n_sel:] = 0.0
    return ov


def _compress(xk, xv, cmp_pos, cmp_w1, cmp_w2, B, G):
    rows = xk.shape[1]
    w_lo, w_hi, w2big, pos_lo, pos_hi = _pack_cmp_weights(cmp_pos, cmp_w1, cmp_w2)
    const = lambda shape: pl.BlockSpec(shape, lambda b: (0,) * len(shape))
    slab = pl.BlockSpec((1, rows, KV_WIDTH), lambda b: (b, 0, 0))
    return pl.pallas_call(
        functools.partial(_compress_kernel, G=G),
        grid=(B,),
        in_specs=[slab, slab] + [const(s) for s in _CMP_SPECS],
        out_specs=pl.BlockSpec((1, G, 2 * KV_WIDTH), lambda b: (b, 0, 0)),
        out_shape=jax.ShapeDtypeStruct((B, G, 2 * KV_WIDTH), F32),
        compiler_params=_cp(("parallel",)),
        name="nsa_compress",
    )(xk, xv, pos_lo, pos_hi, w_lo, w_hi, w2big)


def _softmax_pv(s, bias, v_aug, *, QB, v_keys_minor):
    G, D = NSA_GROUP, HEAD_DIM
    K = s.shape[-1]
    s3 = s.reshape(G, QB, K) + bias[None]
    p = jnp.exp(s3 - jnp.max(s3, axis=-1, keepdims=True)).astype(BF16).reshape(G * QB, K)
    oa = _dot_nt(p, v_aug) if v_keys_minor else _dot(p, v_aug)
    return oa[:, 0:D] / jnp.maximum(oa[:, D:D + 1], 1e-30)


def _cmp_select(units, ov, QB, NS, n_rank):
    G = NSA_GROUP
    blk = _iota((1, LANE), 1)
    for u in units:
        u['s_c'] = _dot_nt(u['q_raw'], u['ck'].astype(BF16))
    for u in units:
        cmp_end = _iota((1, u['ck'].shape[0]), 1) * CMP_STRIDE + (CMP_LEN - 1)
        u['p_c'] = _masked_softmax(u['s_c'], cmp_end <= u['tq_r'])
    for u in units:
        p_c = u['p_c']
        u['o_c'] = _dot(p_c.astype(BF16), u['cv'].astype(BF16))
        psum = p_c[0:QB]
        for h in range(1, G):
            psum = psum + p_c[h * QB:(h + 1) * QB]
        imp = _dot(psum, ov, HIGHEST)
        tq = u['tq']
        cur = tq // SEL_BLOCK
        avail = blk * SEL_BLOCK <= tq
        forced = (blk == 0) | (blk == cur) | (blk == cur - 1)
        u['score'] = jnp.where(avail, imp + jnp.where(forced, FORCE_BONUS, 0.0), SEL_NEG)
        u['cnt'] = jnp.zeros((QB, LANE), F32)
    for i in range(n_rank):
        for u in units:
            score = u['score']
            col = score[:, i:i + 1]
            u['cnt'] = u['cnt'] + jnp.where((col > score) | ((col >= score) & (blk > i)), 1.0, 0.0)
    for u in units:
        u['sel'] = ((u['cnt'] < min(SEL_TOPN, NS)) & (u['score'] > 0.5 * SEL_NEG)).astype(BF16)


def _head_stack(x, base, g):
    G, D = NSA_GROUP, HEAD_DIM
    return jnp.concatenate([x[:, base + (g * G + h) * D:base + (g * G + h + 1) * D] for h in range(G)], 0)


def _gate_stack(gates, g, j):
    c0 = 2 * ML_HEADS + 3 * g * NSA_GROUP + j
    return jnp.concatenate([gates[:, c0 + 3 * h:c0 + 3 * h + 1] for h in range(NSA_GROUP)], 0)


def _nsa_kernel(q_ref, sm_ref, ckv_ref, kv_ref, win_ref, ov_ref, ex_ref, o_ref, skb, svb, wkb, wvb, *,
                QB, NB, NS, P, WN, offset, win_pos0, win_band, nvar):
    nb = pl.program_id(1)
    D = HEAD_DIM
    G = NSA_GROUP
    R = G * QB

    @pl.when(nb == 0)
    def _():
        for g in range(NSA_KV_HEADS):
            skb[g] = kv_ref[0, :, g * D:(g + 1) * D].astype(BF16)
            svb[g, :, 0:D] = kv_ref[0, :, KV_WIDTH + g * D:KV_WIDTH + (g + 1) * D].astype(BF16)
            svb[g, :, D:2 * D] = jnp.ones((P, D), BF16)
            wkb[g] = win_ref[0, :, g * D:(g + 1) * D].astype(BF16)
            wvb[g, :, 0:D] = win_ref[0, :, KV_WIDTH + g * D:KV_WIDTH + (g + 1) * D].astype(BF16)
            wvb[g, :, D:2 * D] = jnp.ones((wvb.shape[1], D), BF16)

    softmax_pv = functools.partial(_softmax_pv, QB=QB, v_keys_minor=False)

    def body(PS):
        qs = nb * QB
        tq = offset + qs + _iota((QB, 1), 0)
        tq_r = offset + qs + (_iota((R, 1), 0) % QB)
        gates = _sigmoid(sm_ref[...])
        kpos = _iota((1, PS), 1)
        if win_band:
            wstart = pl.multiple_of(jnp.maximum(qs - WINDOW, 0), 16)
        else:
            wstart = 0
        dlt = tq - (win_pos0 + wstart + _iota((1, WN), 1))
        bias_w = jnp.where((dlt >= 0) & (dlt < WINDOW), 0.0, NEG)
        qv = q_ref[...] * (D ** -0.5)

        units = [dict(g=g, tq=tq, tq_r=tq_r, q_raw=_head_stack(qv, 0, g).astype(BF16),
                      q_rot=_head_stack(qv, NSA_WIDTH, g).astype(BF16), ck=ckv_ref[0, :, g * D:(g + 1) * D],
                      cv=ckv_ref[0, :, KV_WIDTH + g * D:KV_WIDTH + (g + 1) * D]) for g in range(NSA_KV_HEADS)]
        for u in units:
            u['s_s'] = _dot_nt(u['q_rot'], skb[u['g'], 0:PS, :])
            u['s_w'] = _dot_nt(u['q_rot'], wkb[u['g'], pl.ds(wstart, WN), :])
        for u in units:
            u['o_w'] = softmax_pv(u['s_w'], bias_w, wvb[u['g'], pl.ds(wstart, WN), :])
        _cmp_select(units, ov_ref[...], QB, NS, min(NS, PS // SEL_BLOCK))
        outs = []
        for u in units:
            g = u['g']
            selk = _dot(u['sel'], ex_ref[:, 0:PS])
            bias_s = jnp.where((selk > 0.5) & (kpos <= tq), 0.0, NEG)
            o_s = softmax_pv(u['s_s'], bias_s, svb[g, 0:PS, :])
            o = _gate_stack(gates, g, 0) * u['o_c'] + _gate_stack(gates, g, 1) * o_s + _gate_stack(gates, g, 2) * u['o_w']
            outs.extend(o[h * QB:(h + 1) * QB] for h in range(G))
        o_ref[...] = jnp.concatenate(outs, axis=-1)

    for v in range(nvar):
        pl.when(nb // (NB // nvar) == v)(functools.partial(body, P * (v + 1) // nvar))


def _nsa_attend(qq, sm, ckv, kv_rows, win_rows, B, T, offset, p_len, win_pos0, win_band):
    N = B * T
    QB = math.gcd(T, Q_BLOCK)
    NB = T // QB
    P = kv_rows.shape[1]
    NS = -(-p_len // SEL_BLOCK)
    NC = (p_len - CMP_LEN) // CMP_STRIDE + 1
    NCP = ckv.shape[1]
    assert NS <= LANE and NC <= NCP and P % LANE == 0
    LW = win_rows.shape[1]
    WN = WINDOW + QB if win_band else LW
    ov = _overlap_map(NCP, NC, NS)
    expand = (np.arange(P)[None, :] // SEL_BLOCK == np.arange(LANE)[:, None]).astype(np.float32)
    const = lambda shape: pl.BlockSpec(shape, lambda b, nb: (0,) * len(shape))
    rows = lambda w: pl.BlockSpec((QB, w), lambda b, nb: (b * NB + nb, 0))
    causal_only = offset == 0 and P == T
    nvar = 4 if (causal_only and NB % 4 == 0 and (P // 4) % LANE == 0) else 1
    return pl.pallas_call(
        functools.partial(_nsa_kernel, QB=QB, NB=NB, NS=NS, P=P, WN=WN, offset=offset, win_pos0=win_pos0,
                          win_band=win_band, nvar=nvar),
        grid=(B, NB),
        in_specs=[rows(2 * NSA_WIDTH), rows(LANE),
                  pl.BlockSpec((1, NCP, 2 * KV_WIDTH), lambda b, nb: (b, 0, 0)),
                  pl.BlockSpec((1, P, 2 * KV_WIDTH), lambda b, nb: (b, 0, 1)),
                  pl.BlockSpec((1, LW, 2 * KV_WIDTH), lambda b, nb: (b, 0, 0)),
                  const((NCP, LANE)), const((LANE, P))],
        out_specs=rows(NSA_WIDTH),
        out_shape=jax.ShapeDtypeStruct((N, NSA_WIDTH), F32),
        scratch_shapes=[pltpu.VMEM((NSA_KV_HEADS, P, HEAD_DIM), BF16), pltpu.VMEM((NSA_KV_HEADS, P, 2 * HEAD_DIM), BF16),
                        pltpu.VMEM((NSA_KV_HEADS, LW, HEAD_DIM), BF16),
                        pltpu.VMEM((NSA_KV_HEADS, LW, 2 * HEAD_DIM), BF16)],
        compiler_params=_cp(("parallel", "arbitrary")),
        name="nsa_attend",
    )(qq, sm, ckv, kv_rows, win_rows, jnp.asarray(ov, F32), jnp.asarray(expand, BF16))


def _nsa_paged_kernel(pt_ref, *refs, SEQ, n_pages, page, T, NS, offset, lwin):
    del pt_ref
    page_refs = refs[:SEQ * n_pages]
    (q_ref, sm_ref, new_ref, wcache_ref, wnew_ref, posl_ref, posh_ref, wl_ref, wh_ref, w2_ref, ov_ref, ex_ref,
     o_ref, wout_ref, x_ref) = refs[SEQ * n_pages:]
    D, G = HEAD_DIM, NSA_GROUP
    QB = T
    R = G * QB
    P = (n_pages + 1) * page
    NG = n_pages * page // GRP_ROWS
    pages = lambda s: page_refs[s * n_pages:(s + 1) * n_pages]

    def new_tile(x, c0):
        return x[:, c0:c0 + D].T.astype(BF16)

    for s in range(SEQ):
        for i in range(n_pages):
            for j in range(2):
                for g in range(NSA_KV_HEADS):
                    x_ref[s, j, i * page:(i + 1) * page, g * D:(g + 1) * D] = pages(s)[i][0, j, g].T
    ckv = {}
    for s in range(SEQ):
        for j in range(2):
            xg = jnp.concatenate([x_ref.at[s, j][pl.ds(r, NG, stride=GRP_ROWS), :] for r in range(GRP_ROWS)], axis=-1)
            ckv[s, j] = _compress_slab(xg, j, posl_ref, posh_ref, wl_ref, wh_ref, w2_ref)

    tq = offset + _iota((QB, 1), 0)
    tq_r = offset + (_iota((R, 1), 0) % QB)
    kpos = _iota((1, P), 1)
    dlt = tq - (offset - lwin + _iota((1, lwin + page), 1))
    bias_w = jnp.where((dlt >= 0) & (dlt < WINDOW), 0.0, NEG)
    zrows = lambda w: jnp.zeros((page - T, w), F32)
    softmax_pv = functools.partial(_softmax_pv, QB=QB, v_keys_minor=True)

    units = []
    for s in range(SEQ):
        rows = slice(s * T, (s + 1) * T)
        qv = q_ref[rows, :] * (D ** -0.5)
        gates = _sigmoid(sm_ref[rows, :])
        new_t = jnp.concatenate([new_ref[rows, :], zrows(KV_ROW)], axis=0)
        wnew_t = jnp.concatenate([wnew_ref[rows, :], zrows(2 * KV_WIDTH)], axis=0)
        for g in range(NSA_KV_HEADS):
            kt = jnp.concatenate([pg[0, 2, g].astype(BF16) for pg in pages(s)]
                                 + [new_tile(new_t, 2 * KV_WIDTH + g * D)], axis=1)
            vt = jnp.concatenate([pg[0, 3, g].astype(BF16) for pg in pages(s)]
                                 + [new_tile(new_t, 3 * KV_WIDTH + g * D)], axis=1)
            wkt = jnp.concatenate([wcache_ref[s, 0, g].astype(BF16), new_tile(wnew_t, g * D)], axis=1)
            wvt = jnp.concatenate([wcache_ref[s, 1, g].astype(BF16), new_tile(wnew_t, KV_WIDTH + g * D)], axis=1)
            units.append(dict(s=s, g=g, tq=tq, tq_r=tq_r, gates=gates, q_raw=_head_stack(qv, 0, g).astype(BF16),
                              q_rot=_head_stack(qv, NSA_WIDTH, g).astype(BF16), kt=kt, vt=vt, wkt=wkt, wvt=wvt,
                              ck=ckv[s, 0][:, g * D:(g + 1) * D], cv=ckv[s, 1][:, g * D:(g + 1) * D]))
    for u in units:
        u['s_s'] = _dot(u['q_rot'], u['kt'])
        u['s_w'] = _dot(u['q_rot'], u['wkt'])
    for u in units:
        u['o_w'] = softmax_pv(u['s_w'], bias_w, jnp.concatenate([u['wvt'], jnp.ones_like(u['wvt'])], axis=0))
    _cmp_select(units, ov_ref[...], QB, NS, NS)
    for u in units:
        bias_s = jnp.where((_dot(u['sel'], ex_ref[...]) > 0.5) & (kpos <= tq), 0.0, NEG)
        o_s = softmax_pv(u['s_s'], bias_s, jnp.concatenate([u['vt'], jnp.ones_like(u['vt'])], axis=0))
        gates, g = u['gates'], u['g']
        u['o'] = _gate_stack(gates, g, 0) * u['o_c'] + _gate_stack(gates, g, 1) * o_s + _gate_stack(gates, g, 2) * u['o_w']
    for s in range(SEQ):
        outs = [u['o'][h * QB:(h + 1) * QB] for u in units if u['s'] == s for h in range(G)]
        o_ref[s * T:(s + 1) * T, :] = jnp.concatenate(outs, axis=-1)
    for s in range(SEQ):
        wnew_t = jnp.concatenate([wnew_ref[s * T:(s + 1) * T, :], zrows(2 * KV_WIDTH)], axis=0)
        for kv in range(2):
            for g in range(NSA_KV_HEADS):
                c0 = kv * KV_WIDTH + g * D
                wout_ref[s, kv, g] = jnp.concatenate([wcache_ref[s, kv, g][:, T:], wnew_t[:, c0:c0 + D].T[:, 0:T]], axis=1)


def _nsa_paged(qq, sm, kv_new, win_new, cache_t, page_ids, wcache_t, win_base, cmp_pos, cmp_w1, cmp_w2, B, T, offset):
    N = B * T
    n_pages = page_ids.shape[1]
    page = cache_t.shape[-1]
    lwin = wcache_t.shape[-1]
    p_len = n_pages * page + T
    P = (n_pages + 1) * page
    NS = -(-p_len // SEL_BLOCK)
    n_cmp = (p_len - CMP_LEN) // CMP_STRIDE + 1
    NG = n_pages * page // GRP_ROWS
    SEQ = 2 if (B % 2 == 0 and win_base % 2 == 0) else 1
    assert T % 8 == 0 and T <= page and NS <= LANE and n_cmp + 1 == NG and lwin % LANE == 0 and NG % 8 == 0
    w_lo, w_hi, w2big, pos_lo, pos_hi = _pack_cmp_weights(cmp_pos, cmp_w1, cmp_w2)
    ov = _overlap_map(NG, n_cmp, NS)
    expand = (np.arange(P)[None, :] // SEL_BLOCK == np.arange(LANE)[:, None]).astype(np.float32)

    def page_map(k, b, pt):
        return (pt[b * SEQ * n_pages + k], 0, 0, 0, 0)

    const = lambda shape: pl.BlockSpec(shape, lambda b, pt: (0,) * len(shape))
    rows = lambda w: pl.BlockSpec((SEQ * T, w), lambda b, pt: (b, 0))
    in_specs = [pl.BlockSpec((1, 4, NSA_KV_HEADS, HEAD_DIM, page), functools.partial(page_map, k))
                for k in range(SEQ * n_pages)]
    wshape = (SEQ, 2, NSA_KV_HEADS, HEAD_DIM, lwin)
    in_specs += [rows(2 * NSA_WIDTH), rows(LANE), rows(KV_ROW),
                 pl.BlockSpec(wshape, lambda b, pt: (win_base // SEQ + b, 0, 0, 0, 0)),
                 rows(2 * KV_WIDTH)] + [const(s) for s in _CMP_SPECS] + [const((NG, LANE)), const((LANE, P))]
    return pl.pallas_call(
        functools.partial(_nsa_paged_kernel, SEQ=SEQ, n_pages=n_pages, page=page, T=T, NS=NS, offset=offset, lwin=lwin),
        grid_spec=pltpu.PrefetchScalarGridSpec(
            num_scalar_prefetch=1, grid=(B // SEQ,), in_specs=in_specs,
            out_specs=[rows(NSA_WIDTH), pl.BlockSpec(wshape, lambda b, pt: (b, 0, 0, 0, 0))],
            scratch_shapes=[pltpu.VMEM((SEQ, 2, n_pages * page, KV_WIDTH), F32)]),
        out_shape=[jax.ShapeDtypeStruct((N, NSA_WIDTH), F32), jax.ShapeDtypeStruct((B,) + wshape[1:], F32)],
        compiler_params=_cp(("parallel",)),
        name="nsa_paged",
    )(page_ids.reshape(-1), *([cache_t] * (SEQ * n_pages)), qq, sm, kv_new, wcache_t, win_new, pos_lo, pos_hi, w_lo,
      w_hi, w2big, jnp.asarray(ov, F32), jnp.asarray(expand, BF16))


def _outproj_kernel(x_ref, hml_ref, yrw_ref, gb_ref, hns_ref, lnw_ref, lnb_ref, w_ref, o_ref):
    W = RW_WIDTH
    y = yrw_ref[...]
    ones = _head_ones(W) * (1.0 / HEAD_DIM)
    mu = _dot(y, ones, HIGHEST)
    var = _dot(jnp.square(y - mu), ones, HIGHEST)
    yn = (y - mu) * lax.rsqrt(var + RW_GN_EPS) * lnw_ref[...]
    h_rw = (yn + lnb_ref[...] + gb_ref[:, W:2 * W]) * gb_ref[:, 0:W]
    acc = _dot(hml_ref[...].astype(BF16), w_ref[0:ML_WIDTH, :])
    acc = acc + _dot(h_rw.astype(BF16), w_ref[ML_WIDTH:ML_WIDTH + W, :])
    acc = acc + _dot(hns_ref[...].astype(BF16), w_ref[ML_WIDTH + W:, :])
    o_ref[...] = x_ref[...] + acc


def _outproj(x2, h_ml, y_rw, gb, h_ns, ln_w, ln_b, w_out):
    N = x2.shape[0]
    tm = min(512, N)
    row = lambda w: pl.BlockSpec((tm, w), lambda i: (i, 0))
    const = lambda shape: pl.BlockSpec(shape, lambda i: (0,) * len(shape))
    return pl.pallas_call(
        _outproj_kernel,
        grid=(N // tm,),
        in_specs=[row(D_MODEL), row(ML_WIDTH), row(RW_WIDTH), row(2 * RW_WIDTH), row(NSA_WIDTH),
                  const((1, RW_WIDTH)), const((1, RW_WIDTH)), const((MIX_WIDTH, D_MODEL))],
        out_specs=row(D_MODEL),
        out_shape=jax.ShapeDtypeStruct((N, D_MODEL), F32),
        compiler_params=_cp(("parallel",)),
        name="outproj",
    )(x2, h_ml, y_rw, gb, h_ns, ln_w.reshape(1, -1), ln_b.reshape(1, -1), w_out.astype(BF16))


def _rms(x, w):
    return x * lax.rsqrt(jnp.mean(x * x, axis=-1, keepdims=True) + RMS_EPS) * w


def _ffn_kernel(x_ref, nw_ref, w1_ref, w3_ref, w2_ref, fw_ref, o_ref, xn_ref, *, final_norm):
    j = pl.program_id(1)

    @pl.when(j == 0)
    def _():
        x = x_ref[...]
        xn_ref[...] = _rms(x, nw_ref[...]).astype(BF16)
        o_ref[...] = x

    xn = xn_ref[...]
    hid = _silu(_dot(xn, w1_ref[...])) * _dot(xn, w3_ref[...])
    o_ref[...] += _dot(hid.astype(BF16), w2_ref[...])

    if final_norm:
        @pl.when(j == pl.num_programs(1) - 1)
        def _():
            o_ref[...] = _rms(o_ref[...], fw_ref[...])


def _ffn(x2, norm_w, w1, w3, w2, final_w):
    N = x2.shape[0]
    d_ff = w1.shape[1]
    tm = min(512, N)
    nj = 2 if (d_ff // 2) % LANE == 0 else 1
    tf = d_ff // nj
    final_norm = final_w is not None
    fw = (final_w if final_norm else jnp.ones((D_MODEL,), F32)).reshape(1, D_MODEL)
    return pl.pallas_call(
        functools.partial(_ffn_kernel, final_norm=final_norm),
        grid=(N // tm, nj),
        in_specs=[pl.BlockSpec((tm, D_MODEL), lambda i, j: (i, 0)), pl.BlockSpec((1, D_MODEL), lambda i, j: (0, 0)),
                  pl.BlockSpec((D_MODEL, tf), lambda i, j: (0, j)), pl.BlockSpec((D_MODEL, tf), lambda i, j: (0, j)),
                  pl.BlockSpec((tf, D_MODEL), lambda i, j: (j, 0)), pl.BlockSpec((1, D_MODEL), lambda i, j: (0, 0))],
        out_specs=pl.BlockSpec((tm, D_MODEL), lambda i, j: (i, 0)),
        out_shape=jax.ShapeDtypeStruct((N, D_MODEL), F32),
        scratch_shapes=[pltpu.VMEM((tm, D_MODEL), BF16)],
        compiler_params=_cp(("parallel", "arbitrary")),
        name="ffn",
    )(x2, norm_w.reshape(1, D_MODEL), w1.astype(BF16), w3.astype(BF16), w2.astype(BF16), fw)


def _moe_kernel(x_ref, nw_ref, rt_ref, w1_ref, w3_ref, w2_ref, fw_ref, o_ref, xn_ref, comb_ref, *, final_norm):
    e = pl.program_id(1)

    @pl.when(e == 0)
    def _():
        x = x_ref[...]
        xn = _rms(x, nw_ref[...])
        xn_ref[...] = xn.astype(BF16)
        o_ref[...] = x
        lane = _iota((1, LANE), 1).astype(F32)
        logits = jnp.where(lane < N_EXPERTS, _dot(xn, rt_ref[...], HIGHEST), -jnp.inf)
        m1 = jnp.max(logits, axis=-1, keepdims=True)
        i1 = jnp.min(jnp.where(logits == m1, lane, float(LANE)), axis=-1, keepdims=True)
        rest = jnp.where(lane == i1, -jnp.inf, logits)
        m2 = jnp.max(rest, axis=-1, keepdims=True)
        i2 = jnp.min(jnp.where(rest == m2, lane, float(LANE)), axis=-1, keepdims=True)
        e2 = jnp.exp(m2 - m1)
        den = 1.0 + e2
        comb_ref[...] = jnp.where(lane == i1, 1.0 / den, 0.0) + jnp.where(lane == i2, e2 / den, 0.0)

    xn = xn_ref[...]
    comb = comb_ref[...]
    ce = jnp.sum(jnp.where(_iota((1, LANE), 1) == e, comb, 0.0), axis=-1, keepdims=True)
    hid = _silu(_dot(xn, w1_ref[0])) * _dot(xn, w3_ref[0])
    o_ref[...] += ce * _dot(hid.astype(BF16), w2_ref[0])

    if final_norm:
        @pl.when(e == pl.num_programs(1) - 1)
        def _():
            o_ref[...] = _rms(o_ref[...], fw_ref[...])


def _moe(x2, norm_w, router, w1, w3, w2, final_w):
    N = x2.shape[0]
    E, _, d_exp = w1.shape
    assert E == N_EXPERTS and TOP_K == 2
    tm = min(512, N)
    final_norm = final_w is not None
    fw = (final_w if final_norm else jnp.ones((D_MODEL,), F32)).reshape(1, D_MODEL)
    rt = jnp.pad(router, ((0, 0), (0, LANE - E)))
    return pl.pallas_call(
        functools.partial(_moe_kernel, final_norm=final_norm),
        grid=(N // tm, E),
        in_specs=[pl.BlockSpec((tm, D_MODEL), lambda i, e: (i, 0)), pl.BlockSpec((1, D_MODEL), lambda i, e: (0, 0)),
                  pl.BlockSpec((D_MODEL, LANE), lambda i, e: (0, 0)),
                  pl.BlockSpec((1, D_MODEL, d_exp), lambda i, e: (e, 0, 0)),
                  pl.BlockSpec((1, D_MODEL, d_exp), lambda i, e: (e, 0, 0)),
                  pl.BlockSpec((1, d_exp, D_MODEL), lambda i, e: (e, 0, 0)),
                  pl.BlockSpec((1, D_MODEL), lambda i, e: (0, 0))],
        out_specs=pl.BlockSpec((tm, D_MODEL), lambda i, e: (i, 0)),
        out_shape=jax.ShapeDtypeStruct((N, D_MODEL), F32),
        scratch_shapes=[pltpu.VMEM((tm, D_MODEL), BF16), pltpu.VMEM((tm, LANE), F32)],
        compiler_params=_cp(("parallel", "arbitrary")),
        name="moe",
    )(x2, norm_w.reshape(1, D_MODEL), rt, w1.astype(BF16), w3.astype(BF16), w2.astype(BF16), fw)


def _layer(l, x2, B, T, offset, st, P, final_w):
    C0, n0, m0, conv0, S0, shift0, cache, page_ids, win0 = st
    rw_perm = jnp.asarray(_rw_order())
    pos = offset + jnp.arange(T)
    cos2, sin2 = _rope_tables(pos)
    ml, rw, qq, kv4, wrow, sm, cks, cvs = _inproj(x2, P['norm_mix_w'][l], _pack_w_in(P['w_in'][l]), cos2, sin2, T)

    h_ml, C, n, m, conv = _mlstm(ml, sm, conv0, C0, n0, m0, P['ml_conv_w'][l], P['ml_conv_b'][l],
                                 P['ml_b_i'][l], P['ml_b_f'][l], P['ml_norm_w'][l], B, T)

    y_rw, gb, S, shift_p = _rwkv(rw, shift0[:, rw_perm], S0, P['rw_mu'][l][rw_perm], P['rw_w0'][l], P['rw_w2'][l],
                                 P['rw_a0'][l], P['rw_a2'][l], P['rw_g2'][l], P['rw_k_k'][l], P['rw_k_a'][l],
                                 P['rw_r_k'][l].reshape(-1), B, T)
    shift_new = jnp.zeros_like(shift_p).at[:, rw_perm].set(shift_p)

    kv3 = kv4.reshape(B, T, KV_ROW)
    w3 = wrow.reshape(B, T, 2 * KV_WIDTH)
    cmp_w = (P['nsa_cmp_pos'][l], P['nsa_cmp_w1'][l], P['nsa_cmp_w2'][l])
    wbuf = P['wbuf']
    if cache is None:
        n_cmp = (T - CMP_LEN) // CMP_STRIDE + 1
        G = -(-(n_cmp + 1) // 8) * 8
        assert G * GRP_ROWS <= T
        ckv = _compress(cks.reshape(B, T, KV_WIDTH), cvs.reshape(B, T, KV_WIDTH), *cmp_w, B, G)
        h_ns = _nsa_attend(qq, sm, ckv, kv3, w3, B, T, offset, T, 0, True)
        win_new = w3[:, T - wbuf:] if T >= wbuf else jnp.pad(w3, ((0, 0), (wbuf - T, 0), (0, 0)))
        win_new = win_new.reshape(B, wbuf, 2, NSA_KV_HEADS, HEAD_DIM)
    else:
        assert win0['key_minor'].shape[-1] == wbuf and T <= wbuf
        h_ns, wout = _nsa_paged(qq, sm, kv4, wrow, cache, page_ids, win0['key_minor'], win0['base'], *cmp_w, B, T, offset)
        win_new = jnp.transpose(wout, (0, 4, 1, 2, 3))

    x2 = _outproj(x2, h_ml, y_rw, gb, h_ns, P['rw_ln_w'][l], P['rw_ln_b'][l], P['w_out'][l])
    if l % 2 == 0:
        x2 = _ffn(x2, P['norm_ffn_w'][l], P['ffn_w1'][l // 2], P['ffn_w3'][l // 2], P['ffn_w2'][l // 2], final_w)
    else:
        x2 = _moe(x2, P['norm_ffn_w'][l], P['moe_router'][l // 2], P['moe_w1'][l // 2], P['moe_w3'][l // 2],
                  P['moe_w2'][l // 2], final_w)
    new = (kv3.reshape(B, T, 4, NSA_KV_HEADS, HEAD_DIM), win_new, C, n, m, conv, S, shift_new)
    return x2, new


def _run(x, offset, get_state, P, depth):
    B, T, _ = x.shape
    x2 = x.reshape(B * T, D_MODEL)
    news = []
    for l in range(depth):
        x2, nw = _layer(l, x2, B, T, offset, get_state(l), P, P['norm_final_w'] if l == depth - 1 else None)
        news.append(nw)
    return x2.reshape(B, T, D_MODEL), [jnp.stack([nw[i] for nw in news]) for i in range(8)]


def kernel(x_prompt, x_sample, cache_nsa_kv, cache_win_kv, state_mlstm_C, state_mlstm_n, state_mlstm_m, state_mlstm_conv, state_rwkv_S, state_rwkv_shift, page_table, norm_mix_w, w_in, ml_conv_w, ml_conv_b, ml_b_i, ml_b_f, ml_norm_w, rw_mu, rw_w0, rw_w2, rw_a0, rw_a2, rw_g2, rw_k_k, rw_k_a, rw_r_k, rw_ln_w, rw_ln_b, nsa_cmp_pos, nsa_cmp_w1, nsa_cmp_w2, w_out, norm_ffn_w, ffn_w1, ffn_w3, ffn_w2, moe_router, moe_w1, moe_w3, moe_w2, norm_final_w):
    depth = w_in.shape[0]
    Bp = x_prompt.shape[0]
    Bs = x_sample.shape[0]
    n_pool, page = cache_nsa_kv.shape[1], cache_nsa_kv.shape[2]
    past_len = page_table.shape[1] * page
    wbuf = min(WINDOW, past_len)
    P = dict(norm_mix_w=norm_mix_w, w_in=w_in, ml_conv_w=ml_conv_w, ml_conv_b=ml_conv_b, ml_b_i=ml_b_i, ml_b_f=ml_b_f,
             ml_norm_w=ml_norm_w, rw_mu=rw_mu, rw_w0=rw_w0, rw_w2=rw_w2, rw_a0=rw_a0, rw_a2=rw_a2, rw_g2=rw_g2,
             rw_k_k=rw_k_k, rw_k_a=rw_k_a, rw_r_k=rw_r_k, rw_ln_w=rw_ln_w, rw_ln_b=rw_ln_b, nsa_cmp_pos=nsa_cmp_pos,
             nsa_cmp_w1=nsa_cmp_w1, nsa_cmp_w2=nsa_cmp_w2, w_out=w_out, norm_ffn_w=norm_ffn_w, ffn_w1=ffn_w1,
             ffn_w3=ffn_w3, ffn_w2=ffn_w2, moe_router=moe_router, moe_w1=moe_w1, moe_w3=moe_w3, moe_w2=moe_w2,
             norm_final_w=norm_final_w, wbuf=wbuf)

    def prompt_state(l):
        z = lambda *s: jnp.zeros((Bp,) + s, F32)
        return (z(ML_HEADS, HEAD_DIM, HEAD_DIM), z(ML_HEADS, HEAD_DIM), z(ML_HEADS), z(ML_CONV - 1, 2 * ML_WIDTH),
                z(RW_HEADS, HEAD_DIM, HEAD_DIM), z(RW_COLS), None, None, None)

    cache_t = jnp.transpose(cache_nsa_kv, (0, 1, 3, 4, 5, 2)).reshape(depth * n_pool, 4, NSA_KV_HEADS, HEAD_DIM, page)
    wcache_t = jnp.transpose(cache_win_kv, (0, 1, 3, 4, 5, 2)).reshape(depth * Bs, 2, NSA_KV_HEADS, HEAD_DIM, -1)

    def sample_state(l):
        win = dict(key_minor=wcache_t, base=l * Bs)
        return (state_mlstm_C[l], state_mlstm_n[l], state_mlstm_m[l], state_mlstm_conv[l], state_rwkv_S[l],
                state_rwkv_shift[l], cache_t, page_table + l * n_pool, win)

    y_prompt, pst = _run(x_prompt, 0, prompt_state, P, depth)
    y_sample, sst = _run(x_sample, past_len, sample_state, P, depth)
    p_kv, p_win, p_C, p_n, p_m, p_conv, p_S, p_shift = pst
    s_kv, s_win, s_C, s_n, s_m, s_conv, s_S, s_shift = sst
    return (y_prompt, y_sample, p_kv, s_kv, p_win, s_win, p_C, s_C, p_n, s_n, p_m, s_m,
            p_conv, s_conv, p_S, s_S, p_shift, s_shift)
```

```python
import functools
import math

import numpy as np
import jax
import jax.numpy as jnp
from jax import lax
from jax.experimental import pallas as pl
from jax.experimental.pallas import tpu as pltpu

F32 = jnp.float32
BF16 = jnp.bfloat16
HIGHEST = lax.Precision.HIGHEST

D_MODEL = 1024
HEAD_DIM = 64
ML_HEADS = 4
RW_HEADS = 4
NSA_HEADS = 8
NSA_KV_HEADS = 2
NSA_GROUP = NSA_HEADS // NSA_KV_HEADS
ML_WIDTH = ML_HEADS * HEAD_DIM
RW_WIDTH = RW_HEADS * HEAD_DIM
NSA_WIDTH = NSA_HEADS * HEAD_DIM
KV_WIDTH = NSA_KV_HEADS * HEAD_DIM
MIX_WIDTH = ML_WIDTH + RW_WIDTH + NSA_WIDTH
ML_CHUNK = 64
ML_CONV = 4
ML_NORM_EPS = 1e-6
RW_LORA_W = 64
RW_LORA_A = 64
RW_LORA_G = 128
RW_GN_EPS = 64e-5
RW_CHUNK = 64
PREP_ROWS = 256
CMP_LEN = 32
CMP_STRIDE = 16
CMP_HID = 128
SEL_BLOCK = 64
SEL_TOPN = 8
WINDOW = 512
Q_BLOCK = 64
ROPE_THETA = 10000.0
N_EXPERTS = 8
TOP_K = 2
RMS_EPS = 1e-6
NEG = -1e30
SEL_NEG = -1e9
FORCE_BONUS = 1e3
ML_COLS = 4 * ML_WIDTH + 2 * ML_HEADS
RW_COLS = 3 * RW_WIDTH + RW_LORA_W + RW_LORA_A + RW_LORA_G
NSA_COLS = NSA_WIDTH + 6 * KV_WIDTH + 3 * NSA_HEADS

LANE = 128
KV_ROW = 4 * KV_WIDTH
GRP_ROWS = CMP_STRIDE
VMEM_LIMIT = 52 * 1024 * 1024

C_ML = 0
C_RW = 1024
C_Q = 2048
C_QS = 2560
C_KV = 3072
C_SKS = 3584
C_WK = 3712
C_WV = 3840
C_WKS = 3968
C_SM = 4096
C_TOT = 4224


def _cp(sem, vmem=VMEM_LIMIT):
    return pltpu.CompilerParams(dimension_semantics=sem, vmem_limit_bytes=vmem)


def _dot(a, b, precision=None):
    return jnp.dot(a, b, preferred_element_type=F32, precision=precision)


def _dot_nt(a, b, precision=None):
    return lax.dot_general(a, b, (((1,), (1,)), ((), ())), preferred_element_type=F32, precision=precision)


def _dot_tn(a, b, precision=None):
    return lax.dot_general(a, b, (((0,), (0,)), ((), ())), preferred_element_type=F32, precision=precision)


def _sigmoid(x):
    return 1.0 / (1.0 + jnp.exp(-x))


def _silu(x):
    return x * _sigmoid(x)


def _iota(shape, dim):
    return lax.broadcasted_iota(jnp.int32, shape, dim)


def _head_ones(w):
    return jnp.where(_iota((w, w), 0) // HEAD_DIM == _iota((w, w), 1) // HEAD_DIM, 1.0, 0.0).astype(BF16)


def _head_sum(x, ones):
    hi = x.astype(BF16)
    lo = (x - hi.astype(F32)).astype(BF16)
    return _dot(hi, ones) + _dot(lo, ones)


def _masked_softmax(s, mask):
    s = jnp.where(mask, s, NEG)
    m = jnp.max(s, axis=-1, keepdims=True)
    p = jnp.where(mask, jnp.exp(s - m), 0.0)
    return p / jnp.maximum(jnp.sum(p, -1, keepdims=True), 1e-30)


def _swap_half(n_heads):
    idx = np.arange(n_heads * HEAD_DIM)
    return (idx // HEAD_DIM) * HEAD_DIM + (idx % HEAD_DIM + HEAD_DIM // 2) % HEAD_DIM


def _rw_order():
    r0, wl0 = 0, RW_WIDTH
    k0 = wl0 + RW_LORA_W
    v0 = k0 + RW_WIDTH
    al0 = v0 + RW_WIDTH
    gl0 = al0 + RW_LORA_A
    return np.concatenate([np.arange(r0, r0 + RW_WIDTH), np.arange(k0, k0 + RW_WIDTH), np.arange(v0, v0 + RW_WIDTH),
                           np.arange(wl0, wl0 + RW_LORA_W), np.arange(al0, al0 + RW_LORA_A),
                           np.arange(gl0, gl0 + RW_LORA_G)])


def _pack_cols():
    ml0, rw0, ns0 = 0, ML_COLS, ML_COLS + RW_COLS
    q0 = ns0
    kv0 = q0 + NSA_WIDTH
    sk0 = kv0 + 2 * KV_WIDTH
    wk0 = kv0 + 4 * KV_WIDTH
    wv0 = wk0 + KV_WIDTH
    gt0 = wv0 + KV_WIDTH
    small = np.concatenate([np.arange(ml0 + 4 * ML_WIDTH, ml0 + ML_COLS), np.arange(gt0, gt0 + 3 * NSA_HEADS)])
    cols = np.concatenate([
        np.arange(ml0, ml0 + 4 * ML_WIDTH),
        rw0 + _rw_order(),
        np.arange(q0, q0 + NSA_WIDTH),
        q0 + _swap_half(NSA_HEADS),
        np.arange(kv0, kv0 + 4 * KV_WIDTH),
        sk0 + _swap_half(NSA_KV_HEADS),
        np.arange(wk0, wk0 + KV_WIDTH),
        np.arange(wv0, wv0 + KV_WIDTH),
        wk0 + _swap_half(NSA_KV_HEADS),
        small,
    ])
    n_small = small.shape[0]
    assert cols.shape[0] == C_SM + n_small
    return cols, n_small


def _pack_w_in(w):
    cols, n_small = _pack_cols()
    wp = jnp.take(w, jnp.asarray(cols), axis=1)
    wp = jnp.pad(wp, ((0, 0), (0, C_TOT - C_SM - n_small)))
    return wp.astype(BF16)


def _rope_tables(pos):
    half = HEAD_DIM // 2
    inv = ROPE_THETA ** (-jnp.arange(half, dtype=F32) / half)
    ang = pos.astype(F32)[:, None] * inv[None, :]
    cos, sin = jnp.cos(ang), jnp.sin(ang)
    cos2 = jnp.concatenate([cos, cos, cos, cos], -1)
    sin2 = jnp.concatenate([-sin, sin, -sin, sin], -1)
    return cos2, sin2


def _inproj_kernel(x_ref, nw_ref, w_ref, cos_ref, sin_ref, ml_ref, rw_ref, q_ref, kv_ref, win_ref, sm_ref, ck_ref,
                   cv_ref):
    x = x_ref[...]
    xn = x * lax.rsqrt(jnp.mean(x * x, axis=-1, keepdims=True) + RMS_EPS) * nw_ref[...]
    acc = _dot(xn.astype(BF16), w_ref[...])
    cos = cos_ref[...]
    sin = sin_ref[...]
    cos4 = jnp.concatenate([cos] * 4, axis=-1)
    sin4 = jnp.concatenate([sin] * 4, axis=-1)
    ml_ref[...] = acc[:, C_ML:C_ML + 1024]
    rw_ref[...] = acc[:, C_RW:C_RW + 1024]
    q_ref[:, 0:NSA_WIDTH] = acc[:, C_Q:C_Q + NSA_WIDTH]
    q_ref[:, NSA_WIDTH:2 * NSA_WIDTH] = acc[:, C_Q:C_Q + NSA_WIDTH] * cos4 + acc[:, C_QS:C_QS + NSA_WIDTH] * sin4
    kv_ref[:, 0:2 * KV_WIDTH] = acc[:, C_KV:C_KV + 2 * KV_WIDTH]
    kv_ref[:, 2 * KV_WIDTH:3 * KV_WIDTH] = (acc[:, C_KV + 2 * KV_WIDTH:C_KV + 3 * KV_WIDTH] * cos
                                            + acc[:, C_SKS:C_SKS + KV_WIDTH] * sin)
    kv_ref[:, 3 * KV_WIDTH:4 * KV_WIDTH] = acc[:, C_KV + 3 * KV_WIDTH:C_KV + 4 * KV_WIDTH]
    win_ref[:, 0:KV_WIDTH] = acc[:, C_WK:C_WK + KV_WIDTH] * cos + acc[:, C_WKS:C_WKS + KV_WIDTH] * sin
    win_ref[:, KV_WIDTH:2 * KV_WIDTH] = acc[:, C_WV:C_WV + KV_WIDTH]
    sm_ref[...] = acc[:, C_SM:C_SM + LANE]
    ck_ref[...] = acc[:, C_KV:C_KV + KV_WIDTH]
    cv_ref[...] = acc[:, C_KV + KV_WIDTH:C_KV + 2 * KV_WIDTH]


def _inproj(x2, norm_w, w_packed, cos2, sin2, T):
    N = x2.shape[0]
    tm = min(256, N)
    if T >= tm:
        assert T % tm == 0
        nt = T // tm
    else:
        assert tm % T == 0
        cos2 = jnp.tile(cos2, (tm // T, 1))
        sin2 = jnp.tile(sin2, (tm // T, 1))
        nt = 1
    row = lambda w: pl.BlockSpec((tm, w), lambda i: (i, 0))
    tab = pl.BlockSpec((tm, LANE), lambda i: (i % nt, 0))
    widths = (1024, 1024, 2 * NSA_WIDTH, KV_ROW, 2 * KV_WIDTH, LANE, KV_WIDTH, KV_WIDTH)
    return pl.pallas_call(
        _inproj_kernel,
        grid=(N // tm,),
        in_specs=[row(D_MODEL), pl.BlockSpec((1, D_MODEL), lambda i: (0, 0)),
                  pl.BlockSpec((D_MODEL, C_TOT), lambda i: (0, 0)), tab, tab],
        out_specs=[row(w) for w in widths],
        out_shape=[jax.ShapeDtypeStruct((N, w), F32) for w in widths],
        compiler_params=_cp(("parallel",)),
        name="inproj",
    )(x2, norm_w.reshape(1, D_MODEL), w_packed, cos2, sin2)


def _mlstm_kernel(ml_ref, sm_ref, conv0_ref, c0_ref, n0_ref, m0_ref, cw_ref, cb_ref, gb_ref, nw_ref,
                  h_ref, c_ref, n_ref, m_ref, conv_ref, xp_ref, *, Bb, L):
    c = pl.program_id(1)
    H, D = ML_HEADS, HEAD_DIM
    PAD = 8
    KC = ML_CONV - 1

    @pl.when(c == 0)
    def _():
        c_ref[...] = c0_ref[...]
        n_ref[...] = n0_ref[...]
        m_ref[...] = m0_ref[...]
        xp_ref[:, PAD - KC:PAD, :] = conv0_ref[...]

    tril = (_iota((L, L), 1) <= _iota((L, L), 0))
    eye = (_iota((L, L), 1) == _iota((L, L), 0))
    tril_b = jnp.where(tril, 1.0, 0.0).astype(BF16)

    bs = [dict(b=b) for b in range(Bb)]
    for s in bs:
        b = s['b']
        xp_ref[b, PAD:PAD + L, :] = ml_ref[b, :, 0:2 * ML_WIDTH]
    for s in bs:
        b = s['b']
        y = jnp.broadcast_to(cb_ref[...], (L, 2 * ML_WIDTH))
        for j in range(ML_CONV):
            y = y + xp_ref[b, PAD - KC + j:PAD - KC + j + L, :] * cw_ref[j:j + 1, :]
        s['qk'] = _silu(y)
        gates = sm_ref[b] + gb_ref[...]
        lf = jnp.minimum(gates, 0.0) - jnp.log1p(jnp.exp(-jnp.abs(gates)))
        lf_hi = lf.astype(BF16)
        lf_lo = (lf - lf_hi.astype(F32)).astype(BF16)
        s['gates'] = gates
        s['bcum'] = _dot(tril_b, lf_hi) + _dot(tril_b, lf_lo)
    for s in bs:
        b = s['b']
        tail = xp_ref[b, PAD + L - KC:PAD + L, :]
        xp_ref[b, PAD - KC:PAD, :] = tail
        conv_ref[b] = tail
    for s in bs:
        s['b_all'] = pltpu.roll(s['bcum'], LANE - H, 1)
        s['a_all'] = s['gates'] - s['b_all']
        s['cmax'] = s['a_all']
    step = 1
    while step < L:
        for s in bs:
            s['cmax'] = jnp.where(_iota((L, 1), 0) >= step, jnp.maximum(s['cmax'], pltpu.roll(s['cmax'], step, 0)),
                                  s['cmax'])
        step *= 2
    units = []
    for s in bs:
        b, b_all, a_all = s['b'], s['b_all'], s['a_all']
        m_prev = m_ref[b]
        m_t = b_all + jnp.maximum(m_prev, s['cmax'])
        m_new = m_t[L - 1:L, :]
        b_last = b_all[L - 1:L, :]
        m_ref[b] = m_new
        inter = jnp.exp(b_all + m_prev - m_t)
        wgt = jnp.exp(b_last + a_all - m_new)
        dec = jnp.exp(b_last + m_prev - m_new)
        ccol = b_all - m_t
        einv = jnp.exp(-m_t)
        qk = s['qk']
        for h in range(H):
            lo = h * D
            col = lambda x: x[:, h:h + 1]
            units.append(dict(b=b, h=h, lo=lo, q=qk[:, lo:lo + D], k=qk[:, ML_WIDTH + lo:ML_WIDTH + lo + D] * (D ** -0.5),
                              v=ml_ref[b, :, 2 * ML_WIDTH + lo:2 * ML_WIDTH + lo + D], acol=col(a_all),
                              ccol=col(ccol), inter=col(inter), wgt=col(wgt), dec=col(dec), einv=col(einv)))

    ones_col = jnp.ones((L, D), BF16)
    for u in units:
        u['arow'] = jnp.sum(jnp.where(eye, u['acol'], 0.0), axis=0, keepdims=True)
    for u in units:
        u['dm'] = jnp.exp(jnp.where(tril, u['ccol'] + u['arow'], -jnp.inf))
    for u in units:
        C = c_ref[u['b'], u['h']]
        qb, kb = u['q'].astype(BF16), u['k'].astype(BF16)
        if 2 * L == LANE:
            m1 = _dot_nt(qb, jnp.concatenate([kb, C.astype(BF16)], axis=0))
            u['s_raw'], u['qc'] = m1[:, 0:L], m1[:, L:]
        else:
            u['s_raw'], u['qc'] = _dot_nt(qb, kb), _dot_nt(qb, C.astype(BF16))
        c_ref[u['b'], u['h']] = u['dec'] * C + _dot_tn((u['v'] * u['wgt']).astype(BF16), kb)
        nrow = n_ref[u['b'], u['h']:u['h'] + 1, :]
        u['qn'] = jnp.sum(u['q'] * nrow, axis=-1, keepdims=True)
        n_ref[u['b'], u['h']:u['h'] + 1, :] = u['dec'] * nrow + jnp.sum(u['wgt'] * u['k'], axis=0, keepdims=True)
    for u in units:
        S = (u['s_raw'] * u['dm']).astype(BF16)
        sv = _dot(S, jnp.concatenate([u['v'].astype(BF16), ones_col], axis=1))
        den = u['inter'] * u['qn'] + sv[:, D:D + 1]
        u['hh'] = (u['inter'] * u['qc'] + sv[:, 0:D]) / jnp.maximum(jnp.abs(den), u['einv'])
    ones = _head_ones(ML_WIDTH).astype(F32) * (1.0 / D)
    for s in bs:
        s['hh'] = jnp.concatenate([u['hh'] for u in units if u['b'] == s['b']], axis=-1)
        s['mu'] = _dot(s['hh'], ones, HIGHEST)
    for s in bs:
        s['var'] = _dot(jnp.square(s['hh'] - s['mu']), ones, HIGHEST)
    for s in bs:
        b = s['b']
        hn = (s['hh'] - s['mu']) * lax.rsqrt(s['var'] + ML_NORM_EPS) * nw_ref[...]
        h_ref[b] = hn * _sigmoid(ml_ref[b, :, 3 * ML_WIDTH:4 * ML_WIDTH])


def _mlstm(ml, sm, conv0, C0, n0, m0, conv_w, conv_b, b_i, b_f, norm_w, B, T):
    L = math.gcd(T, ML_CHUNK)
    NC = T // L
    N = B * T
    Bb = 8 if B % 8 == 0 else B
    gate_bias = jnp.zeros((1, LANE), F32).at[0, 0:ML_HEADS].set(b_i).at[0, ML_HEADS:2 * ML_HEADS].set(b_f)
    m0p = jnp.pad(m0.reshape(B, 1, ML_HEADS), ((0, 0), (0, 0), (0, LANE - ML_HEADS)))
    per_b = lambda shape: pl.BlockSpec((Bb,) + shape, lambda b, c: (b,) + (0,) * len(shape))
    const = lambda shape: pl.BlockSpec(shape, lambda b, c: (0,) * len(shape))
    rows = lambda w: pl.BlockSpec((Bb, L, w), lambda b, c: (b, c, 0))
    state_shapes = [(ML_HEADS, HEAD_DIM, HEAD_DIM), (ML_HEADS, HEAD_DIM), (1, LANE), (ML_CONV - 1, 2 * ML_WIDTH)]
    h, C, n, m, conv = pl.pallas_call(
        functools.partial(_mlstm_kernel, Bb=Bb, L=L),
        grid=(B // Bb, NC),
        in_specs=[rows(1024), rows(LANE), per_b(state_shapes[3]), per_b(state_shapes[0]), per_b(state_shapes[1]),
                  per_b(state_shapes[2]), const((ML_CONV, 2 * ML_WIDTH)), const((1, 2 * ML_WIDTH)),
                  const((1, LANE)), const((1, ML_WIDTH))],
        out_specs=[rows(ML_WIDTH)] + [per_b(s) for s in state_shapes],
        out_shape=[jax.ShapeDtypeStruct((B, T, ML_WIDTH), F32)]
                  + [jax.ShapeDtypeStruct((B,) + s, F32) for s in state_shapes],
        scratch_shapes=[pltpu.VMEM((Bb, L + 8, 2 * ML_WIDTH), F32)],
        compiler_params=_cp(("parallel", "arbitrary")),
        name="mlstm",
    )(ml.reshape(B, T, 1024), sm.reshape(B, T, LANE), conv0, C0, n0, m0p, conv_w, conv_b.reshape(1, -1), gate_bias,
      norm_w.reshape(1, -1))
    return h.reshape(N, ML_WIDTH), C, n, m[:, 0, :ML_HEADS], conv


def _rwkv_prep_kernel(rw_ref, shift0_ref, mu_ref, w0_ref, w2_ref, a0_ref, a2_ref, g2_ref, kk_ref, ka_ref, rk_ref,
                      p_ref, gb_ref, shift_ref, carry_ref, *, Bb, tt):
    c = pl.program_id(1)
    W = RW_WIDTH
    C = rw_ref.shape[1]

    @pl.when(c == 0)
    def _():
        carry_ref[...] = shift0_ref[...]

    x = rw_ref[...]
    carry_rows = jnp.broadcast_to(carry_ref[...], (Bb, tt, C)).reshape(Bb * tt, C)
    prev = jnp.where(_iota((Bb * tt, 1), 0) % tt == 0, carry_rows, pltpu.roll(x, 1, 0))
    last = x.reshape(Bb, tt, C)[:, tt - 1:tt, :]
    carry_ref[...] = last
    shift_ref[...] = last
    xm = x + (prev - x) * mu_ref[...]
    r = xm[:, 0:W]
    k = xm[:, W:2 * W]
    v = xm[:, 2 * W:3 * W]
    wl = xm[:, 3 * W:3 * W + RW_LORA_W]
    al = xm[:, 3 * W + RW_LORA_W:3 * W + RW_LORA_W + RW_LORA_A]
    gl = xm[:, 3 * W + RW_LORA_W + RW_LORA_A:]
    wpre = w0_ref[...] + _dot(jnp.tanh(wl), w2_ref[...], HIGHEST)
    softplus_neg = jnp.maximum(-wpre, 0.0) + jnp.log1p(jnp.exp(-jnp.abs(wpre)))
    decay = jnp.exp(-jnp.exp(-softplus_neg - 0.5))
    a = _sigmoid(a0_ref[...] + _dot(al, a2_ref[...], HIGHEST))
    g = _dot(_sigmoid(gl), g2_ref[...], HIGHEST)
    ones = _head_ones(W)
    kk = k * kk_ref[...]
    kk = kk / jnp.maximum(jnp.sqrt(_head_sum(kk * kk, ones)), 1e-12)
    k2 = k * (1.0 + (a - 1.0) * ka_ref[...])
    bonus = _head_sum(r * k2 * rk_ref[...], ones) * v
    p_ref[:, 0:W] = r
    p_ref[:, W:2 * W] = decay
    p_ref[:, 2 * W:3 * W] = k2
    p_ref[:, 3 * W:4 * W] = v
    p_ref[:, 4 * W:5 * W] = kk
    p_ref[:, 5 * W:6 * W] = a * kk
    gb_ref[:, 0:W] = g
    gb_ref[:, W:2 * W] = bonus


def _rwkv_chunk_kernel(p_ref, s0_ref, o_ref, s_ref, *, Bb, L):
    c = pl.program_id(1)
    W, D = RW_WIDTH, HEAD_DIM

    @pl.when(c == 0)
    def _():
        s_ref[...] = s0_ref[...]

    row, col = _iota((L, L), 0), _iota((L, L), 1)
    incl = col <= row
    strict = col < row
    tril_b = jnp.where(incl, 1.0, 0.0).astype(BF16)
    mm = lambda a, b: _dot(a.astype(BF16), b.astype(BF16))
    mm_nt = lambda a, b: _dot_nt(a.astype(BF16), b.astype(BF16))
    mm_tn = lambda a, b: _dot_tn(a.astype(BF16), b.astype(BF16))

    units = []
    for b in range(Bb):
        x = p_ref[b]
        r, w, k, v, kk, bb = (x[:, j * W:(j + 1) * W] for j in range(6))
        logw = jnp.log(w)
        logw_hi = logw.astype(BF16)
        logw_lo = (logw - logw_hi.astype(F32)).astype(BF16)
        cum = _dot(tril_b, logw_hi) + _dot(tril_b, logw_lo)
        g = jnp.exp(cum)
        ginv = jnp.exp(-cum)
        khat = kk * jnp.exp(cum - logw)
        rhat = r * g
        kd = k * ginv
        bd = bb * ginv
        glast = g[L - 1:L, :]
        for h in range(RW_HEADS):
            sl = slice(h * D, (h + 1) * D)
            units.append(dict(b=b, h=h, sl=sl, khat=khat[:, sl].astype(BF16), rhat=rhat[:, sl].astype(BF16),
                              kd=kd[:, sl], bd=bd[:, sl], v=v[:, sl].astype(BF16), gl=glast[:, sl]))

    for u in units:
        S0 = s_ref[u['b'], u['h']]
        kdb, bdb = u['kd'].astype(BF16), u['bd'].astype(BF16)
        if 2 * L == LANE:
            lhs = jnp.concatenate([u['khat'], u['rhat']], axis=0)
            m1 = _dot_nt(lhs, jnp.concatenate([bdb, kdb], axis=0))
            m2 = _dot_nt(lhs, S0.astype(BF16))
            q_raw, p_raw, aub_raw, avk_raw = m1[0:L, 0:L], m1[0:L, L:], m1[L:, 0:L], m1[L:, L:]
            ks0, u['rs0'] = m2[0:L], m2[L:]
        else:
            q_raw, p_raw = _dot_nt(u['khat'], bdb), _dot_nt(u['khat'], kdb)
            aub_raw, avk_raw = _dot_nt(u['rhat'], bdb), _dot_nt(u['rhat'], kdb)
            ks0, u['rs0'] = _dot_nt(u['khat'], S0.astype(BF16)), _dot_nt(u['rhat'], S0.astype(BF16))
        u['Q'] = jnp.where(strict, q_raw, 0.0).astype(BF16)
        u['P'] = jnp.where(strict, p_raw, 0.0).astype(BF16)
        u['avk'] = jnp.where(incl, avk_raw, 0.0).astype(BF16)
        u['aub'] = jnp.where(incl, aub_raw, 0.0).astype(BF16)
        u['X'] = ks0
    for u in units:
        u['X'] = u['X'] + _dot(u['P'], u['v'])
    for u in units:
        u['X'] = u['X'] - mm(u['Q'], u['X'])
    n = 2
    while n < L:
        for u in units:
            u['Q'] = _dot(u['Q'], u['Q']).astype(BF16)
        for u in units:
            u['X'] = u['X'] + mm(u['Q'], u['X'])
        n *= 2
    for u in units:
        xb = u['X'].astype(BF16)
        o_ref[u['b'], :, u['sl']] = u['rs0'] + _dot(u['avk'], u['v']) - _dot(u['aub'], xb)
        gl = u['gl']
        S0 = s_ref[u['b'], u['h']]
        s_ref[u['b'], u['h']] = S0 * gl + _dot_tn(u['v'], (u['kd'] * gl).astype(BF16)) - _dot_tn(xb, (u['bd'] * gl).astype(BF16))


def _rwkv(rw, shift0p, S0, mu_p, w0, w2, a0, a2, g2, k_k, k_a, r_k, B, T):
    N = B * T
    W = RW_WIDTH
    tt = min(T, PREP_ROWS)
    ntt = T // tt
    Bp = math.gcd(B, PREP_ROWS // tt)
    assert ntt == 1 or Bp == 1
    const = lambda shape: pl.BlockSpec(shape, lambda b, c: (0,) * len(shape))
    rows = lambda w: pl.BlockSpec((Bp * tt, w), lambda b, c: (b * ntt + c, 0))
    per_seq = pl.BlockSpec((Bp, 1, 1024), lambda b, c: (b, 0, 0))
    vec = lambda a: a.reshape(1, -1)
    p, gb, shift = pl.pallas_call(
        functools.partial(_rwkv_prep_kernel, Bb=Bp, tt=tt),
        grid=(B // Bp, ntt),
        in_specs=[rows(1024), per_seq, const((1, 1024)),
                  const((1, W)), const((RW_LORA_W, W)), const((1, W)), const((RW_LORA_A, W)), const((RW_LORA_G, W)),
                  const((1, W)), const((1, W)), const((1, W))],
        out_specs=[rows(6 * W), rows(2 * W), per_seq],
        out_shape=[jax.ShapeDtypeStruct((N, 6 * W), F32), jax.ShapeDtypeStruct((N, 2 * W), F32),
                   jax.ShapeDtypeStruct((B, 1, 1024), F32)],
        scratch_shapes=[pltpu.VMEM((Bp, 1, 1024), F32)],
        compiler_params=_cp(("parallel", "arbitrary")),
        name="rwkv_prep",
    )(rw, shift0p.reshape(B, 1, 1024), vec(mu_p), vec(w0), w2, vec(a0), a2, g2, vec(k_k), vec(k_a), vec(r_k))

    Bb = 8 if B % 8 == 0 else B
    L = math.gcd(T, RW_CHUNK)
    blk = lambda w: pl.BlockSpec((Bb, L, w), lambda b, c: (b, c, 0))
    st = pl.BlockSpec((Bb, RW_HEADS, HEAD_DIM, HEAD_DIM), lambda b, c: (b, 0, 0, 0))
    y, S = pl.pallas_call(
        functools.partial(_rwkv_chunk_kernel, Bb=Bb, L=L),
        grid=(B // Bb, T // L),
        in_specs=[blk(6 * W), st],
        out_specs=[blk(W), st],
        out_shape=[jax.ShapeDtypeStruct((B, T, W), F32), jax.ShapeDtypeStruct(S0.shape, F32)],
        compiler_params=_cp(("parallel", "arbitrary")),
        name="rwkv_scan",
    )(p.reshape(B, T, 6 * W), S0)
    return y.reshape(N, W), gb, S, shift[:, 0]


def _compress_slab(xg, j, posl_ref, posh_ref, wl_ref, wh_ref, w2_ref):
    G = xg.shape[0]
    lo = _dot((xg + posl_ref[j]).astype(BF16), wl_ref[j])
    hi = _dot((xg + posh_ref[j]).astype(BF16), wh_ref[j])
    hid = _silu(lo + pltpu.roll(hi, G - 1, 0))
    return _dot(hid.astype(BF16), w2_ref[j])


def _compress_kernel(xk_ref, xv_ref, posl_ref, posh_ref, wl_ref, wh_ref, w2_ref, o_ref, *, G):
    for j, x_ref in enumerate((xk_ref, xv_ref)):
        xg = jnp.concatenate([x_ref.at[0][pl.ds(r, G, stride=GRP_ROWS), :] for r in range(GRP_ROWS)], axis=-1)
        o_ref[0, :, j * KV_WIDTH:(j + 1) * KV_WIDTH] = _compress_slab(xg, j, posl_ref, posh_ref, wl_ref, wh_ref, w2_ref)


def _pack_cmp_weights(cmp_pos, cmp_w1, cmp_w2):
    eye = jnp.eye(NSA_KV_HEADS, dtype=F32)
    w1 = cmp_w1.reshape(2, CMP_LEN, HEAD_DIM, CMP_HID)
    wbig = jnp.einsum('jrde,bg->jrbdge', w1, eye).reshape(2, CMP_LEN, KV_WIDTH, NSA_KV_HEADS * CMP_HID).astype(BF16)
    w_lo = wbig[:, :GRP_ROWS].reshape(2, GRP_ROWS * KV_WIDTH, NSA_KV_HEADS * CMP_HID)
    w_hi = wbig[:, GRP_ROWS:].reshape(2, GRP_ROWS * KV_WIDTH, NSA_KV_HEADS * CMP_HID)
    w2big = jnp.einsum('jed,bg->jbegd', cmp_w2, eye).reshape(2, NSA_KV_HEADS * CMP_HID, KV_WIDTH).astype(BF16)
    pos = jnp.broadcast_to(cmp_pos[:, :, None, :], (2, CMP_LEN, NSA_KV_HEADS, HEAD_DIM))
    pos_lo = pos[:, :GRP_ROWS].reshape(2, 1, -1)
    pos_hi = pos[:, GRP_ROWS:].reshape(2, 1, -1)
    return w_lo, w_hi, w2big, pos_lo, pos_hi


_CMP_K = GRP_ROWS * KV_WIDTH
_CMP_SPECS = [(2, 1, _CMP_K), (2, 1, _CMP_K), (2, _CMP_K, NSA_KV_HEADS * CMP_HID), (2, _CMP_K, NSA_KV_HEADS * CMP_HID),
              (2, NSA_KV_HEADS * CMP_HID, KV_WIDTH)]


def _overlap_map(n_rows, n_cmp, n_sel):
    cs = np.arange(n_rows) * CMP_STRIDE
    ss = np.arange(LANE) * SEL_BLOCK
    ov = np.clip(np.minimum(cs[:, None] + CMP_LEN, ss[None, :] + SEL_BLOCK) - np.maximum(cs[:, None], ss[None, :]),
                 0, None) / CMP_LEN
    ov[n_cmp:, :] = 0.0
    ov[:, n_sel:] = 0.0
    return ov


def _compress(xk, xv, cmp_pos, cmp_w1, cmp_w2, B, G):
    rows = xk.shape[1]
    w_lo, w_hi, w2big, pos_lo, pos_hi = _pack_cmp_weights(cmp_pos, cmp_w1, cmp_w2)
    const = lambda shape: pl.BlockSpec(shape, lambda b: (0,) * len(shape))
    slab = pl.BlockSpec((1, rows, KV_WIDTH), lambda b: (b, 0, 0))
    return pl.pallas_call(
        functools.partial(_compress_kernel, G=G),
        grid=(B,),
        in_specs=[slab, slab] + [const(s) for s in _CMP_SPECS],
        out_specs=pl.BlockSpec((1, G, 2 * KV_WIDTH), lambda b: (b, 0, 0)),
        out_shape=jax.ShapeDtypeStruct((B, G, 2 * KV_WIDTH), F32),
        compiler_params=_cp(("parallel",)),
        name="nsa_compress",
    )(xk, xv, pos_lo, pos_hi, w_lo, w_hi, w2big)


def _softmax_pv(s, bias, v_aug, *, QB, v_keys_minor):
    G, D = NSA_GROUP, HEAD_DIM
    K = s.shape[-1]
    s3 = s.reshape(G, QB, K) + bias[None]
    p = jnp.exp((s3 - jnp.max(s3, axis=-1, keepdims=True)).astype(BF16)).reshape(G * QB, K)
    oa = _dot_nt(p, v_aug) if v_keys_minor else _dot(p, v_aug)
    return oa[:, 0:D] / jnp.maximum(oa[:, D:D + 1], 1e-30)


def _cmp_select(units, ov, QB, NS, n_rank):
    G = NSA_GROUP
    blk = _iota((1, LANE), 1)
    for u in units:
        u['s_c'] = _dot_nt(u['q_raw'], u['ck'].astype(BF16))
    for u in units:
        cmp_end = _iota((1, u['ck'].shape[0]), 1) * CMP_STRIDE + (CMP_LEN - 1)
        u['p_c'] = _masked_softmax(u['s_c'], cmp_end <= u['tq_r'])
    for u in units:
        p_c = u['p_c']
        u['o_c'] = _dot(p_c.astype(BF16), u['cv'].astype(BF16))
        psum = p_c[0:QB]
        for h in range(1, G):
            psum = psum + p_c[h * QB:(h + 1) * QB]
        imp = _dot(psum, ov, HIGHEST)
        tq = u['tq']
        cur = tq // SEL_BLOCK
        avail = blk * SEL_BLOCK <= tq
        forced = (blk == 0) | (blk == cur) | (blk == cur - 1)
        u['score'] = jnp.where(avail, imp + jnp.where(forced, FORCE_BONUS, 0.0), SEL_NEG)
        u['cnt'] = jnp.zeros((QB, LANE), F32)
    for i in range(n_rank):
        for u in units:
            score = u['score']
            col = score[:, i:i + 1]
            u['cnt'] = u['cnt'] + jnp.where((col > score) | ((col >= score) & (blk > i)), 1.0, 0.0)
    for u in units:
        u['sel'] = ((u['cnt'] < min(SEL_TOPN, NS)) & (u['score'] > 0.5 * SEL_NEG)).astype(BF16)


def _head_stack(x, base, g):
    G, D = NSA_GROUP, HEAD_DIM
    return jnp.concatenate([x[:, base + (g * G + h) * D:base + (g * G + h + 1) * D] for h in range(G)], 0)


def _gate_stack(gates, g, j):
    c0 = 2 * ML_HEADS + 3 * g * NSA_GROUP + j
    return jnp.concatenate([gates[:, c0 + 3 * h:c0 + 3 * h + 1] for h in range(NSA_GROUP)], 0)


def _softmax_rows(s_ref, b_ref, p_ref, K, QB, R):
    s3 = s_ref[:, 0:K].reshape(R // QB, QB, K) + b_ref[:, 0:K][None]
    p_ref[:, 0:K] = jnp.exp((s3 - jnp.max(s3, axis=-1, keepdims=True)).astype(BF16)).reshape(R, K)


def _pv_normalised(p, v_aug):
    D = HEAD_DIM
    oa = _dot(p, v_aug)
    return oa[:, 0:D] / jnp.maximum(oa[:, D:D + 1], 1e-30)


def _nsa_kernel(q_ref, sm_ref, ckv_ref, kv_ref, win_ref, ov_ref, ex_ref, o_ref, skb, svb, wkb, wvb,
                s_scr, p_scr, bs_scr, sw_scr, pw_scr, bw_scr, *, QB, NB, NS, P, WN, nvar):
    nb = pl.program_id(1)
    D = HEAD_DIM
    G = NSA_GROUP
    R = G * QB
    LW = wkb.shape[1]

    @pl.when(nb == 0)
    def _():
        for g in range(NSA_KV_HEADS):
            skb[g] = kv_ref[0, :, g * D:(g + 1) * D].astype(BF16)
            svb[g, :, 0:D] = kv_ref[0, :, KV_WIDTH + g * D:KV_WIDTH + (g + 1) * D].astype(BF16)
            svb[g, :, D:2 * D] = jnp.ones((P, D), BF16)
            wkb[g] = win_ref[0, :, g * D:(g + 1) * D].astype(BF16)
            wvb[g, :, 0:D] = win_ref[0, :, KV_WIDTH + g * D:KV_WIDTH + (g + 1) * D].astype(BF16)
            wvb[g, :, D:2 * D] = jnp.ones((LW, D), BF16)

    def body(PS):
        qs = nb * QB
        tq = qs + _iota((QB, 1), 0)
        tq_r = qs + (_iota((R, 1), 0) % QB)
        gates = _sigmoid(sm_ref[...])
        kpos = _iota((1, PS), 1)
        wstart = pl.multiple_of(jnp.clip(qs - WINDOW, 0, LW - WN), 16)
        dlt = tq - (wstart + _iota((1, WN), 1))
        bw_scr[...] = jnp.where((dlt >= 0) & (dlt < WINDOW), 0.0, NEG)
        qv = q_ref[...] * (D ** -0.5)

        units = [dict(g=g, tq=tq, tq_r=tq_r, q_raw=_head_stack(qv, 0, g).astype(BF16),
                      q_rot=_head_stack(qv, NSA_WIDTH, g).astype(BF16), ck=ckv_ref[0, :, g * D:(g + 1) * D],
                      cv=ckv_ref[0, :, KV_WIDTH + g * D:KV_WIDTH + (g + 1) * D]) for g in range(NSA_KV_HEADS)]
        for u in units:
            g = u['g']
            s_scr[g, :, 0:PS] = _dot_nt(u['q_rot'], skb[g, 0:PS, :])
            sw_scr[g] = _dot_nt(u['q_rot'], wkb[g, pl.ds(wstart, WN), :])
        for u in units:
            g = u['g']
            _softmax_rows(sw_scr.at[g], bw_scr, pw_scr.at[g], WN, QB, R)
            u['o_w'] = _pv_normalised(pw_scr[g], wvb[g, pl.ds(wstart, WN), :])
        _cmp_select(units, ov_ref[...], QB, NS, min(NS, PS // SEL_BLOCK))
        for u in units:
            selk = _dot(u['sel'], ex_ref[:, 0:PS])
            bs_scr[u['g'], :, 0:PS] = jnp.where((selk > 0.5) & (kpos <= tq), 0.0, NEG)
        outs = []
        for u in units:
            g = u['g']
            _softmax_rows(s_scr.at[g], bs_scr.at[g], p_scr.at[g], PS, QB, R)
            o_s = _pv_normalised(p_scr[g, :, 0:PS], svb[g, 0:PS, :])
            o = _gate_stack(gates, g, 0) * u['o_c'] + _gate_stack(gates, g, 1) * o_s + _gate_stack(gates, g, 2) * u['o_w']
            outs.extend(o[h * QB:(h + 1) * QB] for h in range(G))
        o_ref[...] = jnp.concatenate(outs, axis=-1)

    for v in range(nvar):
        pl.when(nb // (NB // nvar) == v)(functools.partial(body, P * (v + 1) // nvar))


def _nsa_attend(qq, sm, ckv, kv_rows, win_rows, B, T):
    N = B * T
    QB = math.gcd(T, Q_BLOCK)
    NB = T // QB
    P = LW = T
    NS = -(-T // SEL_BLOCK)
    NC = (T - CMP_LEN) // CMP_STRIDE + 1
    NCP = ckv.shape[1]
    WN = -(-(WINDOW + QB) // LANE) * LANE
    assert NS <= LANE and NC <= NCP and P % LANE == 0 and QB % 32 == 0 and LW >= WN and (LW - WN) % 16 == 0
    ov = _overlap_map(NCP, NC, NS)
    expand = (np.arange(P)[None, :] // SEL_BLOCK == np.arange(LANE)[:, None]).astype(np.float32)
    const = lambda shape: pl.BlockSpec(shape, lambda b, nb: (0,) * len(shape))
    rows = lambda w: pl.BlockSpec((QB, w), lambda b, nb: (b * NB + nb, 0))
    nvar = next((v for v in (8, 4, 2) if NB % v == 0 and (P // v) % (2 * LANE) == 0), 1)
    KVH, R = NSA_KV_HEADS, NSA_GROUP * QB
    return pl.pallas_call(
        functools.partial(_nsa_kernel, QB=QB, NB=NB, NS=NS, P=P, WN=WN, nvar=nvar),
        grid=(B, NB),
        in_specs=[rows(2 * NSA_WIDTH), rows(LANE),
                  pl.BlockSpec((1, NCP, 2 * KV_WIDTH), lambda b, nb: (b, 0, 0)),
                  pl.BlockSpec((1, P, 2 * KV_WIDTH), lambda b, nb: (b, 0, 1)),
                  pl.BlockSpec((1, LW, 2 * KV_WIDTH), lambda b, nb: (b, 0, 0)),
                  const((NCP, LANE)), const((LANE, P))],
        out_specs=rows(NSA_WIDTH),
        out_shape=jax.ShapeDtypeStruct((N, NSA_WIDTH), F32),
        scratch_shapes=[pltpu.VMEM((KVH, P, HEAD_DIM), BF16), pltpu.VMEM((KVH, P, 2 * HEAD_DIM), BF16),
                        pltpu.VMEM((KVH, LW, HEAD_DIM), BF16), pltpu.VMEM((KVH, LW, 2 * HEAD_DIM), BF16),
                        pltpu.VMEM((KVH, R, P), F32), pltpu.VMEM((KVH, R, P), BF16), pltpu.VMEM((KVH, QB, P), F32),
                        pltpu.VMEM((KVH, R, WN), F32), pltpu.VMEM((KVH, R, WN), BF16), pltpu.VMEM((QB, WN), F32)],
        compiler_params=_cp(("parallel", "arbitrary")),
        name="nsa_attend",
    )(qq, sm, ckv, kv_rows, win_rows, jnp.asarray(ov, F32), jnp.asarray(expand, BF16))


def _nsa_paged_kernel(pt_ref, *refs, SEQ, n_pages, page, T, NS, offset, lwin):
    del pt_ref
    page_refs = refs[:SEQ * n_pages]
    (q_ref, sm_ref, new_ref, wcache_ref, wnew_ref, posl_ref, posh_ref, wl_ref, wh_ref, w2_ref, ov_ref, ex_ref,
     o_ref, wout_ref, x_ref) = refs[SEQ * n_pages:]
    D, G = HEAD_DIM, NSA_GROUP
    QB = T
    R = G * QB
    P = (n_pages + 1) * page
    NG = n_pages * page // GRP_ROWS
    pages = lambda s: page_refs[s * n_pages:(s + 1) * n_pages]

    def new_tile(x, c0):
        return x[:, c0:c0 + D].T.astype(BF16)

    for s in range(SEQ):
        for i in range(n_pages):
            for j in range(2):
                for g in range(NSA_KV_HEADS):
                    x_ref[s, j, i * page:(i + 1) * page, g * D:(g + 1) * D] = pages(s)[i][0, j, g].T
    ckv = {}
    for s in range(SEQ):
        for j in range(2):
            xg = jnp.concatenate([x_ref.at[s, j][pl.ds(r, NG, stride=GRP_ROWS), :] for r in range(GRP_ROWS)], axis=-1)
            ckv[s, j] = _compress_slab(xg, j, posl_ref, posh_ref, wl_ref, wh_ref, w2_ref)

    tq = offset + _iota((QB, 1), 0)
    tq_r = offset + (_iota((R, 1), 0) % QB)
    kpos = _iota((1, P), 1)
    dlt = tq - (offset - lwin + _iota((1, lwin + page), 1))
    bias_w = jnp.where((dlt >= 0) & (dlt < WINDOW), 0.0, NEG)
    zrows = lambda w: jnp.zeros((page - T, w), F32)
    softmax_pv = functools.partial(_softmax_pv, QB=QB, v_keys_minor=True)

    units = []
    for s in range(SEQ):
        rows = slice(s * T, (s + 1) * T)
        qv = q_ref[rows, :] * (D ** -0.5)
        gates = _sigmoid(sm_ref[rows, :])
        new_t = jnp.concatenate([new_ref[rows, :], zrows(KV_ROW)], axis=0)
        wnew_t = jnp.concatenate([wnew_ref[rows, :], zrows(2 * KV_WIDTH)], axis=0)
        for g in range(NSA_KV_HEADS):
            kt = jnp.concatenate([pg[0, 2, g].astype(BF16) for pg in pages(s)]
                                 + [new_tile(new_t, 2 * KV_WIDTH + g * D)], axis=1)
            vt = jnp.concatenate([pg[0, 3, g].astype(BF16) for pg in pages(s)]
                                 + [new_tile(new_t, 3 * KV_WIDTH + g * D)], axis=1)
            wkt = jnp.concatenate([wcache_ref[s, 0, g].astype(BF16), new_tile(wnew_t, g * D)], axis=1)
            wvt = jnp.concatenate([wcache_ref[s, 1, g].astype(BF16), new_tile(wnew_t, KV_WIDTH + g * D)], axis=1)
            units.append(dict(s=s, g=g, tq=tq, tq_r=tq_r, gates=gates, q_raw=_head_stack(qv, 0, g).astype(BF16),
                              q_rot=_head_stack(qv, NSA_WIDTH, g).astype(BF16), kt=kt, vt=vt, wkt=wkt, wvt=wvt,
                              ck=ckv[s, 0][:, g * D:(g + 1) * D], cv=ckv[s, 1][:, g * D:(g + 1) * D]))
    for u in units:
        u['s_s'] = _dot(u['q_rot'], u['kt'])
        u['s_w'] = _dot(u['q_rot'], u['wkt'])
    for u in units:
        u['o_w'] = softmax_pv(u['s_w'], bias_w, jnp.concatenate([u['wvt'], jnp.ones_like(u['wvt'])], axis=0))
    _cmp_select(units, ov_ref[...], QB, NS, NS)
    for u in units:
        bias_s = jnp.where((_dot(u['sel'], ex_ref[...]) > 0.5) & (kpos <= tq), 0.0, NEG)
        o_s = softmax_pv(u['s_s'], bias_s, jnp.concatenate([u['vt'], jnp.ones_like(u['vt'])], axis=0))
        gates, g = u['gates'], u['g']
        u['o'] = _gate_stack(gates, g, 0) * u['o_c'] + _gate_stack(gates, g, 1) * o_s + _gate_stack(gates, g, 2) * u['o_w']
    for s in range(SEQ):
        outs = [u['o'][h * QB:(h + 1) * QB] for u in units if u['s'] == s for h in range(G)]
        o_ref[s * T:(s + 1) * T, :] = jnp.concatenate(outs, axis=-1)
    for s in range(SEQ):
        wnew_t = jnp.concatenate([wnew_ref[s * T:(s + 1) * T, :], zrows(2 * KV_WIDTH)], axis=0)
        for kv in range(2):
            for g in range(NSA_KV_HEADS):
                c0 = kv * KV_WIDTH + g * D
                wout_ref[s, kv, g] = jnp.concatenate([wcache_ref[s, kv, g][:, T:], wnew_t[:, c0:c0 + D].T[:, 0:T]], axis=1)


def _nsa_paged(qq, sm, kv_new, win_new, cache_t, page_ids, wcache_t, win_base, cmp_pos, cmp_w1, cmp_w2, B, T, offset):
    N = B * T
    n_pages = page_ids.shape[1]
    page = cache_t.shape[-1]
    lwin = wcache_t.shape[-1]
    p_len = n_pages * page + T
    P = (n_pages + 1) * page
    NS = -(-p_len // SEL_BLOCK)
    n_cmp = (p_len - CMP_LEN) // CMP_STRIDE + 1
    NG = n_pages * page // GRP_ROWS
    SEQ = 2 if (B % 2 == 0 and win_base % 2 == 0) else 1
    assert T % 8 == 0 and T <= page and NS <= LANE and n_cmp + 1 == NG and lwin % LANE == 0 and NG % 8 == 0
    w_lo, w_hi, w2big, pos_lo, pos_hi = _pack_cmp_weights(cmp_pos, cmp_w1, cmp_w2)
    ov = _overlap_map(NG, n_cmp, NS)
    expand = (np.arange(P)[None, :] // SEL_BLOCK == np.arange(LANE)[:, None]).astype(np.float32)

    def page_map(k, b, pt):
        return (pt[b * SEQ * n_pages + k], 0, 0, 0, 0)

    const = lambda shape: pl.BlockSpec(shape, lambda b, pt: (0,) * len(shape))
    rows = lambda w: pl.BlockSpec((SEQ * T, w), lambda b, pt: (b, 0))
    in_specs = [pl.BlockSpec((1, 4, NSA_KV_HEADS, HEAD_DIM, page), functools.partial(page_map, k))
                for k in range(SEQ * n_pages)]
    wshape = (SEQ, 2, NSA_KV_HEADS, HEAD_DIM, lwin)
    in_specs += [rows(2 * NSA_WIDTH), rows(LANE), rows(KV_ROW),
                 pl.BlockSpec(wshape, lambda b, pt: (win_base // SEQ + b, 0, 0, 0, 0)),
                 rows(2 * KV_WIDTH)] + [const(s) for s in _CMP_SPECS] + [const((NG, LANE)), const((LANE, P))]
    return pl.pallas_call(
        functools.partial(_nsa_paged_kernel, SEQ=SEQ, n_pages=n_pages, page=page, T=T, NS=NS, offset=offset, lwin=lwin),
        grid_spec=pltpu.PrefetchScalarGridSpec(
            num_scalar_prefetch=1, grid=(B // SEQ,), in_specs=in_specs,
            out_specs=[rows(NSA_WIDTH), pl.BlockSpec(wshape, lambda b, pt: (b, 0, 0, 0, 0))],
            scratch_shapes=[pltpu.VMEM((SEQ, 2, n_pages * page, KV_WIDTH), F32)]),
        out_shape=[jax.ShapeDtypeStruct((N, NSA_WIDTH), F32), jax.ShapeDtypeStruct((B,) + wshape[1:], F32)],
        compiler_params=_cp(("parallel",)),
        name="nsa_paged",
    )(page_ids.reshape(-1), *([cache_t] * (SEQ * n_pages)), qq, sm, kv_new, wcache_t, win_new, pos_lo, pos_hi, w_lo,
      w_hi, w2big, jnp.asarray(ov, F32), jnp.asarray(expand, BF16))


def _outproj_kernel(x_ref, hml_ref, yrw_ref, gb_ref, hns_ref, lnw_ref, lnb_ref, w_ref, o_ref):
    W = RW_WIDTH
    y = yrw_ref[...]
    ones = _head_ones(W)
    mu = _head_sum(y, ones) * (1.0 / HEAD_DIM)
    var = _head_sum(jnp.square(y - mu), ones) * (1.0 / HEAD_DIM)
    yn = (y - mu) * lax.rsqrt(var + RW_GN_EPS) * lnw_ref[...]
    h_rw = (yn + lnb_ref[...] + gb_ref[:, W:2 * W]) * gb_ref[:, 0:W]
    acc = _dot(hml_ref[...].astype(BF16), w_ref[0:ML_WIDTH, :])
    acc = acc + _dot(h_rw.astype(BF16), w_ref[ML_WIDTH:ML_WIDTH + W, :])
    acc = acc + _dot(hns_ref[...].astype(BF16), w_ref[ML_WIDTH + W:, :])
    o_ref[...] = x_ref[...] + acc


def _outproj(x2, h_ml, y_rw, gb, h_ns, ln_w, ln_b, w_out):
    N = x2.shape[0]
    tm = min(512, N)
    row = lambda w: pl.BlockSpec((tm, w), lambda i: (i, 0))
    const = lambda shape: pl.BlockSpec(shape, lambda i: (0,) * len(shape))
    return pl.pallas_call(
        _outproj_kernel,
        grid=(N // tm,),
        in_specs=[row(D_MODEL), row(ML_WIDTH), row(RW_WIDTH), row(2 * RW_WIDTH), row(NSA_WIDTH),
                  const((1, RW_WIDTH)), const((1, RW_WIDTH)), const((MIX_WIDTH, D_MODEL))],
        out_specs=row(D_MODEL),
        out_shape=jax.ShapeDtypeStruct((N, D_MODEL), F32),
        compiler_params=_cp(("parallel",)),
        name="outproj",
    )(x2, h_ml, y_rw, gb, h_ns, ln_w.reshape(1, -1), ln_b.reshape(1, -1), w_out.astype(BF16))


def _rms(x, w):
    return x * lax.rsqrt(jnp.mean(x * x, axis=-1, keepdims=True) + RMS_EPS) * w


def _ffn_kernel(x_ref, nw_ref, w1_ref, w3_ref, w2_ref, fw_ref, o_ref, xn_ref, *, final_norm):
    j = pl.program_id(1)

    @pl.when(j == 0)
    def _():
        x = x_ref[...]
        xn_ref[...] = _rms(x, nw_ref[...]).astype(BF16)
        o_ref[...] = x

    xn = xn_ref[...]
    hid = _silu(_dot(xn, w1_ref[...])) * _dot(xn, w3_ref[...])
    o_ref[...] += _dot(hid.astype(BF16), w2_ref[...])

    if final_norm:
        @pl.when(j == pl.num_programs(1) - 1)
        def _():
            o_ref[...] = _rms(o_ref[...], fw_ref[...])


def _ffn(x2, norm_w, w1, w3, w2, final_w):
    N = x2.shape[0]
    d_ff = w1.shape[1]
    tm = min(512, N)
    nj = 2 if (d_ff // 2) % LANE == 0 else 1
    tf = d_ff // nj
    final_norm = final_w is not None
    fw = (final_w if final_norm else jnp.ones((D_MODEL,), F32)).reshape(1, D_MODEL)
    return pl.pallas_call(
        functools.partial(_ffn_kernel, final_norm=final_norm),
        grid=(N // tm, nj),
        in_specs=[pl.BlockSpec((tm, D_MODEL), lambda i, j: (i, 0)), pl.BlockSpec((1, D_MODEL), lambda i, j: (0, 0)),
                  pl.BlockSpec((D_MODEL, tf), lambda i, j: (0, j)), pl.BlockSpec((D_MODEL, tf), lambda i, j: (0, j)),
                  pl.BlockSpec((tf, D_MODEL), lambda i, j: (j, 0)), pl.BlockSpec((1, D_MODEL), lambda i, j: (0, 0))],
        out_specs=pl.BlockSpec((tm, D_MODEL), lambda i, j: (i, 0)),
        out_shape=jax.ShapeDtypeStruct((N, D_MODEL), F32),
        scratch_shapes=[pltpu.VMEM((tm, D_MODEL), BF16)],
        compiler_params=_cp(("parallel", "arbitrary")),
        name="ffn",
    )(x2, norm_w.reshape(1, D_MODEL), w1.astype(BF16), w3.astype(BF16), w2.astype(BF16), fw)


def _moe_kernel(x_ref, nw_ref, rt_ref, w1_ref, w3_ref, w2_ref, fw_ref, o_ref, xn_ref, comb_ref, *, final_norm):
    e = pl.program_id(1)

    @pl.when(e == 0)
    def _():
        x = x_ref[...]
        xn = _rms(x, nw_ref[...])
        xn_ref[...] = xn.astype(BF16)
        o_ref[...] = x
        lane = _iota((1, LANE), 1).astype(F32)
        logits = jnp.where(lane < N_EXPERTS, _dot(xn, rt_ref[...], HIGHEST), -jnp.inf)
        m1 = jnp.max(logits, axis=-1, keepdims=True)
        i1 = jnp.min(jnp.where(logits == m1, lane, float(LANE)), axis=-1, keepdims=True)
        rest = jnp.where(lane == i1, -jnp.inf, logits)
        m2 = jnp.max(rest, axis=-1, keepdims=True)
        i2 = jnp.min(jnp.where(rest == m2, lane, float(LANE)), axis=-1, keepdims=True)
        e2 = jnp.exp(m2 - m1)
        den = 1.0 + e2
        comb_ref[...] = jnp.where(lane == i1, 1.0 / den, 0.0) + jnp.where(lane == i2, e2 / den, 0.0)

    xn = xn_ref[...]
    comb = comb_ref[...]
    ce = jnp.sum(jnp.where(_iota((1, LANE), 1) == e, comb, 0.0), axis=-1, keepdims=True)
    hid = _silu(_dot(xn, w1_ref[0])) * _dot(xn, w3_ref[0])
    o_ref[...] += ce * _dot(hid.astype(BF16), w2_ref[0])

    if final_norm:
        @pl.when(e == pl.num_programs(1) - 1)
        def _():
            o_ref[...] = _rms(o_ref[...], fw_ref[...])


def _moe(x2, norm_w, router, w1, w3, w2, final_w):
    N = x2.shape[0]
    E, _, d_exp = w1.shape
    assert E == N_EXPERTS and TOP_K == 2
    tm = min(512, N)
    final_norm = final_w is not None
    fw = (final_w if final_norm else jnp.ones((D_MODEL,), F32)).reshape(1, D_MODEL)
    rt = jnp.pad(router, ((0, 0), (0, LANE - E)))
    return pl.pallas_call(
        functools.partial(_moe_kernel, final_norm=final_norm),
        grid=(N // tm, E),
        in_specs=[pl.BlockSpec((tm, D_MODEL), lambda i, e: (i, 0)), pl.BlockSpec((1, D_MODEL), lambda i, e: (0, 0)),
                  pl.BlockSpec((D_MODEL, LANE), lambda i, e: (0, 0)),
                  pl.BlockSpec((1, D_MODEL, d_exp), lambda i, e: (e, 0, 0)),
                  pl.BlockSpec((1, D_MODEL, d_exp), lambda i, e: (e, 0, 0)),
                  pl.BlockSpec((1, d_exp, D_MODEL), lambda i, e: (e, 0, 0)),
                  pl.BlockSpec((1, D_MODEL), lambda i, e: (0, 0))],
        out_specs=pl.BlockSpec((tm, D_MODEL), lambda i, e: (i, 0)),
        out_shape=jax.ShapeDtypeStruct((N, D_MODEL), F32),
        scratch_shapes=[pltpu.VMEM((tm, D_MODEL), BF16), pltpu.VMEM((tm, LANE), F32)],
        compiler_params=_cp(("parallel", "arbitrary")),
        name="moe",
    )(x2, norm_w.reshape(1, D_MODEL), rt, w1.astype(BF16), w3.astype(BF16), w2.astype(BF16), fw)


def _layer(l, x2, B, T, offset, st, P, final_w):
    C0, n0, m0, conv0, S0, shift0, cache, page_ids, win0 = st
    rw_perm = jnp.asarray(_rw_order())
    pos = offset + jnp.arange(T)
    cos2, sin2 = _rope_tables(pos)
    ml, rw, qq, kv4, wrow, sm, cks, cvs = _inproj(x2, P['norm_mix_w'][l], _pack_w_in(P['w_in'][l]), cos2, sin2, T)

    h_ml, C, n, m, conv = _mlstm(ml, sm, conv0, C0, n0, m0, P['ml_conv_w'][l], P['ml_conv_b'][l],
                                 P['ml_b_i'][l], P['ml_b_f'][l], P['ml_norm_w'][l], B, T)

    y_rw, gb, S, shift_p = _rwkv(rw, shift0[:, rw_perm], S0, P['rw_mu'][l][rw_perm], P['rw_w0'][l], P['rw_w2'][l],
                                 P['rw_a0'][l], P['rw_a2'][l], P['rw_g2'][l], P['rw_k_k'][l], P['rw_k_a'][l],
                                 P['rw_r_k'][l].reshape(-1), B, T)
    shift_new = jnp.zeros_like(shift_p).at[:, rw_perm].set(shift_p)

    kv3 = kv4.reshape(B, T, KV_ROW)
    w3 = wrow.reshape(B, T, 2 * KV_WIDTH)
    cmp_w = (P['nsa_cmp_pos'][l], P['nsa_cmp_w1'][l], P['nsa_cmp_w2'][l])
    wbuf = P['wbuf']
    if cache is None:
        n_cmp = (T - CMP_LEN) // CMP_STRIDE + 1
        G = -(-(n_cmp + 1) // 8) * 8
        assert G * GRP_ROWS <= T
        ckv = _compress(cks.reshape(B, T, KV_WIDTH), cvs.reshape(B, T, KV_WIDTH), *cmp_w, B, G)
        assert offset == 0
        h_ns = _nsa_attend(qq, sm, ckv, kv3, w3, B, T)
        win_new = w3[:, T - wbuf:] if T >= wbuf else jnp.pad(w3, ((0, 0), (wbuf - T, 0), (0, 0)))
        win_new = win_new.reshape(B, wbuf, 2, NSA_KV_HEADS, HEAD_DIM)
    else:
        assert win0['key_minor'].shape[-1] == wbuf and T <= wbuf
        h_ns, wout = _nsa_paged(qq, sm, kv4, wrow, cache, page_ids, win0['key_minor'], win0['base'], *cmp_w, B, T, offset)
        win_new = jnp.transpose(wout, (0, 4, 1, 2, 3))

    x2 = _outproj(x2, h_ml, y_rw, gb, h_ns, P['rw_ln_w'][l], P['rw_ln_b'][l], P['w_out'][l])
    if l % 2 == 0:
        x2 = _ffn(x2, P['norm_ffn_w'][l], P['ffn_w1'][l // 2], P['ffn_w3'][l // 2], P['ffn_w2'][l // 2], final_w)
    else:
        x2 = _moe(x2, P['norm_ffn_w'][l], P['moe_router'][l // 2], P['moe_w1'][l // 2], P['moe_w3'][l // 2],
                  P['moe_w2'][l // 2], final_w)
    new = (kv3.reshape(B, T, 4, NSA_KV_HEADS, HEAD_DIM), win_new, C, n, m, conv, S, shift_new)
    return x2, new


def _run(x, offset, get_state, P, depth):
    B, T, _ = x.shape
    x2 = x.reshape(B * T, D_MODEL)
    news = []
    for l in range(depth):
        x2, nw = _layer(l, x2, B, T, offset, get_state(l), P, P['norm_final_w'] if l == depth - 1 else None)
        news.append(nw)
    return x2.reshape(B, T, D_MODEL), [jnp.stack([nw[i] for nw in news]) for i in range(8)]


def kernel(x_prompt, x_sample, cache_nsa_kv, cache_win_kv, state_mlstm_C, state_mlstm_n, state_mlstm_m, state_mlstm_conv, state_rwkv_S, state_rwkv_shift, page_table, norm_mix_w, w_in, ml_conv_w, ml_conv_b, ml_b_i, ml_b_f, ml_norm_w, rw_mu, rw_w0, rw_w2, rw_a0, rw_a2, rw_g2, rw_k_k, rw_k_a, rw_r_k, rw_ln_w, rw_ln_b, nsa_cmp_pos, nsa_cmp_w1, nsa_cmp_w2, w_out, norm_ffn_w, ffn_w1, ffn_w3, ffn_w2, moe_router, moe_w1, moe_w3, moe_w2, norm_final_w):
    depth = w_in.shape[0]
    Bp = x_prompt.shape[0]
    Bs = x_sample.shape[0]
    n_pool, page = cache_nsa_kv.shape[1], cache_nsa_kv.shape[2]
    past_len = page_table.shape[1] * page
    wbuf = min(WINDOW, past_len)
    P = dict(norm_mix_w=norm_mix_w, w_in=w_in, ml_conv_w=ml_conv_w, ml_conv_b=ml_conv_b, ml_b_i=ml_b_i, ml_b_f=ml_b_f,
             ml_norm_w=ml_norm_w, rw_mu=rw_mu, rw_w0=rw_w0, rw_w2=rw_w2, rw_a0=rw_a0, rw_a2=rw_a2, rw_g2=rw_g2,
             rw_k_k=rw_k_k, rw_k_a=rw_k_a, rw_r_k=rw_r_k, rw_ln_w=rw_ln_w, rw_ln_b=rw_ln_b, nsa_cmp_pos=nsa_cmp_pos,
             nsa_cmp_w1=nsa_cmp_w1, nsa_cmp_w2=nsa_cmp_w2, w_out=w_out, norm_ffn_w=norm_ffn_w, ffn_w1=ffn_w1,
             ffn_w3=ffn_w3, ffn_w2=ffn_w2, moe_router=moe_router, moe_w1=moe_w1, moe_w3=moe_w3, moe_w2=moe_w2,
             norm_final_w=norm_final_w, wbuf=wbuf)

    def prompt_state(l):
        z = lambda *s: jnp.zeros((Bp,) + s, F32)
        return (z(ML_HEADS, HEAD_DIM, HEAD_DIM), z(ML_HEADS, HEAD_DIM), z(ML_HEADS), z(ML_CONV - 1, 2 * ML_WIDTH),
                z(RW_HEADS, HEAD_DIM, HEAD_DIM), z(RW_COLS), None, None, None)

    cache_t = jnp.transpose(cache_nsa_kv, (0, 1, 3, 4, 5, 2)).reshape(depth * n_pool, 4, NSA_KV_HEADS, HEAD_DIM, page)
    wcache_t = jnp.transpose(cache_win_kv, (0, 1, 3, 4, 5, 2)).reshape(depth * Bs, 2, NSA_KV_HEADS, HEAD_DIM, -1)

    def sample_state(l):
        win = dict(key_minor=wcache_t, base=l * Bs)
        return (state_mlstm_C[l], state_mlstm_n[l], state_mlstm_m[l], state_mlstm_conv[l], state_rwkv_S[l],
                state_rwkv_shift[l], cache_t, page_table + l * n_pool, win)

    y_prompt, pst = _run(x_prompt, 0, prompt_state, P, depth)
    y_sample, sst = _run(x_sample, past_len, sample_state, P, depth)
    p_kv, p_win, p_C, p_n, p_m, p_conv, p_S, p_shift = pst
    s_kv, s_win, s_C, s_n, s_m, s_conv, s_S, s_shift = sst
    return (y_prompt, y_sample, p_kv, s_kv, p_win, s_win, p_C, s_C, p_n, s_n, p_m, s_m,
            p_conv, s_conv, p_S, s_S, p_shift, s_shift)
```

```python
import functools
import math

import numpy as np
import jax
import jax.numpy as jnp
from jax import lax
from jax.experimental import pallas as pl
from jax.experimental.pallas import tpu as pltpu

F32 = jnp.float32
BF16 = jnp.bfloat16
HIGHEST = lax.Precision.HIGHEST

D_MODEL = 1024
HEAD_DIM = 64
ML_HEADS = 4
RW_HEADS = 4
NSA_HEADS = 8
NSA_KV_HEADS = 2
NSA_GROUP = NSA_HEADS // NSA_KV_HEADS
ML_WIDTH = ML_HEADS * HEAD_DIM
RW_WIDTH = RW_HEADS * HEAD_DIM
NSA_WIDTH = NSA_HEADS * HEAD_DIM
KV_WIDTH = NSA_KV_HEADS * HEAD_DIM
MIX_WIDTH = ML_WIDTH + RW_WIDTH + NSA_WIDTH
ML_CHUNK = 64
ML_CONV = 4
ML_NORM_EPS = 1e-6
RW_LORA_W = 64
RW_LORA_A = 64
RW_LORA_G = 128
RW_GN_EPS = 64e-5
RW_CHUNK = 64
CMP_LEN = 32
CMP_STRIDE = 16
CMP_HID = 128
SEL_BLOCK = 64
SEL_TOPN = 8
WINDOW = 512
Q_BLOCK = 64
ROPE_THETA = 10000.0
N_EXPERTS = 8
TOP_K = 2
RMS_EPS = 1e-6
NEG = -1e30
SEL_NEG = -1e9
FORCE_BONUS = 1e3
ML_COLS = 4 * ML_WIDTH + 2 * ML_HEADS
RW_COLS = 3 * RW_WIDTH + RW_LORA_W + RW_LORA_A + RW_LORA_G
NSA_COLS = NSA_WIDTH + 6 * KV_WIDTH + 3 * NSA_HEADS

LANE = 128
KV_ROW = 4 * KV_WIDTH
GRP_ROWS = CMP_STRIDE
VMEM_LIMIT = 52 * 1024 * 1024

C_ML = 0
C_RW = 1024
C_Q = 2048
C_KV = 2560
C_WK = 3072
C_WV = 3200
C_SM = 3328
C_TOT = 3456


def _cp(sem, vmem=VMEM_LIMIT):
    return pltpu.CompilerParams(dimension_semantics=sem, vmem_limit_bytes=vmem)


def _dot(a, b, precision=None):
    return jnp.dot(a, b, preferred_element_type=F32, precision=precision)


def _dot_nt(a, b, precision=None):
    return lax.dot_general(a, b, (((1,), (1,)), ((), ())), preferred_element_type=F32, precision=precision)


def _dot_tn(a, b, precision=None):
    return lax.dot_general(a, b, (((0,), (0,)), ((), ())), preferred_element_type=F32, precision=precision)


def _sigmoid(x):
    return 1.0 / (1.0 + jnp.exp(-x))


def _silu(x):
    return x * _sigmoid(x)


def _iota(shape, dim):
    return lax.broadcasted_iota(jnp.int32, shape, dim)


def _head_ones(w):
    return jnp.where(_iota((w, w), 0) // HEAD_DIM == _iota((w, w), 1) // HEAD_DIM, 1.0, 0.0).astype(BF16)


def _head_sum(x, ones):
    hi = x.astype(BF16)
    lo = (x - hi.astype(F32)).astype(BF16)
    return _dot(hi, ones) + _dot(lo, ones)


def _masked_softmax(s, mask):
    s = jnp.where(mask, s, NEG)
    m = jnp.max(s, axis=-1, keepdims=True)
    p = jnp.where(mask, jnp.exp(s - m), 0.0)
    return p / jnp.maximum(jnp.sum(p, -1, keepdims=True), 1e-30)


def _rw_order():
    r0, wl0 = 0, RW_WIDTH
    k0 = wl0 + RW_LORA_W
    v0 = k0 + RW_WIDTH
    al0 = v0 + RW_WIDTH
    gl0 = al0 + RW_LORA_A
    return np.concatenate([np.arange(r0, r0 + RW_WIDTH), np.arange(k0, k0 + RW_WIDTH), np.arange(v0, v0 + RW_WIDTH),
                           np.arange(wl0, wl0 + RW_LORA_W), np.arange(al0, al0 + RW_LORA_A),
                           np.arange(gl0, gl0 + RW_LORA_G)])


def _runs(order):
    cuts = np.flatnonzero(np.diff(order) != 1) + 1
    return [(int(seg[0]), int(seg[-1]) + 1) for seg in np.split(np.asarray(order), cuts)]


def _take_cols(x, order):
    return jnp.concatenate([x[..., lo:hi] for lo, hi in _runs(order)], axis=-1)


def _pack_w_in(w):
    ml0, rw0, ns0 = 0, ML_COLS, ML_COLS + RW_COLS
    gt0 = ns0 + NSA_WIDTH + 6 * KV_WIDTH
    rw_runs = [(rw0 + lo, rw0 + hi) for lo, hi in _runs(_rw_order())]
    runs = [(ml0, ml0 + 4 * ML_WIDTH)] + rw_runs + [(ns0, gt0), (ml0 + 4 * ML_WIDTH, ml0 + ML_COLS), (gt0, gt0 + 3 * NSA_HEADS)]
    width = sum(hi - lo for lo, hi in runs)
    assert C_SM <= width <= C_TOT
    pieces = [w[:, lo:hi] for lo, hi in runs] + [jnp.zeros((w.shape[0], C_TOT - width), w.dtype)]
    return jnp.concatenate(pieces, axis=1).astype(BF16)


def _swap_rotary_halves(x):
    half = HEAD_DIM // 2
    w = x.shape[-1]
    first = _iota((1, w), 1) % HEAD_DIM < half
    return jnp.where(first, pltpu.roll(x, w - half, 1), pltpu.roll(x, half, 1))


def _rope_tables(pos):
    half = HEAD_DIM // 2
    inv = ROPE_THETA ** (-jnp.arange(half, dtype=F32) / half)
    ang = pos.astype(F32)[:, None] * inv[None, :]
    cos, sin = jnp.cos(ang), jnp.sin(ang)
    cos2 = jnp.concatenate([cos, cos, cos, cos], -1)
    sin2 = jnp.concatenate([-sin, sin, -sin, sin], -1)
    return cos2, sin2


def _inproj_kernel(x_ref, nw_ref, w_ref, cos_ref, sin_ref, ml_ref, rw_ref, q_ref, kv_ref, win_ref, sm_ref, ck_ref,
                   cv_ref):
    x = x_ref[...]
    xn = x * lax.rsqrt(jnp.mean(x * x, axis=-1, keepdims=True) + RMS_EPS) * nw_ref[...]
    acc = _dot(xn.astype(BF16), w_ref[...])
    cos = cos_ref[...]
    sin = sin_ref[...]
    cos4 = jnp.concatenate([cos] * 4, axis=-1)
    sin4 = jnp.concatenate([sin] * 4, axis=-1)
    ml_ref[...] = acc[:, C_ML:C_ML + 1024]
    rw_ref[...] = acc[:, C_RW:C_RW + 1024]
    rope = lambda v, c, s: v * c + _swap_rotary_halves(v) * s
    q = acc[:, C_Q:C_Q + NSA_WIDTH]
    q_ref[:, 0:NSA_WIDTH] = q
    q_ref[:, NSA_WIDTH:2 * NSA_WIDTH] = rope(q, cos4, sin4)
    kv_ref[:, 0:2 * KV_WIDTH] = acc[:, C_KV:C_KV + 2 * KV_WIDTH]
    kv_ref[:, 2 * KV_WIDTH:3 * KV_WIDTH] = rope(acc[:, C_KV + 2 * KV_WIDTH:C_KV + 3 * KV_WIDTH], cos, sin)
    kv_ref[:, 3 * KV_WIDTH:4 * KV_WIDTH] = acc[:, C_KV + 3 * KV_WIDTH:C_KV + 4 * KV_WIDTH]
    win_ref[:, 0:KV_WIDTH] = rope(acc[:, C_WK:C_WK + KV_WIDTH], cos, sin)
    win_ref[:, KV_WIDTH:2 * KV_WIDTH] = acc[:, C_WV:C_WV + KV_WIDTH]
    sm_ref[...] = acc[:, C_SM:C_SM + LANE]
    ck_ref[...] = acc[:, C_KV:C_KV + KV_WIDTH]
    cv_ref[...] = acc[:, C_KV + KV_WIDTH:C_KV + 2 * KV_WIDTH]


def _inproj(x2, norm_w, w_packed, cos2, sin2, T):
    N = x2.shape[0]
    tm = min(256, N)
    if T >= tm:
        assert T % tm == 0
        nt = T // tm
    else:
        assert tm % T == 0
        cos2 = jnp.tile(cos2, (tm // T, 1))
        sin2 = jnp.tile(sin2, (tm // T, 1))
        nt = 1
    row = lambda w: pl.BlockSpec((tm, w), lambda i: (i, 0))
    tab = pl.BlockSpec((tm, LANE), lambda i: (i % nt, 0))
    widths = (1024, 1024, 2 * NSA_WIDTH, KV_ROW, 2 * KV_WIDTH, LANE, KV_WIDTH, KV_WIDTH)
    return pl.pallas_call(
        _inproj_kernel,
        grid=(N // tm,),
        in_specs=[row(D_MODEL), pl.BlockSpec((1, D_MODEL), lambda i: (0, 0)),
                  pl.BlockSpec((D_MODEL, C_TOT), lambda i: (0, 0)), tab, tab],
        out_specs=[row(w) for w in widths],
        out_shape=[jax.ShapeDtypeStruct((N, w), F32) for w in widths],
        compiler_params=_cp(("parallel",)),
        name="inproj",
    )(x2, norm_w.reshape(1, D_MODEL), w_packed, cos2, sin2)


def _mlstm_kernel(ml_ref, sm_ref, conv0_ref, c0_ref, n0_ref, m0_ref, cw_ref, cb_ref, gb_ref, nw_ref,
                  h_ref, c_ref, n_ref, m_ref, conv_ref, xp_ref, *, Bb, L):
    c = pl.program_id(1)
    H, D = ML_HEADS, HEAD_DIM
    PAD = 8
    KC = ML_CONV - 1

    @pl.when(c == 0)
    def _():
        c_ref[...] = c0_ref[...]
        n_ref[...] = n0_ref[...]
        m_ref[...] = m0_ref[...]
        xp_ref[:, PAD - KC:PAD, :] = conv0_ref[...]

    tril = (_iota((L, L), 1) <= _iota((L, L), 0))
    eye = (_iota((L, L), 1) == _iota((L, L), 0))
    tril_b = jnp.where(tril, 1.0, 0.0).astype(BF16)

    bs = [dict(b=b) for b in range(Bb)]
    for s in bs:
        b = s['b']
        xp_ref[b, PAD:PAD + L, :] = ml_ref[b, :, 0:2 * ML_WIDTH]
    for s in bs:
        b = s['b']
        y = jnp.broadcast_to(cb_ref[...], (L, 2 * ML_WIDTH))
        for j in range(ML_CONV):
            y = y + xp_ref[b, PAD - KC + j:PAD - KC + j + L, :] * cw_ref[j:j + 1, :]
        s['qk'] = _silu(y)
        gates = sm_ref[b] + gb_ref[...]
        lf = jnp.minimum(gates, 0.0) - jnp.log1p(jnp.exp(-jnp.abs(gates)))
        lf_hi = lf.astype(BF16)
        lf_lo = (lf - lf_hi.astype(F32)).astype(BF16)
        s['gates'] = gates
        s['bcum'] = _dot(tril_b, lf_hi) + _dot(tril_b, lf_lo)
    for s in bs:
        b = s['b']
        tail = xp_ref[b, PAD + L - KC:PAD + L, :]
        xp_ref[b, PAD - KC:PAD, :] = tail
        conv_ref[b] = tail
    for s in bs:
        s['b_all'] = pltpu.roll(s['bcum'], LANE - H, 1)
        s['a_all'] = s['gates'] - s['b_all']
        s['cmax'] = s['a_all']
    step = 1
    while step < L:
        for s in bs:
            s['cmax'] = jnp.where(_iota((L, 1), 0) >= step, jnp.maximum(s['cmax'], pltpu.roll(s['cmax'], step, 0)),
                                  s['cmax'])
        step *= 2
    units = []
    for s in bs:
        b, b_all, a_all = s['b'], s['b_all'], s['a_all']
        m_prev = m_ref[b]
        m_t = b_all + jnp.maximum(m_prev, s['cmax'])
        m_new = m_t[L - 1:L, :]
        b_last = b_all[L - 1:L, :]
        m_ref[b] = m_new
        inter = jnp.exp(b_all + m_prev - m_t)
        wgt = jnp.exp(b_last + a_all - m_new)
        dec = jnp.exp(b_last + m_prev - m_new)
        ccol = b_all - m_t
        einv = jnp.exp(-m_t)
        qk = s['qk']
        for h in range(H):
            lo = h * D
            col = lambda x: x[:, h:h + 1]
            units.append(dict(b=b, h=h, lo=lo, q=qk[:, lo:lo + D], k=qk[:, ML_WIDTH + lo:ML_WIDTH + lo + D] * (D ** -0.5),
                              v=ml_ref[b, :, 2 * ML_WIDTH + lo:2 * ML_WIDTH + lo + D], acol=col(a_all),
                              ccol=col(ccol), inter=col(inter), wgt=col(wgt), dec=col(dec), einv=col(einv)))

    ones_col = jnp.ones((L, D), BF16)
    for u in units:
        u['arow'] = jnp.sum(jnp.where(eye, u['acol'], 0.0), axis=0, keepdims=True)
    for u in units:
        u['dm'] = jnp.exp(jnp.where(tril, u['ccol'] + u['arow'], -jnp.inf))
    for u in units:
        C = c_ref[u['b'], u['h']]
        qb, kb = u['q'].astype(BF16), u['k'].astype(BF16)
        if 2 * L == LANE:
            m1 = _dot_nt(qb, jnp.concatenate([kb, C.astype(BF16)], axis=0))
            u['s_raw'], u['qc'] = m1[:, 0:L], m1[:, L:]
        else:
            u['s_raw'], u['qc'] = _dot_nt(qb, kb), _dot_nt(qb, C.astype(BF16))
        c_ref[u['b'], u['h']] = u['dec'] * C + _dot_tn((u['v'] * u['wgt']).astype(BF16), kb)
        nrow = n_ref[u['b'], u['h']:u['h'] + 1, :]
        u['qn'] = jnp.sum(u['q'] * nrow, axis=-1, keepdims=True)
        n_ref[u['b'], u['h']:u['h'] + 1, :] = u['dec'] * nrow + jnp.sum(u['wgt'] * u['k'], axis=0, keepdims=True)
    for u in units:
        S = (u['s_raw'] * u['dm']).astype(BF16)
        sv = _dot(S, jnp.concatenate([u['v'].astype(BF16), ones_col], axis=1))
        den = u['inter'] * u['qn'] + sv[:, D:D + 1]
        u['hh'] = (u['inter'] * u['qc'] + sv[:, 0:D]) / jnp.maximum(jnp.abs(den), u['einv'])
    ones = _head_ones(ML_WIDTH).astype(F32) * (1.0 / D)
    for s in bs:
        s['hh'] = jnp.concatenate([u['hh'] for u in units if u['b'] == s['b']], axis=-1)
        s['mu'] = _dot(s['hh'], ones, HIGHEST)
    for s in bs:
        s['var'] = _dot(jnp.square(s['hh'] - s['mu']), ones, HIGHEST)
    for s in bs:
        b = s['b']
        hn = (s['hh'] - s['mu']) * lax.rsqrt(s['var'] + ML_NORM_EPS) * nw_ref[...]
        h_ref[b] = hn * _sigmoid(ml_ref[b, :, 3 * ML_WIDTH:4 * ML_WIDTH])


def _mlstm(ml, sm, conv0, C0, n0, m0, conv_w, conv_b, b_i, b_f, norm_w, B, T):
    L = math.gcd(T, ML_CHUNK)
    NC = T // L
    N = B * T
    Bb = 8 if B % 8 == 0 else B
    gate_bias = jnp.concatenate([b_i, b_f, jnp.zeros((LANE - 2 * ML_HEADS,), F32)]).reshape(1, LANE)
    m0p = jnp.pad(m0.reshape(B, 1, ML_HEADS), ((0, 0), (0, 0), (0, LANE - ML_HEADS)))
    per_b = lambda shape: pl.BlockSpec((Bb,) + shape, lambda b, c: (b,) + (0,) * len(shape))
    const = lambda shape: pl.BlockSpec(shape, lambda b, c: (0,) * len(shape))
    rows = lambda w: pl.BlockSpec((Bb, L, w), lambda b, c: (b, c, 0))
    state_shapes = [(ML_HEADS, HEAD_DIM, HEAD_DIM), (ML_HEADS, HEAD_DIM), (1, LANE), (ML_CONV - 1, 2 * ML_WIDTH)]
    h, C, n, m, conv = pl.pallas_call(
        functools.partial(_mlstm_kernel, Bb=Bb, L=L),
        grid=(B // Bb, NC),
        in_specs=[rows(1024), rows(LANE), per_b(state_shapes[3]), per_b(state_shapes[0]), per_b(state_shapes[1]),
                  per_b(state_shapes[2]), const((ML_CONV, 2 * ML_WIDTH)), const((1, 2 * ML_WIDTH)),
                  const((1, LANE)), const((1, ML_WIDTH))],
        out_specs=[rows(ML_WIDTH)] + [per_b(s) for s in state_shapes],
        out_shape=[jax.ShapeDtypeStruct((B, T, ML_WIDTH), F32)]
                  + [jax.ShapeDtypeStruct((B,) + s, F32) for s in state_shapes],
        scratch_shapes=[pltpu.VMEM((Bb, L + 8, 2 * ML_WIDTH), F32)],
        compiler_params=_cp(("parallel", "arbitrary")),
        name="mlstm",
    )(ml.reshape(B, T, 1024), sm.reshape(B, T, LANE), conv0, C0, n0, m0p, conv_w, conv_b.reshape(1, -1), gate_bias,
      norm_w.reshape(1, -1))
    return h.reshape(N, ML_WIDTH), C, n, m[:, 0, :ML_HEADS], conv


def _rwkv_prep(rw_ref, shift0_ref, mu_ref, w0_ref, w2_ref, a0_ref, a2_ref, g2_ref, kk_ref, ka_ref, rk_ref,
               gb_ref, shift_ref, carry_ref, *, Bb, tt):
    c = pl.program_id(1)
    W = RW_WIDTH
    C = rw_ref.shape[-1]

    @pl.when(c == 0)
    def _():
        carry_ref[...] = shift0_ref[...]

    x = rw_ref[...].reshape(Bb * tt, C)
    carry_rows = jnp.broadcast_to(carry_ref[...], (Bb, tt, C)).reshape(Bb * tt, C)
    prev = jnp.where(_iota((Bb * tt, 1), 0) % tt == 0, carry_rows, pltpu.roll(x, 1, 0))
    last = rw_ref[:, tt - 1:tt, :]
    carry_ref[...] = last
    shift_ref[...] = last
    xm = x + (prev - x) * mu_ref[...]
    r = xm[:, 0:W]
    k = xm[:, W:2 * W]
    v = xm[:, 2 * W:3 * W]
    wl = xm[:, 3 * W:3 * W + RW_LORA_W]
    al = xm[:, 3 * W + RW_LORA_W:3 * W + RW_LORA_W + RW_LORA_A]
    gl = xm[:, 3 * W + RW_LORA_W + RW_LORA_A:]
    wpre = w0_ref[...] + _dot(jnp.tanh(wl), w2_ref[...], HIGHEST)
    softplus_neg = jnp.maximum(-wpre, 0.0) + jnp.log1p(jnp.exp(-jnp.abs(wpre)))
    log_decay = -jnp.exp(-softplus_neg - 0.5)
    a = _sigmoid(a0_ref[...] + _dot(al, a2_ref[...], HIGHEST))
    g = _dot(_sigmoid(gl), g2_ref[...], HIGHEST)
    ones = _head_ones(W)
    kk = k * kk_ref[...]
    kk = kk / jnp.maximum(jnp.sqrt(_head_sum(kk * kk, ones)), 1e-12)
    k2 = k * (1.0 + (a - 1.0) * ka_ref[...])
    bonus = _head_sum(r * k2 * rk_ref[...], ones) * v
    gb_ref[...] = jnp.concatenate([g, bonus], axis=-1).reshape(Bb, tt, 2 * W)
    return r, log_decay, k2, v, kk, a * kk


def _rwkv_kernel(rw_ref, shift0_ref, mu_ref, w0_ref, w2_ref, a0_ref, a2_ref, g2_ref, kk_ref, ka_ref, rk_ref, s0_ref,
                 o_ref, gb_ref, shift_ref, s_ref, carry_ref, *, Bb, L):
    c = pl.program_id(1)
    W, D = RW_WIDTH, HEAD_DIM

    @pl.when(c == 0)
    def _():
        s_ref[...] = s0_ref[...]

    row, col = _iota((L, L), 0), _iota((L, L), 1)
    incl = col <= row
    strict = col < row
    tril_b = jnp.where(incl, 1.0, 0.0).astype(BF16)
    mm = lambda a, b: _dot(a.astype(BF16), b.astype(BF16))
    mm_nt = lambda a, b: _dot_nt(a.astype(BF16), b.astype(BF16))
    mm_tn = lambda a, b: _dot_tn(a.astype(BF16), b.astype(BF16))

    prepped = _rwkv_prep(rw_ref, shift0_ref, mu_ref, w0_ref, w2_ref, a0_ref, a2_ref, g2_ref, kk_ref, ka_ref, rk_ref,
                         gb_ref, shift_ref, carry_ref, Bb=Bb, tt=L)
    units = []
    for b in range(Bb):
        r, logw, k, v, kk, bb = (x[b * L:(b + 1) * L] for x in prepped)
        logw_hi = logw.astype(BF16)
        logw_lo = (logw - logw_hi.astype(F32)).astype(BF16)
        cum = _dot(tril_b, logw_hi) + _dot(tril_b, logw_lo)
        g = jnp.exp(cum)
        ginv = jnp.exp(-cum)
        khat = kk * jnp.exp(cum - logw)
        rhat = r * g
        kd = k * ginv
        bd = bb * ginv
        glast = g[L - 1:L, :]
        for h in range(RW_HEADS):
            sl = slice(h * D, (h + 1) * D)
            units.append(dict(b=b, h=h, sl=sl, khat=khat[:, sl].astype(BF16), rhat=rhat[:, sl].astype(BF16),
                              kd=kd[:, sl], bd=bd[:, sl], v=v[:, sl].astype(BF16), gl=glast[:, sl]))

    for u in units:
        S0 = s_ref[u['b'], u['h']]
        kdb, bdb = u['kd'].astype(BF16), u['bd'].astype(BF16)
        if 2 * L == LANE:
            lhs = jnp.concatenate([u['khat'], u['rhat']], axis=0)
            m1 = _dot_nt(lhs, jnp.concatenate([bdb, kdb], axis=0))
            m2 = _dot_nt(lhs, S0.astype(BF16))
            q_raw, p_raw, aub_raw, avk_raw = m1[0:L, 0:L], m1[0:L, L:], m1[L:, 0:L], m1[L:, L:]
            ks0, u['rs0'] = m2[0:L], m2[L:]
        else:
            q_raw, p_raw = _dot_nt(u['khat'], bdb), _dot_nt(u['khat'], kdb)
            aub_raw, avk_raw = _dot_nt(u['rhat'], bdb), _dot_nt(u['rhat'], kdb)
            ks0, u['rs0'] = _dot_nt(u['khat'], S0.astype(BF16)), _dot_nt(u['rhat'], S0.astype(BF16))
        u['Q'] = jnp.where(strict, q_raw, 0.0).astype(BF16)
        u['P'] = jnp.where(strict, p_raw, 0.0).astype(BF16)
        u['avk'] = jnp.where(incl, avk_raw, 0.0).astype(BF16)
        u['aub'] = jnp.where(incl, aub_raw, 0.0).astype(BF16)
        u['X'] = ks0
    for u in units:
        u['X'] = u['X'] + _dot(u['P'], u['v'])
    for u in units:
        u['X'] = u['X'] - mm(u['Q'], u['X'])
    n = 2
    while n < L:
        for u in units:
            u['Q'] = _dot(u['Q'], u['Q']).astype(BF16)
        for u in units:
            u['X'] = u['X'] + mm(u['Q'], u['X'])
        n *= 2
    for u in units:
        xb = u['X'].astype(BF16)
        o_ref[u['b'], :, u['sl']] = u['rs0'] + _dot(u['avk'], u['v']) - _dot(u['aub'], xb)
        gl = u['gl']
        S0 = s_ref[u['b'], u['h']]
        s_ref[u['b'], u['h']] = S0 * gl + _dot_tn(u['v'], (u['kd'] * gl).astype(BF16)) - _dot_tn(xb, (u['bd'] * gl).astype(BF16))


def _rwkv(rw, shift0p, S0, mu_p, w0, w2, a0, a2, g2, k_k, k_a, r_k, B, T):
    N = B * T
    W = RW_WIDTH
    Bb = 8 if B % 8 == 0 else B
    L = math.gcd(T, RW_CHUNK)
    const = lambda shape: pl.BlockSpec(shape, lambda b, c: (0,) * len(shape))
    blk = lambda w: pl.BlockSpec((Bb, L, w), lambda b, c: (b, c, 0))
    per_seq = pl.BlockSpec((Bb, 1, 1024), lambda b, c: (b, 0, 0))
    st = pl.BlockSpec((Bb, RW_HEADS, HEAD_DIM, HEAD_DIM), lambda b, c: (b, 0, 0, 0))
    vec = lambda a: a.reshape(1, -1)
    y, gb, shift, S = pl.pallas_call(
        functools.partial(_rwkv_kernel, Bb=Bb, L=L),
        grid=(B // Bb, T // L),
        in_specs=[blk(1024), per_seq, const((1, 1024)),
                  const((1, W)), const((RW_LORA_W, W)), const((1, W)), const((RW_LORA_A, W)), const((RW_LORA_G, W)),
                  const((1, W)), const((1, W)), const((1, W)), st],
        out_specs=[blk(W), blk(2 * W), per_seq, st],
        out_shape=[jax.ShapeDtypeStruct((B, T, W), F32), jax.ShapeDtypeStruct((B, T, 2 * W), F32),
                   jax.ShapeDtypeStruct((B, 1, 1024), F32), jax.ShapeDtypeStruct(S0.shape, F32)],
        scratch_shapes=[pltpu.VMEM((Bb, 1, 1024), F32)],
        compiler_params=_cp(("parallel", "arbitrary")),
        name="rwkv",
    )(rw.reshape(B, T, 1024), shift0p.reshape(B, 1, 1024), vec(mu_p), vec(w0), w2, vec(a0), a2, g2, vec(k_k),
      vec(k_a), vec(r_k), S0)
    return y.reshape(N, W), gb.reshape(N, 2 * W), S, shift[:, 0]


def _compress_slab(xg, j, posl_ref, posh_ref, wl_ref, wh_ref, w2_ref):
    G = xg.shape[0]
    lo = _dot((xg + posl_ref[j]).astype(BF16), wl_ref[j])
    hi = _dot((xg + posh_ref[j]).astype(BF16), wh_ref[j])
    hid = _silu(lo + pltpu.roll(hi, G - 1, 0))
    return _dot(hid.astype(BF16), w2_ref[j])


def _compress_kernel(xk_ref, xv_ref, posl_ref, posh_ref, wl_ref, wh_ref, w2_ref, o_ref, *, G):
    for j, x_ref in enumerate((xk_ref, xv_ref)):
        xg = jnp.concatenate([x_ref.at[0][pl.ds(r, G, stride=GRP_ROWS), :] for r in range(GRP_ROWS)], axis=-1)
        o_ref[0, :, j * KV_WIDTH:(j + 1) * KV_WIDTH] = _compress_slab(xg, j, posl_ref, posh_ref, wl_ref, wh_ref, w2_ref)


def _pack_cmp_weights(cmp_pos, cmp_w1, cmp_w2):
    eye = jnp.eye(NSA_KV_HEADS, dtype=F32)
    w1 = cmp_w1.reshape(2, CMP_LEN, HEAD_DIM, CMP_HID)
    wbig = jnp.einsum('jrde,bg->jrbdge', w1, eye).reshape(2, CMP_LEN, KV_WIDTH, NSA_KV_HEADS * CMP_HID).astype(BF16)
    w_lo = wbig[:, :GRP_ROWS].reshape(2, GRP_ROWS * KV_WIDTH, NSA_KV_HEADS * CMP_HID)
    w_hi = wbig[:, GRP_ROWS:].reshape(2, GRP_ROWS * KV_WIDTH, NSA_KV_HEADS * CMP_HID)
    w2big = jnp.einsum('jed,bg->jbegd', cmp_w2, eye).reshape(2, NSA_KV_HEADS * CMP_HID, KV_WIDTH).astype(BF16)
    pos = jnp.broadcast_to(cmp_pos[:, :, None, :], (2, CMP_LEN, NSA_KV_HEADS, HEAD_DIM))
    pos_lo = pos[:, :GRP_ROWS].reshape(2, 1, -1)
    pos_hi = pos[:, GRP_ROWS:].reshape(2, 1, -1)
    return w_lo, w_hi, w2big, pos_lo, pos_hi


_CMP_K = GRP_ROWS * KV_WIDTH
_CMP_SPECS = [(2, 1, _CMP_K), (2, 1, _CMP_K), (2, _CMP_K, NSA_KV_HEADS * CMP_HID), (2, _CMP_K, NSA_KV_HEADS * CMP_HID),
              (2, NSA_KV_HEADS * CMP_HID, KV_WIDTH)]


def _overlap_map(n_rows, n_cmp, n_sel):
    cs = np.arange(n_rows) * CMP_STRIDE
    ss = np.arange(LANE) * SEL_BLOCK
    ov = np.clip(np.minimum(cs[:, None] + CMP_LEN, ss[None, :] + SEL_BLOCK) - np.maximum(cs[:, None], ss[None, :]),
                 0, None) / CMP_LEN
    ov[n_cmp:, :] = 0.0
    ov[:, n_sel:] = 0.0
    return ov


def _compress(xk, xv, cmp_pos, cmp_w1, cmp_w2, B, G):
    rows = xk.shape[1]
    w_lo, w_hi, w2big, pos_lo, pos_hi = _pack_cmp_weights(cmp_pos, cmp_w1, cmp_w2)
    const = lambda shape: pl.BlockSpec(shape, lambda b: (0,) * len(shape))
    slab = pl.BlockSpec((1, rows, KV_WIDTH), lambda b: (b, 0, 0))
    return pl.pallas_call(
        functools.partial(_compress_kernel, G=G),
        grid=(B,),
        in_specs=[slab, slab] + [const(s) for s in _CMP_SPECS],
        out_specs=pl.BlockSpec((1, G, 2 * KV_WIDTH), lambda b: (b, 0, 0)),
        out_shape=jax.ShapeDtypeStruct((B, G, 2 * KV_WIDTH), F32),
        compiler_params=_cp(("parallel",)),
        name="nsa_compress",
    )(xk, xv, pos_lo, pos_hi, w_lo, w_hi, w2big)


def _softmax_pv(s, bias, v_aug, *, QB, v_keys_minor):
    G, D = NSA_GROUP, HEAD_DIM
    K = s.shape[-1]
    s3 = s.reshape(G, QB, K) + bias[None]
    p = jnp.exp((s3 - jnp.max(s3, axis=-1, keepdims=True)).astype(BF16)).reshape(G * QB, K)
    oa = _dot_nt(p, v_aug) if v_keys_minor else _dot(p, v_aug)
    return oa[:, 0:D] / jnp.maximum(oa[:, D:D + 1], 1e-30)


def _cmp_select(units, ov, QB, NS, n_rank):
    G = NSA_GROUP
    blk = _iota((1, LANE), 1)
    for u in units:
        u['s_c'] = _dot_nt(u['q_raw'], u['ck'].astype(BF16))
    for u in units:
        cmp_end = _iota((1, u['ck'].shape[0]), 1) * CMP_STRIDE + (CMP_LEN - 1)
        u['p_c'] = _masked_softmax(u['s_c'], cmp_end <= u['tq_r'])
    for u in units:
        p_c = u['p_c']
        u['o_c'] = _dot(p_c.astype(BF16), u['cv'].astype(BF16))
        psum = p_c[0:QB]
        for h in range(1, G):
            psum = psum + p_c[h * QB:(h + 1) * QB]
        imp = _dot(psum, ov, HIGHEST)
        tq = u['tq']
        cur = tq // SEL_BLOCK
        avail = blk * SEL_BLOCK <= tq
        forced = (blk == 0) | (blk == cur) | (blk == cur - 1)
        u['score'] = jnp.where(avail, imp + jnp.where(forced, FORCE_BONUS, 0.0), SEL_NEG)
        u['cnt'] = jnp.zeros((QB, LANE), F32)
    for i in range(n_rank):
        for u in units:
            score = u['score']
            col = score[:, i:i + 1]
            u['cnt'] = u['cnt'] + jnp.where((col > score) | ((col >= score) & (blk > i)), 1.0, 0.0)
    for u in units:
        u['sel'] = ((u['cnt'] < min(SEL_TOPN, NS)) & (u['score'] > 0.5 * SEL_NEG)).astype(BF16)


def _head_stack(x, base, g):
    G, D = NSA_GROUP, HEAD_DIM
    return jnp.concatenate([x[:, base + (g * G + h) * D:base + (g * G + h + 1) * D] for h in range(G)], 0)


def _gate_stack(gates, g, j):
    c0 = 2 * ML_HEADS + 3 * g * NSA_GROUP + j
    return jnp.concatenate([gates[:, c0 + 3 * h:c0 + 3 * h + 1] for h in range(NSA_GROUP)], 0)


def _softmax_rows(s_ref, b_ref, p_ref, K, QB, R):
    s3 = s_ref[:, 0:K].reshape(R // QB, QB, K) + b_ref[:, 0:K][None]
    p_ref[:, 0:K] = jnp.exp((s3 - jnp.max(s3, axis=-1, keepdims=True)).astype(BF16)).reshape(R, K)


def _pv_normalised(p, v_aug):
    D = HEAD_DIM
    oa = _dot(p, v_aug)
    return oa[:, 0:D] / jnp.maximum(oa[:, D:D + 1], 1e-30)


def _nsa_kernel(q_ref, sm_ref, ckv_ref, kv_ref, win_ref, ov_ref, ex_ref, o_ref, skb, svb, wkb, wvb,
                s_scr, p_scr, bs_scr, sw_scr, pw_scr, bw_scr, *, QB, NB, NS, P, WN, nvar):
    nb = pl.program_id(1)
    D = HEAD_DIM
    G = NSA_GROUP
    R = G * QB
    LW = wkb.shape[1]

    @pl.when(nb == 0)
    def _():
        for g in range(NSA_KV_HEADS):
            skb[g] = kv_ref[0, :, g * D:(g + 1) * D].astype(BF16)
            svb[g, :, 0:D] = kv_ref[0, :, KV_WIDTH + g * D:KV_WIDTH + (g + 1) * D].astype(BF16)
            svb[g, :, D:2 * D] = jnp.ones((P, D), BF16)
            wkb[g] = win_ref[0, :, g * D:(g + 1) * D].astype(BF16)
            wvb[g, :, 0:D] = win_ref[0, :, KV_WIDTH + g * D:KV_WIDTH + (g + 1) * D].astype(BF16)
            wvb[g, :, D:2 * D] = jnp.ones((LW, D), BF16)

    def body(PS):
        qs = nb * QB
        tq = qs + _iota((QB, 1), 0)
        tq_r = qs + (_iota((R, 1), 0) % QB)
        gates = _sigmoid(sm_ref[...])
        kpos = _iota((1, PS), 1)
        wstart = pl.multiple_of(jnp.clip(qs - WINDOW, 0, LW - WN), 16)
        dlt = tq - (wstart + _iota((1, WN), 1))
        bw_scr[...] = jnp.where((dlt >= 0) & (dlt < WINDOW), 0.0, NEG)
        qv = q_ref[...] * (D ** -0.5)

        units = [dict(g=g, tq=tq, tq_r=tq_r, q_raw=_head_stack(qv, 0, g).astype(BF16),
                      q_rot=_head_stack(qv, NSA_WIDTH, g).astype(BF16), ck=ckv_ref[0, :, g * D:(g + 1) * D],
                      cv=ckv_ref[0, :, KV_WIDTH + g * D:KV_WIDTH + (g + 1) * D]) for g in range(NSA_KV_HEADS)]
        for u in units:
            g = u['g']
            s_scr[g, :, 0:PS] = _dot_nt(u['q_rot'], skb[g, 0:PS, :])
            sw_scr[g] = _dot_nt(u['q_rot'], wkb[g, pl.ds(wstart, WN), :])
        for u in units:
            g = u['g']
            _softmax_rows(sw_scr.at[g], bw_scr, pw_scr.at[g], WN, QB, R)
            u['o_w'] = _pv_normalised(pw_scr[g], wvb[g, pl.ds(wstart, WN), :])
        _cmp_select(units, ov_ref[...], QB, NS, min(NS, PS // SEL_BLOCK))
        for u in units:
            selk = _dot(u['sel'], ex_ref[:, 0:PS])
            bs_scr[u['g'], :, 0:PS] = jnp.where((selk > 0.5) & (kpos <= tq), 0.0, NEG)
        outs = []
        for u in units:
            g = u['g']
            _softmax_rows(s_scr.at[g], bs_scr.at[g], p_scr.at[g], PS, QB, R)
            o_s = _pv_normalised(p_scr[g, :, 0:PS], svb[g, 0:PS, :])
            o = _gate_stack(gates, g, 0) * u['o_c'] + _gate_stack(gates, g, 1) * o_s + _gate_stack(gates, g, 2) * u['o_w']
            outs.extend(o[h * QB:(h + 1) * QB] for h in range(G))
        o_ref[...] = jnp.concatenate(outs, axis=-1)

    for v in range(nvar):
        pl.when(nb // (NB // nvar) == v)(functools.partial(body, P * (v + 1) // nvar))


def _nsa_attend(qq, sm, ckv, kv_rows, win_rows, B, T):
    N = B * T
    QB = math.gcd(T, Q_BLOCK)
    NB = T // QB
    P = LW = T
    NS = -(-T // SEL_BLOCK)
    NC = (T - CMP_LEN) // CMP_STRIDE + 1
    NCP = ckv.shape[1]
    WN = -(-(WINDOW + QB) // LANE) * LANE
    assert NS <= LANE and NC <= NCP and P % LANE == 0 and QB % 32 == 0 and LW >= WN and (LW - WN) % 16 == 0
    ov = _overlap_map(NCP, NC, NS)
    expand = (np.arange(P)[None, :] // SEL_BLOCK == np.arange(LANE)[:, None]).astype(np.float32)
    const = lambda shape: pl.BlockSpec(shape, lambda b, nb: (0,) * len(shape))
    rows = lambda w: pl.BlockSpec((QB, w), lambda b, nb: (b * NB + nb, 0))
    nvar = next((v for v in (8, 4, 2) if NB % v == 0 and (P // v) % (2 * LANE) == 0), 1)
    KVH, R = NSA_KV_HEADS, NSA_GROUP * QB
    return pl.pallas_call(
        functools.partial(_nsa_kernel, QB=QB, NB=NB, NS=NS, P=P, WN=WN, nvar=nvar),
        grid=(B, NB),
        in_specs=[rows(2 * NSA_WIDTH), rows(LANE),
                  pl.BlockSpec((1, NCP, 2 * KV_WIDTH), lambda b, nb: (b, 0, 0)),
                  pl.BlockSpec((1, P, 2 * KV_WIDTH), lambda b, nb: (b, 0, 1)),
                  pl.BlockSpec((1, LW, 2 * KV_WIDTH), lambda b, nb: (b, 0, 0)),
                  const((NCP, LANE)), const((LANE, P))],
        out_specs=rows(NSA_WIDTH),
        out_shape=jax.ShapeDtypeStruct((N, NSA_WIDTH), F32),
        scratch_shapes=[pltpu.VMEM((KVH, P, HEAD_DIM), BF16), pltpu.VMEM((KVH, P, 2 * HEAD_DIM), BF16),
                        pltpu.VMEM((KVH, LW, HEAD_DIM), BF16), pltpu.VMEM((KVH, LW, 2 * HEAD_DIM), BF16),
                        pltpu.VMEM((KVH, R, P), F32), pltpu.VMEM((KVH, R, P), BF16), pltpu.VMEM((KVH, QB, P), F32),
                        pltpu.VMEM((KVH, R, WN), F32), pltpu.VMEM((KVH, R, WN), BF16), pltpu.VMEM((QB, WN), F32)],
        compiler_params=_cp(("parallel", "arbitrary")),
        name="nsa_attend",
    )(qq, sm, ckv, kv_rows, win_rows, jnp.asarray(ov, F32), jnp.asarray(expand, BF16))


def _nsa_paged_kernel(pt_ref, *refs, SEQ, n_pages, page, T, NS, offset, lwin):
    del pt_ref
    page_refs = refs[:SEQ * n_pages]
    (q_ref, sm_ref, new_ref, wcache_ref, wnew_ref, posl_ref, posh_ref, wl_ref, wh_ref, w2_ref, ov_ref, ex_ref,
     o_ref, wout_ref, x_ref) = refs[SEQ * n_pages:]
    D, G = HEAD_DIM, NSA_GROUP
    QB = T
    R = G * QB
    P = (n_pages + 1) * page
    NG = n_pages * page // GRP_ROWS
    pages = lambda s: page_refs[s * n_pages:(s + 1) * n_pages]

    def new_tile(x, c0):
        return x[:, c0:c0 + D].T.astype(BF16)

    for s in range(SEQ):
        for i in range(n_pages):
            for j in range(2):
                for g in range(NSA_KV_HEADS):
                    x_ref[s, j, i * page:(i + 1) * page, g * D:(g + 1) * D] = pages(s)[i][0, j, g].T
    ckv = {}
    for s in range(SEQ):
        for j in range(2):
            xg = jnp.concatenate([x_ref.at[s, j][pl.ds(r, NG, stride=GRP_ROWS), :] for r in range(GRP_ROWS)], axis=-1)
            ckv[s, j] = _compress_slab(xg, j, posl_ref, posh_ref, wl_ref, wh_ref, w2_ref)

    tq = offset + _iota((QB, 1), 0)
    tq_r = offset + (_iota((R, 1), 0) % QB)
    kpos = _iota((1, P), 1)
    dlt = tq - (offset - lwin + _iota((1, lwin + page), 1))
    bias_w = jnp.where((dlt >= 0) & (dlt < WINDOW), 0.0, NEG)
    zrows = lambda w: jnp.zeros((page - T, w), F32)
    softmax_pv = functools.partial(_softmax_pv, QB=QB, v_keys_minor=True)

    units = []
    for s in range(SEQ):
        rows = slice(s * T, (s + 1) * T)
        qv = q_ref[rows, :] * (D ** -0.5)
        gates = _sigmoid(sm_ref[rows, :])
        new_t = jnp.concatenate([new_ref[rows, :], zrows(KV_ROW)], axis=0)
        wnew_t = jnp.concatenate([wnew_ref[rows, :], zrows(2 * KV_WIDTH)], axis=0)
        for g in range(NSA_KV_HEADS):
            kt = jnp.concatenate([pg[0, 2, g].astype(BF16) for pg in pages(s)]
                                 + [new_tile(new_t, 2 * KV_WIDTH + g * D)], axis=1)
            vt = jnp.concatenate([pg[0, 3, g].astype(BF16) for pg in pages(s)]
                                 + [new_tile(new_t, 3 * KV_WIDTH + g * D)], axis=1)
            wkt = jnp.concatenate([wcache_ref[s, 0, g].astype(BF16), new_tile(wnew_t, g * D)], axis=1)
            wvt = jnp.concatenate([wcache_ref[s, 1, g].astype(BF16), new_tile(wnew_t, KV_WIDTH + g * D)], axis=1)
            units.append(dict(s=s, g=g, tq=tq, tq_r=tq_r, gates=gates, q_raw=_head_stack(qv, 0, g).astype(BF16),
                              q_rot=_head_stack(qv, NSA_WIDTH, g).astype(BF16), kt=kt, vt=vt, wkt=wkt, wvt=wvt,
                              ck=ckv[s, 0][:, g * D:(g + 1) * D], cv=ckv[s, 1][:, g * D:(g + 1) * D]))
    for u in units:
        u['s_s'] = _dot(u['q_rot'], u['kt'])
        u['s_w'] = _dot(u['q_rot'], u['wkt'])
    for u in units:
        u['o_w'] = softmax_pv(u['s_w'], bias_w, jnp.concatenate([u['wvt'], jnp.ones_like(u['wvt'])], axis=0))
    _cmp_select(units, ov_ref[...], QB, NS, NS)
    for u in units:
        bias_s = jnp.where((_dot(u['sel'], ex_ref[...]) > 0.5) & (kpos <= tq), 0.0, NEG)
        o_s = softmax_pv(u['s_s'], bias_s, jnp.concatenate([u['vt'], jnp.ones_like(u['vt'])], axis=0))
        gates, g = u['gates'], u['g']
        u['o'] = _gate_stack(gates, g, 0) * u['o_c'] + _gate_stack(gates, g, 1) * o_s + _gate_stack(gates, g, 2) * u['o_w']
    for s in range(SEQ):
        outs = [u['o'][h * QB:(h + 1) * QB] for u in units if u['s'] == s for h in range(G)]
        o_ref[s * T:(s + 1) * T, :] = jnp.concatenate(outs, axis=-1)
    for s in range(SEQ):
        wnew_t = jnp.concatenate([wnew_ref[s * T:(s + 1) * T, :], zrows(2 * KV_WIDTH)], axis=0)
        for kv in range(2):
            for g in range(NSA_KV_HEADS):
                c0 = kv * KV_WIDTH + g * D
                wout_ref[s, kv, g] = jnp.concatenate([wcache_ref[s, kv, g][:, T:], wnew_t[:, c0:c0 + D].T[:, 0:T]], axis=1)


def _nsa_paged(qq, sm, kv_new, win_new, cache_t, page_ids, wcache_t, win_base, cmp_pos, cmp_w1, cmp_w2, B, T, offset):
    N = B * T
    n_pages = page_ids.shape[1]
    page = cache_t.shape[-1]
    lwin = wcache_t.shape[-1]
    p_len = n_pages * page + T
    P = (n_pages + 1) * page
    NS = -(-p_len // SEL_BLOCK)
    n_cmp = (p_len - CMP_LEN) // CMP_STRIDE + 1
    NG = n_pages * page // GRP_ROWS
    SEQ = 2 if (B % 2 == 0 and win_base % 2 == 0) else 1
    assert T % 8 == 0 and T <= page and NS <= LANE and n_cmp + 1 == NG and lwin % LANE == 0 and NG % 8 == 0
    w_lo, w_hi, w2big, pos_lo, pos_hi = _pack_cmp_weights(cmp_pos, cmp_w1, cmp_w2)
    ov = _overlap_map(NG, n_cmp, NS)
    expand = (np.arange(P)[None, :] // SEL_BLOCK == np.arange(LANE)[:, None]).astype(np.float32)

    def page_map(k, b, pt):
        return (pt[b * SEQ * n_pages + k], 0, 0, 0, 0)

    const = lambda shape: pl.BlockSpec(shape, lambda b, pt: (0,) * len(shape))
    rows = lambda w: pl.BlockSpec((SEQ * T, w), lambda b, pt: (b, 0))
    in_specs = [pl.BlockSpec((1, 4, NSA_KV_HEADS, HEAD_DIM, page), functools.partial(page_map, k))
                for k in range(SEQ * n_pages)]
    wshape = (SEQ, 2, NSA_KV_HEADS, HEAD_DIM, lwin)
    in_specs += [rows(2 * NSA_WIDTH), rows(LANE), rows(KV_ROW),
                 pl.BlockSpec(wshape, lambda b, pt: (win_base // SEQ + b, 0, 0, 0, 0)),
                 rows(2 * KV_WIDTH)] + [const(s) for s in _CMP_SPECS] + [const((NG, LANE)), const((LANE, P))]
    return pl.pallas_call(
        functools.partial(_nsa_paged_kernel, SEQ=SEQ, n_pages=n_pages, page=page, T=T, NS=NS, offset=offset, lwin=lwin),
        grid_spec=pltpu.PrefetchScalarGridSpec(
            num_scalar_prefetch=1, grid=(B // SEQ,), in_specs=in_specs,
            out_specs=[rows(NSA_WIDTH), pl.BlockSpec(wshape, lambda b, pt: (b, 0, 0, 0, 0))],
            scratch_shapes=[pltpu.VMEM((SEQ, 2, n_pages * page, KV_WIDTH), F32)]),
        out_shape=[jax.ShapeDtypeStruct((N, NSA_WIDTH), F32), jax.ShapeDtypeStruct((B,) + wshape[1:], F32)],
        compiler_params=_cp(("parallel",)),
        name="nsa_paged",
    )(page_ids.reshape(-1), *([cache_t] * (SEQ * n_pages)), qq, sm, kv_new, wcache_t, win_new, pos_lo, pos_hi, w_lo,
      w_hi, w2big, jnp.asarray(ov, F32), jnp.asarray(expand, BF16))


def _outproj_kernel(x_ref, hml_ref, yrw_ref, gb_ref, hns_ref, lnw_ref, lnb_ref, w_ref, o_ref):
    W = RW_WIDTH
    y = yrw_ref[...]
    ones = _head_ones(W)
    mu = _head_sum(y, ones) * (1.0 / HEAD_DIM)
    var = _head_sum(jnp.square(y - mu), ones) * (1.0 / HEAD_DIM)
    yn = (y - mu) * lax.rsqrt(var + RW_GN_EPS) * lnw_ref[...]
    h_rw = (yn + lnb_ref[...] + gb_ref[:, W:2 * W]) * gb_ref[:, 0:W]
    acc = _dot(hml_ref[...].astype(BF16), w_ref[0:ML_WIDTH, :])
    acc = acc + _dot(h_rw.astype(BF16), w_ref[ML_WIDTH:ML_WIDTH + W, :])
    acc = acc + _dot(hns_ref[...].astype(BF16), w_ref[ML_WIDTH + W:, :])
    o_ref[...] = x_ref[...] + acc


def _outproj(x2, h_ml, y_rw, gb, h_ns, ln_w, ln_b, w_out):
    N = x2.shape[0]
    tm = min(512, N)
    row = lambda w: pl.BlockSpec((tm, w), lambda i: (i, 0))
    const = lambda shape: pl.BlockSpec(shape, lambda i: (0,) * len(shape))
    return pl.pallas_call(
        _outproj_kernel,
        grid=(N // tm,),
        in_specs=[row(D_MODEL), row(ML_WIDTH), row(RW_WIDTH), row(2 * RW_WIDTH), row(NSA_WIDTH),
                  const((1, RW_WIDTH)), const((1, RW_WIDTH)), const((MIX_WIDTH, D_MODEL))],
        out_specs=row(D_MODEL),
        out_shape=jax.ShapeDtypeStruct((N, D_MODEL), F32),
        compiler_params=_cp(("parallel",)),
        name="outproj",
    )(x2, h_ml, y_rw, gb, h_ns, ln_w.reshape(1, -1), ln_b.reshape(1, -1), w_out.astype(BF16))


def _rms(x, w):
    return x * lax.rsqrt(jnp.mean(x * x, axis=-1, keepdims=True) + RMS_EPS) * w


def _ffn_kernel(x_ref, nw_ref, w1_ref, w3_ref, w2_ref, fw_ref, o_ref, xn_ref, *, final_norm):
    j = pl.program_id(1)

    @pl.when(j == 0)
    def _():
        x = x_ref[...]
        xn_ref[...] = _rms(x, nw_ref[...]).astype(BF16)
        o_ref[...] = x

    xn = xn_ref[...]
    hid = _silu(_dot(xn, w1_ref[...])) * _dot(xn, w3_ref[...])
    o_ref[...] += _dot(hid.astype(BF16), w2_ref[...])

    if final_norm:
        @pl.when(j == pl.num_programs(1) - 1)
        def _():
            o_ref[...] = _rms(o_ref[...], fw_ref[...])


def _ffn(x2, norm_w, w1, w3, w2, final_w):
    N = x2.shape[0]
    d_ff = w1.shape[1]
    tm = min(512, N)
    nj = 2 if (d_ff // 2) % LANE == 0 else 1
    tf = d_ff // nj
    final_norm = final_w is not None
    fw = (final_w if final_norm else jnp.ones((D_MODEL,), F32)).reshape(1, D_MODEL)
    return pl.pallas_call(
        functools.partial(_ffn_kernel, final_norm=final_norm),
        grid=(N // tm, nj),
        in_specs=[pl.BlockSpec((tm, D_MODEL), lambda i, j: (i, 0)), pl.BlockSpec((1, D_MODEL), lambda i, j: (0, 0)),
                  pl.BlockSpec((D_MODEL, tf), lambda i, j: (0, j)), pl.BlockSpec((D_MODEL, tf), lambda i, j: (0, j)),
                  pl.BlockSpec((tf, D_MODEL), lambda i, j: (j, 0)), pl.BlockSpec((1, D_MODEL), lambda i, j: (0, 0))],
        out_specs=pl.BlockSpec((tm, D_MODEL), lambda i, j: (i, 0)),
        out_shape=jax.ShapeDtypeStruct((N, D_MODEL), F32),
        scratch_shapes=[pltpu.VMEM((tm, D_MODEL), BF16)],
        compiler_params=_cp(("parallel", "arbitrary")),
        name="ffn",
    )(x2, norm_w.reshape(1, D_MODEL), w1.astype(BF16), w3.astype(BF16), w2.astype(BF16), fw)


def _moe_kernel(x_ref, nw_ref, rt_ref, w1_ref, w3_ref, w2_ref, fw_ref, o_ref, xn_ref, comb_ref, *, final_norm):
    e = pl.program_id(1)

    @pl.when(e == 0)
    def _():
        x = x_ref[...]
        xn = _rms(x, nw_ref[...])
        xn_ref[...] = xn.astype(BF16)
        o_ref[...] = x
        lane = _iota((1, LANE), 1).astype(F32)
        logits = jnp.where(lane < N_EXPERTS, _dot(xn, rt_ref[...], HIGHEST), -jnp.inf)
        m1 = jnp.max(logits, axis=-1, keepdims=True)
        i1 = jnp.min(jnp.where(logits == m1, lane, float(LANE)), axis=-1, keepdims=True)
        rest = jnp.where(lane == i1, -jnp.inf, logits)
        m2 = jnp.max(rest, axis=-1, keepdims=True)
        i2 = jnp.min(jnp.where(rest == m2, lane, float(LANE)), axis=-1, keepdims=True)
        e2 = jnp.exp(m2 - m1)
        den = 1.0 + e2
        comb_ref[...] = jnp.where(lane == i1, 1.0 / den, 0.0) + jnp.where(lane == i2, e2 / den, 0.0)

    xn = xn_ref[...]
    comb = comb_ref[...]
    ce = jnp.sum(jnp.where(_iota((1, LANE), 1) == e, comb, 0.0), axis=-1, keepdims=True)
    hid = _silu(_dot(xn, w1_ref[0])) * _dot(xn, w3_ref[0])
    o_ref[...] += ce * _dot(hid.astype(BF16), w2_ref[0])

    if final_norm:
        @pl.when(e == pl.num_programs(1) - 1)
        def _():
            o_ref[...] = _rms(o_ref[...], fw_ref[...])


def _moe(x2, norm_w, router, w1, w3, w2, final_w):
    N = x2.shape[0]
    E, _, d_exp = w1.shape
    assert E == N_EXPERTS and TOP_K == 2
    tm = min(512, N)
    final_norm = final_w is not None
    fw = (final_w if final_norm else jnp.ones((D_MODEL,), F32)).reshape(1, D_MODEL)
    rt = jnp.pad(router, ((0, 0), (0, LANE - E)))
    return pl.pallas_call(
        functools.partial(_moe_kernel, final_norm=final_norm),
        grid=(N // tm, E),
        in_specs=[pl.BlockSpec((tm, D_MODEL), lambda i, e: (i, 0)), pl.BlockSpec((1, D_MODEL), lambda i, e: (0, 0)),
                  pl.BlockSpec((D_MODEL, LANE), lambda i, e: (0, 0)),
                  pl.BlockSpec((1, D_MODEL, d_exp), lambda i, e: (e, 0, 0)),
                  pl.BlockSpec((1, D_MODEL, d_exp), lambda i, e: (e, 0, 0)),
                  pl.BlockSpec((1, d_exp, D_MODEL), lambda i, e: (e, 0, 0)),
                  pl.BlockSpec((1, D_MODEL), lambda i, e: (0, 0))],
        out_specs=pl.BlockSpec((tm, D_MODEL), lambda i, e: (i, 0)),
        out_shape=jax.ShapeDtypeStruct((N, D_MODEL), F32),
        scratch_shapes=[pltpu.VMEM((tm, D_MODEL), BF16), pltpu.VMEM((tm, LANE), F32)],
        compiler_params=_cp(("parallel", "arbitrary")),
        name="moe",
    )(x2, norm_w.reshape(1, D_MODEL), rt, w1.astype(BF16), w3.astype(BF16), w2.astype(BF16), fw)


def _layer(l, x2, B, T, offset, st, P, final_w):
    C0, n0, m0, conv0, S0, shift0, cache, page_ids, win0 = st
    rw_perm = _rw_order()
    pos = offset + jnp.arange(T)
    cos2, sin2 = _rope_tables(pos)
    ml, rw, qq, kv4, wrow, sm, cks, cvs = _inproj(x2, P['norm_mix_w'][l], _pack_w_in(P['w_in'][l]), cos2, sin2, T)

    h_ml, C, n, m, conv = _mlstm(ml, sm, conv0, C0, n0, m0, P['ml_conv_w'][l], P['ml_conv_b'][l],
                                 P['ml_b_i'][l], P['ml_b_f'][l], P['ml_norm_w'][l], B, T)

    y_rw, gb, S, shift_p = _rwkv(rw, _take_cols(shift0, rw_perm), S0, _take_cols(P['rw_mu'][l], rw_perm),
                                 P['rw_w0'][l], P['rw_w2'][l], P['rw_a0'][l], P['rw_a2'][l], P['rw_g2'][l],
                                 P['rw_k_k'][l], P['rw_k_a'][l], P['rw_r_k'][l].reshape(-1), B, T)
    shift_new = _take_cols(shift_p, np.argsort(rw_perm))

    kv3 = kv4.reshape(B, T, KV_ROW)
    w3 = wrow.reshape(B, T, 2 * KV_WIDTH)
    cmp_w = (P['nsa_cmp_pos'][l], P['nsa_cmp_w1'][l], P['nsa_cmp_w2'][l])
    wbuf = P['wbuf']
    if cache is None:
        n_cmp = (T - CMP_LEN) // CMP_STRIDE + 1
        G = -(-(n_cmp + 1) // 8) * 8
        assert G * GRP_ROWS <= T
        ckv = _compress(cks.reshape(B, T, KV_WIDTH), cvs.reshape(B, T, KV_WIDTH), *cmp_w, B, G)
        assert offset == 0
        h_ns = _nsa_attend(qq, sm, ckv, kv3, w3, B, T)
        win_new = w3[:, T - wbuf:] if T >= wbuf else jnp.pad(w3, ((0, 0), (wbuf - T, 0), (0, 0)))
        win_new = win_new.reshape(B, wbuf, 2, NSA_KV_HEADS, HEAD_DIM)
    else:
        assert win0['key_minor'].shape[-1] == wbuf and T <= wbuf
        h_ns, wout = _nsa_paged(qq, sm, kv4, wrow, cache, page_ids, win0['key_minor'], win0['base'], *cmp_w, B, T, offset)
        win_new = jnp.transpose(wout, (0, 4, 1, 2, 3))

    x2 = _outproj(x2, h_ml, y_rw, gb, h_ns, P['rw_ln_w'][l], P['rw_ln_b'][l], P['w_out'][l])
    if l % 2 == 0:
        x2 = _ffn(x2, P['norm_ffn_w'][l], P['ffn_w1'][l // 2], P['ffn_w3'][l // 2], P['ffn_w2'][l // 2], final_w)
    else:
        x2 = _moe(x2, P['norm_ffn_w'][l], P['moe_router'][l // 2], P['moe_w1'][l // 2], P['moe_w3'][l // 2],
                  P['moe_w2'][l // 2], final_w)
    new = (kv3.reshape(B, T, 4, NSA_KV_HEADS, HEAD_DIM), win_new, C, n, m, conv, S, shift_new)
    return x2, new


def _run(x, offset, get_state, P, depth):
    B, T, _ = x.shape
    x2 = x.reshape(B * T, D_MODEL)
    news = []
    for l in range(depth):
        x2, nw = _layer(l, x2, B, T, offset, get_state(l), P, P['norm_final_w'] if l == depth - 1 else None)
        news.append(nw)
    return x2.reshape(B, T, D_MODEL), [jnp.stack([nw[i] for nw in news]) for i in range(8)]


def kernel(x_prompt, x_sample, cache_nsa_kv, cache_win_kv, state_mlstm_C, state_mlstm_n, state_mlstm_m, state_mlstm_conv, state_rwkv_S, state_rwkv_shift, page_table, norm_mix_w, w_in, ml_conv_w, ml_conv_b, ml_b_i, ml_b_f, ml_norm_w, rw_mu, rw_w0, rw_w2, rw_a0, rw_a2, rw_g2, rw_k_k, rw_k_a, rw_r_k, rw_ln_w, rw_ln_b, nsa_cmp_pos, nsa_cmp_w1, nsa_cmp_w2, w_out, norm_ffn_w, ffn_w1, ffn_w3, ffn_w2, moe_router, moe_w1, moe_w3, moe_w2, norm_final_w):
    depth = w_in.shape[0]
    Bp = x_prompt.shape[0]
    Bs = x_sample.shape[0]
    n_pool, page = cache_nsa_kv.shape[1], cache_nsa_kv.shape[2]
    past_len = page_table.shape[1] * page
    wbuf = min(WINDOW, past_len)
    P = dict(norm_mix_w=norm_mix_w, w_in=w_in, ml_conv_w=ml_conv_w, ml_conv_b=ml_conv_b, ml_b_i=ml_b_i, ml_b_f=ml_b_f,
             ml_norm_w=ml_norm_w, rw_mu=rw_mu, rw_w0=rw_w0, rw_w2=rw_w2, rw_a0=rw_a0, rw_a2=rw_a2, rw_g2=rw_g2,
             rw_k_k=rw_k_k, rw_k_a=rw_k_a, rw_r_k=rw_r_k, rw_ln_w=rw_ln_w, rw_ln_b=rw_ln_b, nsa_cmp_pos=nsa_cmp_pos,
             nsa_cmp_w1=nsa_cmp_w1, nsa_cmp_w2=nsa_cmp_w2, w_out=w_out, norm_ffn_w=norm_ffn_w, ffn_w1=ffn_w1,
             ffn_w3=ffn_w3, ffn_w2=ffn_w2, moe_router=moe_router, moe_w1=moe_w1, moe_w3=moe_w3, moe_w2=moe_w2,
             norm_final_w=norm_final_w, wbuf=wbuf)

    def prompt_state(l):
        z = lambda *s: jnp.zeros((Bp,) + s, F32)
        return (z(ML_HEADS, HEAD_DIM, HEAD_DIM), z(ML_HEADS, HEAD_DIM), z(ML_HEADS), z(ML_CONV - 1, 2 * ML_WIDTH),
                z(RW_HEADS, HEAD_DIM, HEAD_DIM), z(RW_COLS), None, None, None)

    cache_t = jnp.transpose(cache_nsa_kv, (0, 1, 3, 4, 5, 2)).reshape(depth * n_pool, 4, NSA_KV_HEADS, HEAD_DIM, page)
    wcache_t = jnp.transpose(cache_win_kv, (0, 1, 3, 4, 5, 2)).reshape(depth * Bs, 2, NSA_KV_HEADS, HEAD_DIM, -1)

    def sample_state(l):
        win = dict(key_minor=wcache_t, base=l * Bs)
        return (state_mlstm_C[l], state_mlstm_n[l], state_mlstm_m[l], state_mlstm_conv[l], state_rwkv_S[l],
                state_rwkv_shift[l], cache_t, page_table + l * n_pool, win)

    y_prompt, pst = _run(x_prompt, 0, prompt_state, P, depth)
    y_sample, sst = _run(x_sample, past_len, sample_state, P, depth)
    p_kv, p_win, p_C, p_n, p_m, p_conv, p_S, p_shift = pst
    s_kv, s_win, s_C, s_n, s_m, s_conv, s_S, s_shift = sst
    return (y_prompt, y_sample, p_kv, s_kv, p_win, s_win, p_C, s_C, p_n, s_n, p_m, s_m,
            p_conv, s_conv, p_S, s_S, p_shift, s_shift)
```

```python
import functools
import math

import numpy as np
import jax
import jax.numpy as jnp
from jax import lax
from jax.experimental import pallas as pl
from jax.experimental.pallas import tpu as pltpu

F32 = jnp.float32
BF16 = jnp.bfloat16
HIGHEST = lax.Precision.HIGHEST

D_MODEL = 1024
HEAD_DIM = 64
ML_HEADS = 4
RW_HEADS = 4
NSA_HEADS = 8
NSA_KV_HEADS = 2
NSA_GROUP = NSA_HEADS // NSA_KV_HEADS
ML_WIDTH = ML_HEADS * HEAD_DIM
RW_WIDTH = RW_HEADS * HEAD_DIM
NSA_WIDTH = NSA_HEADS * HEAD_DIM
KV_WIDTH = NSA_KV_HEADS * HEAD_DIM
MIX_WIDTH = ML_WIDTH + RW_WIDTH + NSA_WIDTH
ML_CHUNK = 64
ML_CONV = 4
ML_NORM_EPS = 1e-6
RW_LORA_W = 64
RW_LORA_A = 64
RW_LORA_G = 128
RW_GN_EPS = 64e-5
RW_CHUNK = 64
CMP_LEN = 32
CMP_STRIDE = 16
CMP_HID = 128
SEL_BLOCK = 64
SEL_TOPN = 8
WINDOW = 512
Q_BLOCK = 64
ROPE_THETA = 10000.0
N_EXPERTS = 8
TOP_K = 2
RMS_EPS = 1e-6
NEG = -1e30
SEL_NEG = -1e9
FORCE_BONUS = 1e3
ML_COLS = 4 * ML_WIDTH + 2 * ML_HEADS
RW_COLS = 3 * RW_WIDTH + RW_LORA_W + RW_LORA_A + RW_LORA_G
NSA_COLS = NSA_WIDTH + 6 * KV_WIDTH + 3 * NSA_HEADS

LANE = 128
KV_ROW = 4 * KV_WIDTH
GRP_ROWS = CMP_STRIDE
VMEM_LIMIT = 52 * 1024 * 1024

C_ML = 0
C_RW = 1024
C_Q = 2048
C_KV = 2560
C_WK = 3072
C_WV = 3200
C_SM = 3328
C_TOT = 3456


def _cp(sem, vmem=VMEM_LIMIT):
    return pltpu.CompilerParams(dimension_semantics=sem, vmem_limit_bytes=vmem)


def _dot(a, b, precision=None):
    return jnp.dot(a, b, preferred_element_type=F32, precision=precision)


def _dot_nt(a, b, precision=None):
    return lax.dot_general(a, b, (((1,), (1,)), ((), ())), preferred_element_type=F32, precision=precision)


def _dot_tn(a, b, precision=None):
    return lax.dot_general(a, b, (((0,), (0,)), ((), ())), preferred_element_type=F32, precision=precision)


def _sigmoid(x):
    return 1.0 / (1.0 + jnp.exp(-x))


def _silu(x):
    return x * _sigmoid(x)


def _iota(shape, dim):
    return lax.broadcasted_iota(jnp.int32, shape, dim)


def _head_ones(w):
    return jnp.where(_iota((w, w), 0) // HEAD_DIM == _iota((w, w), 1) // HEAD_DIM, 1.0, 0.0).astype(BF16)


def _head_sum(x, ones):
    hi = x.astype(BF16)
    lo = (x - hi.astype(F32)).astype(BF16)
    return _dot(hi, ones) + _dot(lo, ones)


def _masked_softmax(s, mask):
    s = jnp.where(mask, s, NEG)
    m = jnp.max(s, axis=-1, keepdims=True)
    p = jnp.where(mask, jnp.exp(s - m), 0.0)
    return p / jnp.maximum(jnp.sum(p, -1, keepdims=True), 1e-30)


def _rw_order():
    r0, wl0 = 0, RW_WIDTH
    k0 = wl0 + RW_LORA_W
    v0 = k0 + RW_WIDTH
    al0 = v0 + RW_WIDTH
    gl0 = al0 + RW_LORA_A
    return np.concatenate([np.arange(r0, r0 + RW_WIDTH), np.arange(k0, k0 + RW_WIDTH), np.arange(v0, v0 + RW_WIDTH),
                           np.arange(wl0, wl0 + RW_LORA_W), np.arange(al0, al0 + RW_LORA_A),
                           np.arange(gl0, gl0 + RW_LORA_G)])


def _runs(order):
    cuts = np.flatnonzero(np.diff(order) != 1) + 1
    return [(int(seg[0]), int(seg[-1]) + 1) for seg in np.split(np.asarray(order), cuts)]


def _take_cols(x, order):
    return jnp.concatenate([x[..., lo:hi] for lo, hi in _runs(order)], axis=-1)


def _pack_w_in(w):
    ml0, rw0, ns0 = 0, ML_COLS, ML_COLS + RW_COLS
    gt0 = ns0 + NSA_WIDTH + 6 * KV_WIDTH
    rw_runs = [(rw0 + lo, rw0 + hi) for lo, hi in _runs(_rw_order())]
    runs = [(ml0, ml0 + 4 * ML_WIDTH)] + rw_runs + [(ns0, gt0), (ml0 + 4 * ML_WIDTH, ml0 + ML_COLS), (gt0, gt0 + 3 * NSA_HEADS)]
    width = sum(hi - lo for lo, hi in runs)
    assert C_SM <= width <= C_TOT
    pieces = [w[:, lo:hi] for lo, hi in runs] + [jnp.zeros((w.shape[0], C_TOT - width), w.dtype)]
    return jnp.concatenate(pieces, axis=1).astype(BF16)


def _swap_rotary_halves(x):
    half = HEAD_DIM // 2
    w = x.shape[-1]
    first = _iota((1, w), 1) % HEAD_DIM < half
    return jnp.where(first, pltpu.roll(x, w - half, 1), pltpu.roll(x, half, 1))


def _rope_tables(pos):
    half = HEAD_DIM // 2
    inv = ROPE_THETA ** (-jnp.arange(half, dtype=F32) / half)
    ang = pos.astype(F32)[:, None] * inv[None, :]
    cos, sin = jnp.cos(ang), jnp.sin(ang)
    cos2 = jnp.concatenate([cos, cos, cos, cos], -1)
    sin2 = jnp.concatenate([-sin, sin, -sin, sin], -1)
    return cos2, sin2


def _inproj_kernel(x_ref, nw_ref, w_ref, cos_ref, sin_ref, ml_ref, rw_ref, q_ref, kv_ref, win_ref, sm_ref, ck_ref,
                   cv_ref):
    x = x_ref[...]
    xn = x * lax.rsqrt(jnp.mean(x * x, axis=-1, keepdims=True) + RMS_EPS) * nw_ref[...]
    acc = _dot(xn.astype(BF16), w_ref[...])
    cos = cos_ref[...]
    sin = sin_ref[...]
    cos4 = jnp.concatenate([cos] * 4, axis=-1)
    sin4 = jnp.concatenate([sin] * 4, axis=-1)
    ml_ref[...] = acc[:, C_ML:C_ML + 1024]
    rw_ref[...] = acc[:, C_RW:C_RW + 1024]
    rope = lambda v, c, s: v * c + _swap_rotary_halves(v) * s
    q = acc[:, C_Q:C_Q + NSA_WIDTH]
    q_ref[:, 0:NSA_WIDTH] = q
    q_ref[:, NSA_WIDTH:2 * NSA_WIDTH] = rope(q, cos4, sin4)
    kv_ref[:, 0:2 * KV_WIDTH] = acc[:, C_KV:C_KV + 2 * KV_WIDTH]
    kv_ref[:, 2 * KV_WIDTH:3 * KV_WIDTH] = rope(acc[:, C_KV + 2 * KV_WIDTH:C_KV + 3 * KV_WIDTH], cos, sin)
    kv_ref[:, 3 * KV_WIDTH:4 * KV_WIDTH] = acc[:, C_KV + 3 * KV_WIDTH:C_KV + 4 * KV_WIDTH]
    win_ref[:, 0:KV_WIDTH] = rope(acc[:, C_WK:C_WK + KV_WIDTH], cos, sin)
    win_ref[:, KV_WIDTH:2 * KV_WIDTH] = acc[:, C_WV:C_WV + KV_WIDTH]
    sm_ref[...] = acc[:, C_SM:C_SM + LANE]
    ck_ref[...] = acc[:, C_KV:C_KV + KV_WIDTH]
    cv_ref[...] = acc[:, C_KV + KV_WIDTH:C_KV + 2 * KV_WIDTH]


def _inproj(x2, norm_w, w_packed, cos2, sin2, T):
    N = x2.shape[0]
    tm = min(256, N)
    if T >= tm:
        assert T % tm == 0
        nt = T // tm
    else:
        assert tm % T == 0
        cos2 = jnp.tile(cos2, (tm // T, 1))
        sin2 = jnp.tile(sin2, (tm // T, 1))
        nt = 1
    row = lambda w: pl.BlockSpec((tm, w), lambda i: (i, 0))
    tab = pl.BlockSpec((tm, LANE), lambda i: (i % nt, 0))
    widths = (1024, 1024, 2 * NSA_WIDTH, KV_ROW, 2 * KV_WIDTH, LANE, KV_WIDTH, KV_WIDTH)
    return pl.pallas_call(
        _inproj_kernel,
        grid=(N // tm,),
        in_specs=[row(D_MODEL), pl.BlockSpec((1, D_MODEL), lambda i: (0, 0)),
                  pl.BlockSpec((D_MODEL, C_TOT), lambda i: (0, 0)), tab, tab],
        out_specs=[row(w) for w in widths],
        out_shape=[jax.ShapeDtypeStruct((N, w), F32) for w in widths],
        compiler_params=_cp(("parallel",)),
        name="inproj",
    )(x2, norm_w.reshape(1, D_MODEL), w_packed, cos2, sin2)


def _mlstm_kernel(ml_ref, sm_ref, conv0_ref, c0_ref, n0_ref, m0_ref, cw_ref, cb_ref, gb_ref, nw_ref,
                  h_ref, c_ref, n_ref, m_ref, conv_ref, xp_ref, *, Bb, L):
    c = pl.program_id(1)
    H, D = ML_HEADS, HEAD_DIM
    PAD = 8
    KC = ML_CONV - 1

    @pl.when(c == 0)
    def _():
        c_ref[...] = c0_ref[...]
        n_ref[...] = n0_ref[...]
        m_ref[...] = m0_ref[...]
        xp_ref[:, PAD - KC:PAD, :] = conv0_ref[...]

    tril = (_iota((L, L), 1) <= _iota((L, L), 0))
    eye = (_iota((L, L), 1) == _iota((L, L), 0))
    tril_b = jnp.where(tril, 1.0, 0.0).astype(BF16)

    bs = [dict(b=b) for b in range(Bb)]
    for s in bs:
        b = s['b']
        xp_ref[b, PAD:PAD + L, :] = ml_ref[b, :, 0:2 * ML_WIDTH]
    for s in bs:
        b = s['b']
        y = jnp.broadcast_to(cb_ref[...], (L, 2 * ML_WIDTH))
        for j in range(ML_CONV):
            y = y + xp_ref[b, PAD - KC + j:PAD - KC + j + L, :] * cw_ref[j:j + 1, :]
        s['qk'] = _silu(y)
        gates = sm_ref[b] + gb_ref[...]
        lf = jnp.minimum(gates, 0.0) - jnp.log1p(jnp.exp(-jnp.abs(gates)))
        lf_hi = lf.astype(BF16)
        lf_lo = (lf - lf_hi.astype(F32)).astype(BF16)
        s['gates'] = gates
        s['bcum'] = _dot(tril_b, lf_hi) + _dot(tril_b, lf_lo)
    for s in bs:
        b = s['b']
        tail = xp_ref[b, PAD + L - KC:PAD + L, :]
        xp_ref[b, PAD - KC:PAD, :] = tail
        conv_ref[b] = tail
    for s in bs:
        s['b_all'] = pltpu.roll(s['bcum'], LANE - H, 1)
        s['a_all'] = s['gates'] - s['b_all']
        s['cmax'] = s['a_all']
    step = 1
    while step < L:
        for s in bs:
            s['cmax'] = jnp.where(_iota((L, 1), 0) >= step, jnp.maximum(s['cmax'], pltpu.roll(s['cmax'], step, 0)),
                                  s['cmax'])
        step *= 2
    units = []
    for s in bs:
        b, b_all, a_all = s['b'], s['b_all'], s['a_all']
        m_prev = m_ref[b]
        m_t = b_all + jnp.maximum(m_prev, s['cmax'])
        m_new = m_t[L - 1:L, :]
        b_last = b_all[L - 1:L, :]
        m_ref[b] = m_new
        inter = jnp.exp(b_all + m_prev - m_t)
        wgt = jnp.exp(b_last + a_all - m_new)
        dec = jnp.exp(b_last + m_prev - m_new)
        ccol = b_all - m_t
        einv = jnp.exp(-m_t)
        qk = s['qk']
        for h in range(H):
            lo = h * D
            col = lambda x: x[:, h:h + 1]
            units.append(dict(b=b, h=h, lo=lo, q=qk[:, lo:lo + D], k=qk[:, ML_WIDTH + lo:ML_WIDTH + lo + D] * (D ** -0.5),
                              v=ml_ref[b, :, 2 * ML_WIDTH + lo:2 * ML_WIDTH + lo + D], acol=col(a_all),
                              ccol=col(ccol), inter=col(inter), wgt=col(wgt), dec=col(dec), einv=col(einv)))

    ones_col = jnp.ones((L, D), BF16)
    for u in units:
        u['arow'] = jnp.sum(jnp.where(eye, u['acol'], 0.0), axis=0, keepdims=True)
    for u in units:
        u['dm'] = jnp.exp(jnp.where(tril, u['ccol'] + u['arow'], -jnp.inf))
    for u in units:
        C = c_ref[u['b'], u['h']]
        qb, kb = u['q'].astype(BF16), u['k'].astype(BF16)
        if 2 * L == LANE:
            m1 = _dot_nt(qb, jnp.concatenate([kb, C.astype(BF16)], axis=0))
            u['s_raw'], u['qc'] = m1[:, 0:L], m1[:, L:]
        else:
            u['s_raw'], u['qc'] = _dot_nt(qb, kb), _dot_nt(qb, C.astype(BF16))
        c_ref[u['b'], u['h']] = u['dec'] * C + _dot_tn((u['v'] * u['wgt']).astype(BF16), kb)
        nrow = n_ref[u['b'], u['h']:u['h'] + 1, :]
        u['qn'] = jnp.sum(u['q'] * nrow, axis=-1, keepdims=True)
        n_ref[u['b'], u['h']:u['h'] + 1, :] = u['dec'] * nrow + jnp.sum(u['wgt'] * u['k'], axis=0, keepdims=True)
    for u in units:
        S = (u['s_raw'] * u['dm']).astype(BF16)
        sv = _dot(S, jnp.concatenate([u['v'].astype(BF16), ones_col], axis=1))
        den = u['inter'] * u['qn'] + sv[:, D:D + 1]
        u['hh'] = (u['inter'] * u['qc'] + sv[:, 0:D]) / jnp.maximum(jnp.abs(den), u['einv'])
    ones = _head_ones(ML_WIDTH).astype(F32) * (1.0 / D)
    for s in bs:
        s['hh'] = jnp.concatenate([u['hh'] for u in units if u['b'] == s['b']], axis=-1)
        s['mu'] = _dot(s['hh'], ones, HIGHEST)
    for s in bs:
        s['var'] = _dot(jnp.square(s['hh'] - s['mu']), ones, HIGHEST)
    for s in bs:
        b = s['b']
        hn = (s['hh'] - s['mu']) * lax.rsqrt(s['var'] + ML_NORM_EPS) * nw_ref[...]
        h_ref[b] = hn * _sigmoid(ml_ref[b, :, 3 * ML_WIDTH:4 * ML_WIDTH])


def _mlstm(ml, sm, conv0, C0, n0, m0, conv_w, conv_b, b_i, b_f, norm_w, B, T):
    L = math.gcd(T, ML_CHUNK)
    NC = T // L
    N = B * T
    Bb = 8 if B % 8 == 0 else B
    gate_bias = jnp.concatenate([b_i, b_f, jnp.zeros((LANE - 2 * ML_HEADS,), F32)]).reshape(1, LANE)
    m0p = jnp.pad(m0.reshape(B, 1, ML_HEADS), ((0, 0), (0, 0), (0, LANE - ML_HEADS)))
    per_b = lambda shape: pl.BlockSpec((Bb,) + shape, lambda b, c: (b,) + (0,) * len(shape))
    const = lambda shape: pl.BlockSpec(shape, lambda b, c: (0,) * len(shape))
    rows = lambda w: pl.BlockSpec((Bb, L, w), lambda b, c: (b, c, 0))
    state_shapes = [(ML_HEADS, HEAD_DIM, HEAD_DIM), (ML_HEADS, HEAD_DIM), (1, LANE), (ML_CONV - 1, 2 * ML_WIDTH)]
    h, C, n, m, conv = pl.pallas_call(
        functools.partial(_mlstm_kernel, Bb=Bb, L=L),
        grid=(B // Bb, NC),
        in_specs=[rows(1024), rows(LANE), per_b(state_shapes[3]), per_b(state_shapes[0]), per_b(state_shapes[1]),
                  per_b(state_shapes[2]), const((ML_CONV, 2 * ML_WIDTH)), const((1, 2 * ML_WIDTH)),
                  const((1, LANE)), const((1, ML_WIDTH))],
        out_specs=[rows(ML_WIDTH)] + [per_b(s) for s in state_shapes],
        out_shape=[jax.ShapeDtypeStruct((B, T, ML_WIDTH), F32)]
                  + [jax.ShapeDtypeStruct((B,) + s, F32) for s in state_shapes],
        scratch_shapes=[pltpu.VMEM((Bb, L + 8, 2 * ML_WIDTH), F32)],
        compiler_params=_cp(("parallel", "arbitrary")),
        name="mlstm",
    )(ml.reshape(B, T, 1024), sm.reshape(B, T, LANE), conv0, C0, n0, m0p, conv_w, conv_b.reshape(1, -1), gate_bias,
      norm_w.reshape(1, -1))
    return h.reshape(N, ML_WIDTH), C, n, m[:, 0, :ML_HEADS], conv


def _rwkv_prep(rw_ref, shift0_ref, mu_ref, w0_ref, w2_ref, a0_ref, a2_ref, g2_ref, kk_ref, ka_ref, rk_ref,
               gb_ref, shift_ref, carry_ref, *, Bb, tt):
    c = pl.program_id(1)
    W = RW_WIDTH
    C = rw_ref.shape[-1]

    @pl.when(c == 0)
    def _():
        carry_ref[...] = shift0_ref[...]

    x = rw_ref[...].reshape(Bb * tt, C)
    carry_rows = jnp.broadcast_to(carry_ref[...], (Bb, tt, C)).reshape(Bb * tt, C)
    prev = jnp.where(_iota((Bb * tt, 1), 0) % tt == 0, carry_rows, pltpu.roll(x, 1, 0))
    last = rw_ref[:, tt - 1:tt, :]
    carry_ref[...] = last
    shift_ref[...] = last
    xm = x + (prev - x) * mu_ref[...]
    r = xm[:, 0:W]
    k = xm[:, W:2 * W]
    v = xm[:, 2 * W:3 * W]
    wl = xm[:, 3 * W:3 * W + RW_LORA_W]
    al = xm[:, 3 * W + RW_LORA_W:3 * W + RW_LORA_W + RW_LORA_A]
    gl = xm[:, 3 * W + RW_LORA_W + RW_LORA_A:]
    wpre = w0_ref[...] + _dot(jnp.tanh(wl), w2_ref[...], HIGHEST)
    softplus_neg = jnp.maximum(-wpre, 0.0) + jnp.log1p(jnp.exp(-jnp.abs(wpre)))
    log_decay = -jnp.exp(-softplus_neg - 0.5)
    a = _sigmoid(a0_ref[...] + _dot(al, a2_ref[...], HIGHEST))
    g = _dot(_sigmoid(gl), g2_ref[...], HIGHEST)
    ones = _head_ones(W)
    kk = k * kk_ref[...]
    kk = kk / jnp.maximum(jnp.sqrt(_head_sum(kk * kk, ones)), 1e-12)
    k2 = k * (1.0 + (a - 1.0) * ka_ref[...])
    bonus = _head_sum(r * k2 * rk_ref[...], ones) * v
    gb_ref[...] = jnp.concatenate([g, bonus], axis=-1).reshape(Bb, tt, 2 * W)
    return r, log_decay, k2, v, kk, a * kk


def _rwkv_kernel(rw_ref, shift0_ref, mu_ref, w0_ref, w2_ref, a0_ref, a2_ref, g2_ref, kk_ref, ka_ref, rk_ref, s0_ref,
                 o_ref, gb_ref, shift_ref, s_ref, carry_ref, *, Bb, L):
    c = pl.program_id(1)
    W, D = RW_WIDTH, HEAD_DIM

    @pl.when(c == 0)
    def _():
        s_ref[...] = s0_ref[...]

    row, col = _iota((L, L), 0), _iota((L, L), 1)
    incl = col <= row
    strict = col < row
    tril_b = jnp.where(incl, 1.0, 0.0).astype(BF16)
    mm = lambda a, b: _dot(a.astype(BF16), b.astype(BF16))
    mm_nt = lambda a, b: _dot_nt(a.astype(BF16), b.astype(BF16))
    mm_tn = lambda a, b: _dot_tn(a.astype(BF16), b.astype(BF16))

    prepped = _rwkv_prep(rw_ref, shift0_ref, mu_ref, w0_ref, w2_ref, a0_ref, a2_ref, g2_ref, kk_ref, ka_ref, rk_ref,
                         gb_ref, shift_ref, carry_ref, Bb=Bb, tt=L)
    units = []
    for b in range(Bb):
        r, logw, k, v, kk, bb = (x[b * L:(b + 1) * L] for x in prepped)
        logw_hi = logw.astype(BF16)
        logw_lo = (logw - logw_hi.astype(F32)).astype(BF16)
        cum = _dot(tril_b, logw_hi) + _dot(tril_b, logw_lo)
        g = jnp.exp(cum)
        ginv = jnp.exp(-cum)
        khat = kk * jnp.exp(cum - logw)
        rhat = r * g
        kd = k * ginv
        bd = bb * ginv
        glast = g[L - 1:L, :]
        for h in range(RW_HEADS):
            sl = slice(h * D, (h + 1) * D)
            units.append(dict(b=b, h=h, sl=sl, khat=khat[:, sl].astype(BF16), rhat=rhat[:, sl].astype(BF16),
                              kd=kd[:, sl], bd=bd[:, sl], v=v[:, sl].astype(BF16), gl=glast[:, sl]))

    for u in units:
        S0 = s_ref[u['b'], u['h']]
        kdb, bdb = u['kd'].astype(BF16), u['bd'].astype(BF16)
        if 2 * L == LANE:
            lhs = jnp.concatenate([u['khat'], u['rhat']], axis=0)
            m1 = _dot_nt(lhs, jnp.concatenate([bdb, kdb], axis=0))
            m2 = _dot_nt(lhs, S0.astype(BF16))
            q_raw, p_raw, aub_raw, avk_raw = m1[0:L, 0:L], m1[0:L, L:], m1[L:, 0:L], m1[L:, L:]
            ks0, u['rs0'] = m2[0:L], m2[L:]
        else:
            q_raw, p_raw = _dot_nt(u['khat'], bdb), _dot_nt(u['khat'], kdb)
            aub_raw, avk_raw = _dot_nt(u['rhat'], bdb), _dot_nt(u['rhat'], kdb)
            ks0, u['rs0'] = _dot_nt(u['khat'], S0.astype(BF16)), _dot_nt(u['rhat'], S0.astype(BF16))
        u['Q'] = jnp.where(strict, q_raw, 0.0).astype(BF16)
        u['P'] = jnp.where(strict, p_raw, 0.0).astype(BF16)
        u['avk'] = jnp.where(incl, avk_raw, 0.0).astype(BF16)
        u['aub'] = jnp.where(incl, aub_raw, 0.0).astype(BF16)
        u['X'] = ks0
    for u in units:
        u['X'] = u['X'] + _dot(u['P'], u['v'])
    for u in units:
        u['X'] = u['X'] - mm(u['Q'], u['X'])
    n = 2
    while n < L:
        for u in units:
            u['Q'] = _dot(u['Q'], u['Q']).astype(BF16)
        for u in units:
            u['X'] = u['X'] + mm(u['Q'], u['X'])
        n *= 2
    for u in units:
        xb = u['X'].astype(BF16)
        o_ref[u['b'], :, u['sl']] = u['rs0'] + _dot(u['avk'], u['v']) - _dot(u['aub'], xb)
        gl = u['gl']
        S0 = s_ref[u['b'], u['h']]
        s_ref[u['b'], u['h']] = S0 * gl + _dot_tn(u['v'], (u['kd'] * gl).astype(BF16)) - _dot_tn(xb, (u['bd'] * gl).astype(BF16))


def _rwkv(rw, shift0p, S0, mu_p, w0, w2, a0, a2, g2, k_k, k_a, r_k, B, T):
    N = B * T
    W = RW_WIDTH
    Bb = 8 if B % 8 == 0 else B
    L = math.gcd(T, RW_CHUNK)
    const = lambda shape: pl.BlockSpec(shape, lambda b, c: (0,) * len(shape))
    blk = lambda w: pl.BlockSpec((Bb, L, w), lambda b, c: (b, c, 0))
    per_seq = pl.BlockSpec((Bb, 1, 1024), lambda b, c: (b, 0, 0))
    st = pl.BlockSpec((Bb, RW_HEADS, HEAD_DIM, HEAD_DIM), lambda b, c: (b, 0, 0, 0))
    vec = lambda a: a.reshape(1, -1)
    y, gb, shift, S = pl.pallas_call(
        functools.partial(_rwkv_kernel, Bb=Bb, L=L),
        grid=(B // Bb, T // L),
        in_specs=[blk(1024), per_seq, const((1, 1024)),
                  const((1, W)), const((RW_LORA_W, W)), const((1, W)), const((RW_LORA_A, W)), const((RW_LORA_G, W)),
                  const((1, W)), const((1, W)), const((1, W)), st],
        out_specs=[blk(W), blk(2 * W), per_seq, st],
        out_shape=[jax.ShapeDtypeStruct((B, T, W), F32), jax.ShapeDtypeStruct((B, T, 2 * W), F32),
                   jax.ShapeDtypeStruct((B, 1, 1024), F32), jax.ShapeDtypeStruct(S0.shape, F32)],
        scratch_shapes=[pltpu.VMEM((Bb, 1, 1024), F32)],
        compiler_params=_cp(("parallel", "arbitrary")),
        name="rwkv",
    )(rw.reshape(B, T, 1024), shift0p.reshape(B, 1, 1024), vec(mu_p), vec(w0), w2, vec(a0), a2, g2, vec(k_k),
      vec(k_a), vec(r_k), S0)
    return y.reshape(N, W), gb.reshape(N, 2 * W), S, shift[:, 0]


def _compress_slab(xg, j, posl_ref, posh_ref, wl_ref, wh_ref, w2_ref):
    G = xg.shape[0]
    lo = _dot((xg + posl_ref[j]).astype(BF16), wl_ref[j])
    hi = _dot((xg + posh_ref[j]).astype(BF16), wh_ref[j])
    hid = _silu(lo + pltpu.roll(hi, G - 1, 0))
    return _dot(hid.astype(BF16), w2_ref[j])


def _compress_kernel(xk_ref, xv_ref, posl_ref, posh_ref, wl_ref, wh_ref, w2_ref, o_ref, *, G):
    for j, x_ref in enumerate((xk_ref, xv_ref)):
        xg = jnp.concatenate([x_ref.at[0][pl.ds(r, G, stride=GRP_ROWS), :] for r in range(GRP_ROWS)], axis=-1)
        o_ref[0, :, j * KV_WIDTH:(j + 1) * KV_WIDTH] = _compress_slab(xg, j, posl_ref, posh_ref, wl_ref, wh_ref, w2_ref)


def _pack_cmp_weights(cmp_pos, cmp_w1, cmp_w2):
    eye = jnp.eye(NSA_KV_HEADS, dtype=F32)
    w1 = cmp_w1.reshape(2, CMP_LEN, HEAD_DIM, CMP_HID)
    wbig = jnp.einsum('jrde,bg->jrbdge', w1, eye).reshape(2, CMP_LEN, KV_WIDTH, NSA_KV_HEADS * CMP_HID).astype(BF16)
    w_lo = wbig[:, :GRP_ROWS].reshape(2, GRP_ROWS * KV_WIDTH, NSA_KV_HEADS * CMP_HID)
    w_hi = wbig[:, GRP_ROWS:].reshape(2, GRP_ROWS * KV_WIDTH, NSA_KV_HEADS * CMP_HID)
    w2big = jnp.einsum('jed,bg->jbegd', cmp_w2, eye).reshape(2, NSA_KV_HEADS * CMP_HID, KV_WIDTH).astype(BF16)
    pos = jnp.broadcast_to(cmp_pos[:, :, None, :], (2, CMP_LEN, NSA_KV_HEADS, HEAD_DIM))
    pos_lo = pos[:, :GRP_ROWS].reshape(2, 1, -1)
    pos_hi = pos[:, GRP_ROWS:].reshape(2, 1, -1)
    return w_lo, w_hi, w2big, pos_lo, pos_hi


_CMP_K = GRP_ROWS * KV_WIDTH
_CMP_SPECS = [(2, 1, _CMP_K), (2, 1, _CMP_K), (2, _CMP_K, NSA_KV_HEADS * CMP_HID), (2, _CMP_K, NSA_KV_HEADS * CMP_HID),
              (2, NSA_KV_HEADS * CMP_HID, KV_WIDTH)]


def _overlap_map(n_rows, n_cmp, n_sel):
    cs = np.arange(n_rows) * CMP_STRIDE
    ss = np.arange(LANE) * SEL_BLOCK
    ov = np.clip(np.minimum(cs[:, None] + CMP_LEN, ss[None, :] + SEL_BLOCK) - np.maximum(cs[:, None], ss[None, :]),
                 0, None) / CMP_LEN
    ov[n_cmp:, :] = 0.0
    ov[:, n_sel:] = 0.0
    return ov


def _compress(xk, xv, cmp_pos, cmp_w1, cmp_w2, B, G):
    rows = xk.shape[1]
    w_lo, w_hi, w2big, pos_lo, pos_hi = _pack_cmp_weights(cmp_pos, cmp_w1, cmp_w2)
    const = lambda shape: pl.BlockSpec(shape, lambda b: (0,) * len(shape))
    slab = pl.BlockSpec((1, rows, KV_WIDTH), lambda b: (b, 0, 0))
    return pl.pallas_call(
        functools.partial(_compress_kernel, G=G),
        grid=(B,),
        in_specs=[slab, slab] + [const(s) for s in _CMP_SPECS],
        out_specs=pl.BlockSpec((1, G, 2 * KV_WIDTH), lambda b: (b, 0, 0)),
        out_shape=jax.ShapeDtypeStruct((B, G, 2 * KV_WIDTH), F32),
        compiler_params=_cp(("parallel",)),
        name="nsa_compress",
    )(xk, xv, pos_lo, pos_hi, w_lo, w_hi, w2big)


def _softmax_pv(s, bias, v_aug, *, QB, v_keys_minor):
    G, D = NSA_GROUP, HEAD_DIM
    K = s.shape[-1]
    s3 = s.reshape(G, QB, K) + bias[None]
    p = jnp.exp((s3 - jnp.max(s3, axis=-1, keepdims=True)).astype(BF16)).reshape(G * QB, K)
    oa = _dot_nt(p, v_aug) if v_keys_minor else _dot(p, v_aug)
    return oa[:, 0:D] / jnp.maximum(oa[:, D:D + 1], 1e-30)


def _cmp_select(units, ov, QB, NS, n_rank):
    G = NSA_GROUP
    blk = _iota((1, LANE), 1)
    for u in units:
        u['s_c'] = _dot_nt(u['q_raw'], u['ck'].astype(BF16))
    for u in units:
        cmp_end = _iota((1, u['ck'].shape[0]), 1) * CMP_STRIDE + (CMP_LEN - 1)
        u['p_c'] = _masked_softmax(u['s_c'], cmp_end <= u['tq_r'])
    for u in units:
        p_c = u['p_c']
        u['o_c'] = _dot(p_c.astype(BF16), u['cv'].astype(BF16))
        psum = p_c[0:QB]
        for h in range(1, G):
            psum = psum + p_c[h * QB:(h + 1) * QB]
        imp = _dot(psum, ov, HIGHEST)
        tq = u['tq']
        cur = tq // SEL_BLOCK
        avail = blk * SEL_BLOCK <= tq
        forced = (blk == 0) | (blk == cur) | (blk == cur - 1)
        u['score'] = jnp.where(avail, imp + jnp.where(forced, FORCE_BONUS, 0.0), SEL_NEG)
        u['cnt'] = jnp.zeros((QB, LANE), F32)
    for i in range(n_rank):
        for u in units:
            score = u['score']
            col = score[:, i:i + 1]
            u['cnt'] = u['cnt'] + jnp.where((col > score) | ((col >= score) & (blk > i)), 1.0, 0.0)
    for u in units:
        u['sel'] = ((u['cnt'] < min(SEL_TOPN, NS)) & (u['score'] > 0.5 * SEL_NEG)).astype(BF16)


def _head_stack(x, base, g):
    G, D = NSA_GROUP, HEAD_DIM
    return jnp.concatenate([x[:, base + (g * G + h) * D:base + (g * G + h + 1) * D] for h in range(G)], 0)


def _gate_stack(gates, g, j):
    c0 = 2 * ML_HEADS + 3 * g * NSA_GROUP + j
    return jnp.concatenate([gates[:, c0 + 3 * h:c0 + 3 * h + 1] for h in range(NSA_GROUP)], 0)


def _softmax_rows(s_ref, b_ref, p_ref, K, QB, R):
    s3 = s_ref[:, 0:K].reshape(R // QB, QB, K) + b_ref[:, 0:K][None]
    p_ref[:, 0:K] = jnp.exp((s3 - jnp.max(s3, axis=-1, keepdims=True)).astype(BF16)).reshape(R, K)


def _pv_normalised(p, v_aug):
    D = HEAD_DIM
    oa = _dot(p, v_aug)
    return oa[:, 0:D] / jnp.maximum(oa[:, D:D + 1], 1e-30)


def _nsa_kernel(q_ref, sm_ref, ckv_ref, kv_ref, win_ref, ov_ref, ex_ref, o_ref, skb, svb, wkb, wvb,
                s_scr, p_scr, bs_scr, sw_scr, pw_scr, bw_scr, *, QB, NQ, NB, NS, P, WN, nvar):
    nb = pl.program_id(1)
    D = HEAD_DIM
    G = NSA_GROUP
    R = G * QB
    LW = wkb.shape[1]

    @pl.when(nb == 0)
    def _():
        for g in range(NSA_KV_HEADS):
            skb[g] = kv_ref[0, :, g * D:(g + 1) * D].astype(BF16)
            svb[g, :, 0:D] = kv_ref[0, :, KV_WIDTH + g * D:KV_WIDTH + (g + 1) * D].astype(BF16)
            svb[g, :, D:2 * D] = jnp.ones((P, D), BF16)
            wkb[g] = win_ref[0, :, g * D:(g + 1) * D].astype(BF16)
            wvb[g, :, 0:D] = win_ref[0, :, KV_WIDTH + g * D:KV_WIDTH + (g + 1) * D].astype(BF16)
            wvb[g, :, D:2 * D] = jnp.ones((LW, D), BF16)

    def body(PS):
        kpos = _iota((1, PS), 1)
        units = []
        for qi in range(NQ):
            rows = slice(qi * QB, (qi + 1) * QB)
            qs = (nb * NQ + qi) * QB
            tq = qs + _iota((QB, 1), 0)
            tq_r = qs + (_iota((R, 1), 0) % QB)
            gates = _sigmoid(sm_ref[rows, :])
            wstart = pl.multiple_of(jnp.clip(qs - WINDOW, 0, LW - WN), 16)
            dlt = tq - (wstart + _iota((1, WN), 1))
            bw_scr[qi] = jnp.where((dlt >= 0) & (dlt < WINDOW), 0.0, NEG)
            qv = q_ref[rows, :] * (D ** -0.5)
            for g in range(NSA_KV_HEADS):
                units.append(dict(i=qi * NSA_KV_HEADS + g, qi=qi, g=g, tq=tq, tq_r=tq_r, gates=gates, wstart=wstart,
                                  q_raw=_head_stack(qv, 0, g).astype(BF16),
                                  q_rot=_head_stack(qv, NSA_WIDTH, g).astype(BF16), ck=ckv_ref[0, :, g * D:(g + 1) * D],
                                  cv=ckv_ref[0, :, KV_WIDTH + g * D:KV_WIDTH + (g + 1) * D]))
        for u in units:
            i, g = u['i'], u['g']
            s_scr[i, :, 0:PS] = _dot_nt(u['q_rot'], skb[g, 0:PS, :])
            sw_scr[i] = _dot_nt(u['q_rot'], wkb[g, pl.ds(u['wstart'], WN), :])
        for u in units:
            i, g = u['i'], u['g']
            _softmax_rows(sw_scr.at[i], bw_scr.at[u['qi']], pw_scr.at[i], WN, QB, R)
            u['o_w'] = _pv_normalised(pw_scr[i], wvb[g, pl.ds(u['wstart'], WN), :])
        _cmp_select(units, ov_ref[...], QB, NS, min(NS, PS // SEL_BLOCK))
        for u in units:
            selk = _dot(u['sel'], ex_ref[:, 0:PS])
            bs_scr[u['i'], :, 0:PS] = jnp.where((selk > 0.5) & (kpos <= u['tq']), 0.0, NEG)
        for u in units:
            i, g, gates = u['i'], u['g'], u['gates']
            _softmax_rows(s_scr.at[i], bs_scr.at[i], p_scr.at[i], PS, QB, R)
            o_s = _pv_normalised(p_scr[i, :, 0:PS], svb[g, 0:PS, :])
            u['o'] = _gate_stack(gates, g, 0) * u['o_c'] + _gate_stack(gates, g, 1) * o_s + _gate_stack(gates, g, 2) * u['o_w']
        for qi in range(NQ):
            outs = [u['o'][h * QB:(h + 1) * QB] for u in units if u['qi'] == qi for h in range(G)]
            o_ref[qi * QB:(qi + 1) * QB, :] = jnp.concatenate(outs, axis=-1)

    for v in range(nvar):
        pl.when((nb * NQ + NQ - 1) // (NB // nvar) == v)(functools.partial(body, P * (v + 1) // nvar))


def _nsa_attend(qq, sm, ckv, kv_rows, win_rows, B, T):
    N = B * T
    QB = math.gcd(T, Q_BLOCK)
    NB = T // QB
    P = LW = T
    NS = -(-T // SEL_BLOCK)
    NC = (T - CMP_LEN) // CMP_STRIDE + 1
    NCP = ckv.shape[1]
    WN = -(-(WINDOW + QB) // LANE) * LANE
    assert NS <= LANE and NC <= NCP and P % LANE == 0 and QB % 32 == 0 and LW >= WN and (LW - WN) % 16 == 0
    ov = _overlap_map(NCP, NC, NS)
    expand = (np.arange(P)[None, :] // SEL_BLOCK == np.arange(LANE)[:, None]).astype(np.float32)
    const = lambda shape: pl.BlockSpec(shape, lambda b, nb: (0,) * len(shape))
    nvar = next((v for v in (8, 4, 2) if NB % v == 0 and (P // v) % (2 * LANE) == 0), 1)
    NQ = 2 if (NB // nvar) % 2 == 0 else 1
    rows = lambda w: pl.BlockSpec((NQ * QB, w), lambda b, nb: (b * (NB // NQ) + nb, 0))
    KVH, R = NSA_KV_HEADS, NSA_GROUP * QB
    NU = NQ * KVH
    return pl.pallas_call(
        functools.partial(_nsa_kernel, QB=QB, NQ=NQ, NB=NB, NS=NS, P=P, WN=WN, nvar=nvar),
        grid=(B, NB // NQ),
        in_specs=[rows(2 * NSA_WIDTH), rows(LANE),
                  pl.BlockSpec((1, NCP, 2 * KV_WIDTH), lambda b, nb: (b, 0, 0)),
                  pl.BlockSpec((1, P, 2 * KV_WIDTH), lambda b, nb: (b, 0, 1)),
                  pl.BlockSpec((1, LW, 2 * KV_WIDTH), lambda b, nb: (b, 0, 0)),
                  const((NCP, LANE)), const((LANE, P))],
        out_specs=rows(NSA_WIDTH),
        out_shape=jax.ShapeDtypeStruct((N, NSA_WIDTH), F32),
        scratch_shapes=[pltpu.VMEM((KVH, P, HEAD_DIM), BF16), pltpu.VMEM((KVH, P, 2 * HEAD_DIM), BF16),
                        pltpu.VMEM((KVH, LW, HEAD_DIM), BF16), pltpu.VMEM((KVH, LW, 2 * HEAD_DIM), BF16),
                        pltpu.VMEM((NU, R, P), F32), pltpu.VMEM((NU, R, P), BF16), pltpu.VMEM((NU, QB, P), F32),
                        pltpu.VMEM((NU, R, WN), F32), pltpu.VMEM((NU, R, WN), BF16), pltpu.VMEM((NQ, QB, WN), F32)],
        compiler_params=_cp(("parallel", "arbitrary")),
        name="nsa_attend",
    )(qq, sm, ckv, kv_rows, win_rows, jnp.asarray(ov, F32), jnp.asarray(expand, BF16))


def _nsa_paged_kernel(pt_ref, *refs, SEQ, n_pages, page, T, NS, offset, lwin):
    del pt_ref
    page_refs = refs[:SEQ * n_pages]
    (q_ref, sm_ref, new_ref, wcache_ref, wnew_ref, posl_ref, posh_ref, wl_ref, wh_ref, w2_ref, ov_ref, ex_ref,
     o_ref, wout_ref, x_ref) = refs[SEQ * n_pages:]
    D, G = HEAD_DIM, NSA_GROUP
    QB = T
    R = G * QB
    P = (n_pages + 1) * page
    NG = n_pages * page // GRP_ROWS
    pages = lambda s: page_refs[s * n_pages:(s + 1) * n_pages]

    def new_tile(x, c0):
        return x[:, c0:c0 + D].T.astype(BF16)

    for s in range(SEQ):
        for i in range(n_pages):
            for j in range(2):
                for g in range(NSA_KV_HEADS):
                    x_ref[s, j, i * page:(i + 1) * page, g * D:(g + 1) * D] = pages(s)[i][0, j, g].T
    ckv = {}
    for s in range(SEQ):
        for j in range(2):
            xg = jnp.concatenate([x_ref.at[s, j][pl.ds(r, NG, stride=GRP_ROWS), :] for r in range(GRP_ROWS)], axis=-1)
            ckv[s, j] = _compress_slab(xg, j, posl_ref, posh_ref, wl_ref, wh_ref, w2_ref)

    tq = offset + _iota((QB, 1), 0)
    tq_r = offset + (_iota((R, 1), 0) % QB)
    kpos = _iota((1, P), 1)
    dlt = tq - (offset - lwin + _iota((1, lwin + page), 1))
    bias_w = jnp.where((dlt >= 0) & (dlt < WINDOW), 0.0, NEG)
    zrows = lambda w: jnp.zeros((page - T, w), F32)
    softmax_pv = functools.partial(_softmax_pv, QB=QB, v_keys_minor=True)

    units = []
    for s in range(SEQ):
        rows = slice(s * T, (s + 1) * T)
        qv = q_ref[rows, :] * (D ** -0.5)
        gates = _sigmoid(sm_ref[rows, :])
        new_t = jnp.concatenate([new_ref[rows, :], zrows(KV_ROW)], axis=0)
        wnew_t = jnp.concatenate([wnew_ref[rows, :], zrows(2 * KV_WIDTH)], axis=0)
        for g in range(NSA_KV_HEADS):
            kt = jnp.concatenate([pg[0, 2, g].astype(BF16) for pg in pages(s)]
                                 + [new_tile(new_t, 2 * KV_WIDTH + g * D)], axis=1)
            vt = jnp.concatenate([pg[0, 3, g].astype(BF16) for pg in pages(s)]
                                 + [new_tile(new_t, 3 * KV_WIDTH + g * D)], axis=1)
            wkt = jnp.concatenate([wcache_ref[s, 0, g].astype(BF16), new_tile(wnew_t, g * D)], axis=1)
            wvt = jnp.concatenate([wcache_ref[s, 1, g].astype(BF16), new_tile(wnew_t, KV_WIDTH + g * D)], axis=1)
            units.append(dict(s=s, g=g, tq=tq, tq_r=tq_r, gates=gates, q_raw=_head_stack(qv, 0, g).astype(BF16),
                              q_rot=_head_stack(qv, NSA_WIDTH, g).astype(BF16), kt=kt, vt=vt, wkt=wkt, wvt=wvt,
                              ck=ckv[s, 0][:, g * D:(g + 1) * D], cv=ckv[s, 1][:, g * D:(g + 1) * D]))
    for u in units:
        u['s_s'] = _dot(u['q_rot'], u['kt'])
        u['s_w'] = _dot(u['q_rot'], u['wkt'])
    for u in units:
        u['o_w'] = softmax_pv(u['s_w'], bias_w, jnp.concatenate([u['wvt'], jnp.ones_like(u['wvt'])], axis=0))
    _cmp_select(units, ov_ref[...], QB, NS, NS)
    for u in units:
        bias_s = jnp.where((_dot(u['sel'], ex_ref[...]) > 0.5) & (kpos <= tq), 0.0, NEG)
        o_s = softmax_pv(u['s_s'], bias_s, jnp.concatenate([u['vt'], jnp.ones_like(u['vt'])], axis=0))
        gates, g = u['gates'], u['g']
        u['o'] = _gate_stack(gates, g, 0) * u['o_c'] + _gate_stack(gates, g, 1) * o_s + _gate_stack(gates, g, 2) * u['o_w']
    for s in range(SEQ):
        outs = [u['o'][h * QB:(h + 1) * QB] for u in units if u['s'] == s for h in range(G)]
        o_ref[s * T:(s + 1) * T, :] = jnp.concatenate(outs, axis=-1)
    for s in range(SEQ):
        wnew_t = jnp.concatenate([wnew_ref[s * T:(s + 1) * T, :], zrows(2 * KV_WIDTH)], axis=0)
        for kv in range(2):
            for g in range(NSA_KV_HEADS):
                c0 = kv * KV_WIDTH + g * D
                wout_ref[s, kv, g] = jnp.concatenate([wcache_ref[s, kv, g][:, T:], wnew_t[:, c0:c0 + D].T[:, 0:T]], axis=1)


def _nsa_paged(qq, sm, kv_new, win_new, cache_t, page_ids, wcache_t, win_base, cmp_pos, cmp_w1, cmp_w2, B, T, offset):
    N = B * T
    n_pages = page_ids.shape[1]
    page = cache_t.shape[-1]
    lwin = wcache_t.shape[-1]
    p_len = n_pages * page + T
    P = (n_pages + 1) * page
    NS = -(-p_len // SEL_BLOCK)
    n_cmp = (p_len - CMP_LEN) // CMP_STRIDE + 1
    NG = n_pages * page // GRP_ROWS
    SEQ = 2 if (B % 2 == 0 and win_base % 2 == 0) else 1
    assert T % 8 == 0 and T <= page and NS <= LANE and n_cmp + 1 == NG and lwin % LANE == 0 and NG % 8 == 0
    w_lo, w_hi, w2big, pos_lo, pos_hi = _pack_cmp_weights(cmp_pos, cmp_w1, cmp_w2)
    ov = _overlap_map(NG, n_cmp, NS)
    expand = (np.arange(P)[None, :] // SEL_BLOCK == np.arange(LANE)[:, None]).astype(np.float32)

    def page_map(k, b, pt):
        return (pt[b * SEQ * n_pages + k], 0, 0, 0, 0)

    const = lambda shape: pl.BlockSpec(shape, lambda b, pt: (0,) * len(shape))
    rows = lambda w: pl.BlockSpec((SEQ * T, w), lambda b, pt: (b, 0))
    in_specs = [pl.BlockSpec((1, 4, NSA_KV_HEADS, HEAD_DIM, page), functools.partial(page_map, k))
                for k in range(SEQ * n_pages)]
    wshape = (SEQ, 2, NSA_KV_HEADS, HEAD_DIM, lwin)
    in_specs += [rows(2 * NSA_WIDTH), rows(LANE), rows(KV_ROW),
                 pl.BlockSpec(wshape, lambda b, pt: (win_base // SEQ + b, 0, 0, 0, 0)),
                 rows(2 * KV_WIDTH)] + [const(s) for s in _CMP_SPECS] + [const((NG, LANE)), const((LANE, P))]
    return pl.pallas_call(
        functools.partial(_nsa_paged_kernel, SEQ=SEQ, n_pages=n_pages, page=page, T=T, NS=NS, offset=offset, lwin=lwin),
        grid_spec=pltpu.PrefetchScalarGridSpec(
            num_scalar_prefetch=1, grid=(B // SEQ,), in_specs=in_specs,
            out_specs=[rows(NSA_WIDTH), pl.BlockSpec(wshape, lambda b, pt: (b, 0, 0, 0, 0))],
            scratch_shapes=[pltpu.VMEM((SEQ, 2, n_pages * page, KV_WIDTH), F32)]),
        out_shape=[jax.ShapeDtypeStruct((N, NSA_WIDTH), F32), jax.ShapeDtypeStruct((B,) + wshape[1:], F32)],
        compiler_params=_cp(("parallel",)),
        name="nsa_paged",
    )(page_ids.reshape(-1), *([cache_t] * (SEQ * n_pages)), qq, sm, kv_new, wcache_t, win_new, pos_lo, pos_hi, w_lo,
      w_hi, w2big, jnp.asarray(ov, F32), jnp.asarray(expand, BF16))


N_MIX = 8


def _mix_residual(x_ref, hml_ref, yrw_ref, gb_ref, hns_ref, lnw_ref, lnb_ref, w_ref):
    W = RW_WIDTH
    y = yrw_ref[...]
    ones = _head_ones(W)
    mu = _head_sum(y, ones) * (1.0 / HEAD_DIM)
    var = _head_sum(jnp.square(y - mu), ones) * (1.0 / HEAD_DIM)
    yn = (y - mu) * lax.rsqrt(var + RW_GN_EPS) * lnw_ref[...]
    h_rw = (yn + lnb_ref[...] + gb_ref[:, W:2 * W]) * gb_ref[:, 0:W]
    acc = _dot(hml_ref[...].astype(BF16), w_ref[0:ML_WIDTH, :])
    acc = acc + _dot(h_rw.astype(BF16), w_ref[ML_WIDTH:ML_WIDTH + W, :])
    acc = acc + _dot(hns_ref[...].astype(BF16), w_ref[ML_WIDTH + W:, :])
    return x_ref[...] + acc


def _mix_specs(tm):
    row = lambda w: pl.BlockSpec((tm, w), lambda i, j: (i, 0))
    const = lambda shape: pl.BlockSpec(shape, lambda i, j: (0,) * len(shape))
    return [row(D_MODEL), row(ML_WIDTH), row(RW_WIDTH), row(2 * RW_WIDTH), row(NSA_WIDTH),
            const((1, RW_WIDTH)), const((1, RW_WIDTH)), const((MIX_WIDTH, D_MODEL))]


def _mix_args(x2, h_ml, y_rw, gb, h_ns, ln_w, ln_b, w_out):
    return (x2, h_ml, y_rw, gb, h_ns, ln_w.reshape(1, -1), ln_b.reshape(1, -1), w_out.astype(BF16))


def _rms(x, w):
    return x * lax.rsqrt(jnp.mean(x * x, axis=-1, keepdims=True) + RMS_EPS) * w


def _ffn_kernel(*refs, final_norm):
    nw_ref, w1_ref, w3_ref, w2_ref, fw_ref, o_ref, xn_ref = refs[N_MIX:]
    j = pl.program_id(1)

    @pl.when(j == 0)
    def _():
        x = _mix_residual(*refs[:N_MIX])
        xn_ref[...] = _rms(x, nw_ref[...]).astype(BF16)
        o_ref[...] = x

    xn = xn_ref[...]
    hid = _silu(_dot(xn, w1_ref[...])) * _dot(xn, w3_ref[...])
    o_ref[...] += _dot(hid.astype(BF16), w2_ref[...])

    if final_norm:
        @pl.when(j == pl.num_programs(1) - 1)
        def _():
            o_ref[...] = _rms(o_ref[...], fw_ref[...])


def _ffn(mix, norm_w, w1, w3, w2, final_w):
    N = mix[0].shape[0]
    d_ff = w1.shape[1]
    tm = min(512, N)
    nj = 2 if (d_ff // 2) % LANE == 0 else 1
    tf = d_ff // nj
    final_norm = final_w is not None
    fw = (final_w if final_norm else jnp.ones((D_MODEL,), F32)).reshape(1, D_MODEL)
    return pl.pallas_call(
        functools.partial(_ffn_kernel, final_norm=final_norm),
        grid=(N // tm, nj),
        in_specs=_mix_specs(tm) + [
            pl.BlockSpec((1, D_MODEL), lambda i, j: (0, 0)),
            pl.BlockSpec((D_MODEL, tf), lambda i, j: (0, j)), pl.BlockSpec((D_MODEL, tf), lambda i, j: (0, j)),
            pl.BlockSpec((tf, D_MODEL), lambda i, j: (j, 0)), pl.BlockSpec((1, D_MODEL), lambda i, j: (0, 0))],
        out_specs=pl.BlockSpec((tm, D_MODEL), lambda i, j: (i, 0)),
        out_shape=jax.ShapeDtypeStruct((N, D_MODEL), F32),
        scratch_shapes=[pltpu.VMEM((tm, D_MODEL), BF16)],
        compiler_params=_cp(("parallel", "arbitrary")),
        name="ffn",
    )(*_mix_args(*mix), norm_w.reshape(1, D_MODEL), w1.astype(BF16), w3.astype(BF16), w2.astype(BF16), fw)


def _moe_kernel(*refs, final_norm):
    nw_ref, rt_ref, w1_ref, w3_ref, w2_ref, fw_ref, o_ref, xn_ref, comb_ref = refs[N_MIX:]
    e = pl.program_id(1)

    @pl.when(e == 0)
    def _():
        x = _mix_residual(*refs[:N_MIX])
        xn = _rms(x, nw_ref[...])
        xn_ref[...] = xn.astype(BF16)
        o_ref[...] = x
        lane = _iota((1, LANE), 1).astype(F32)
        logits = jnp.where(lane < N_EXPERTS, _dot(xn, rt_ref[...], HIGHEST), -jnp.inf)
        m1 = jnp.max(logits, axis=-1, keepdims=True)
        i1 = jnp.min(jnp.where(logits == m1, lane, float(LANE)), axis=-1, keepdims=True)
        rest = jnp.where(lane == i1, -jnp.inf, logits)
        m2 = jnp.max(rest, axis=-1, keepdims=True)
        i2 = jnp.min(jnp.where(rest == m2, lane, float(LANE)), axis=-1, keepdims=True)
        e2 = jnp.exp(m2 - m1)
        den = 1.0 + e2
        comb_ref[...] = jnp.where(lane == i1, 1.0 / den, 0.0) + jnp.where(lane == i2, e2 / den, 0.0)

    xn = xn_ref[...]
    comb = comb_ref[...]
    ce = jnp.sum(jnp.where(_iota((1, LANE), 1) == e, comb, 0.0), axis=-1, keepdims=True)
    hid = _silu(_dot(xn, w1_ref[0])) * _dot(xn, w3_ref[0])
    o_ref[...] += ce * _dot(hid.astype(BF16), w2_ref[0])

    if final_norm:
        @pl.when(e == pl.num_programs(1) - 1)
        def _():
            o_ref[...] = _rms(o_ref[...], fw_ref[...])


def _moe(mix, norm_w, router, w1, w3, w2, final_w):
    N = mix[0].shape[0]
    E, _, d_exp = w1.shape
    assert E == N_EXPERTS and TOP_K == 2
    tm = min(512, N)
    final_norm = final_w is not None
    fw = (final_w if final_norm else jnp.ones((D_MODEL,), F32)).reshape(1, D_MODEL)
    rt = jnp.pad(router, ((0, 0), (0, LANE - E)))
    return pl.pallas_call(
        functools.partial(_moe_kernel, final_norm=final_norm),
        grid=(N // tm, E),
        in_specs=_mix_specs(tm) + [
            pl.BlockSpec((1, D_MODEL), lambda i, e: (0, 0)),
            pl.BlockSpec((D_MODEL, LANE), lambda i, e: (0, 0)),
            pl.BlockSpec((1, D_MODEL, d_exp), lambda i, e: (e, 0, 0)),
            pl.BlockSpec((1, D_MODEL, d_exp), lambda i, e: (e, 0, 0)),
            pl.BlockSpec((1, d_exp, D_MODEL), lambda i, e: (e, 0, 0)),
            pl.BlockSpec((1, D_MODEL), lambda i, e: (0, 0))],
        out_specs=pl.BlockSpec((tm, D_MODEL), lambda i, e: (i, 0)),
        out_shape=jax.ShapeDtypeStruct((N, D_MODEL), F32),
        scratch_shapes=[pltpu.VMEM((tm, D_MODEL), BF16), pltpu.VMEM((tm, LANE), F32)],
        compiler_params=_cp(("parallel", "arbitrary")),
        name="moe",
    )(*_mix_args(*mix), norm_w.reshape(1, D_MODEL), rt, w1.astype(BF16), w3.astype(BF16), w2.astype(BF16), fw)


def _layer(l, x2, B, T, offset, st, P, final_w):
    C0, n0, m0, conv0, S0, shift0, cache, page_ids, win0 = st
    rw_perm = _rw_order()
    pos = offset + jnp.arange(T)
    cos2, sin2 = _rope_tables(pos)
    ml, rw, qq, kv4, wrow, sm, cks, cvs = _inproj(x2, P['norm_mix_w'][l], _pack_w_in(P['w_in'][l]), cos2, sin2, T)

    h_ml, C, n, m, conv = _mlstm(ml, sm, conv0, C0, n0, m0, P['ml_conv_w'][l], P['ml_conv_b'][l],
                                 P['ml_b_i'][l], P['ml_b_f'][l], P['ml_norm_w'][l], B, T)

    y_rw, gb, S, shift_p = _rwkv(rw, _take_cols(shift0, rw_perm), S0, _take_cols(P['rw_mu'][l], rw_perm),
                                 P['rw_w0'][l], P['rw_w2'][l], P['rw_a0'][l], P['rw_a2'][l], P['rw_g2'][l],
                                 P['rw_k_k'][l], P['rw_k_a'][l], P['rw_r_k'][l].reshape(-1), B, T)
    shift_new = _take_cols(shift_p, np.argsort(rw_perm))

    kv3 = kv4.reshape(B, T, KV_ROW)
    w3 = wrow.reshape(B, T, 2 * KV_WIDTH)
    cmp_w = (P['nsa_cmp_pos'][l], P['nsa_cmp_w1'][l], P['nsa_cmp_w2'][l])
    wbuf = P['wbuf']
    if cache is None:
        n_cmp = (T - CMP_LEN) // CMP_STRIDE + 1
        G = -(-(n_cmp + 1) // 8) * 8
        assert G * GRP_ROWS <= T
        ckv = _compress(cks.reshape(B, T, KV_WIDTH), cvs.reshape(B, T, KV_WIDTH), *cmp_w, B, G)
        assert offset == 0
        h_ns = _nsa_attend(qq, sm, ckv, kv3, w3, B, T)
        win_new = w3[:, T - wbuf:] if T >= wbuf else jnp.pad(w3, ((0, 0), (wbuf - T, 0), (0, 0)))
        win_new = win_new.reshape(B, wbuf, 2, NSA_KV_HEADS, HEAD_DIM)
    else:
        assert win0['key_minor'].shape[-1] == wbuf and T <= wbuf
        h_ns, wout = _nsa_paged(qq, sm, kv4, wrow, cache, page_ids, win0['key_minor'], win0['base'], *cmp_w, B, T, offset)
        win_new = jnp.transpose(wout, (0, 4, 1, 2, 3))

    mix = (x2, h_ml, y_rw, gb, h_ns, P['rw_ln_w'][l], P['rw_ln_b'][l], P['w_out'][l])
    if l % 2 == 0:
        x2 = _ffn(mix, P['norm_ffn_w'][l], P['ffn_w1'][l // 2], P['ffn_w3'][l // 2], P['ffn_w2'][l // 2], final_w)
    else:
        x2 = _moe(mix, P['norm_ffn_w'][l], P['moe_router'][l // 2], P['moe_w1'][l // 2], P['moe_w3'][l // 2],
                  P['moe_w2'][l // 2], final_w)
    new = (kv3.reshape(B, T, 4, NSA_KV_HEADS, HEAD_DIM), win_new, C, n, m, conv, S, shift_new)
    return x2, new


def _run(x, offset, get_state, P, depth):
    B, T, _ = x.shape
    x2 = x.reshape(B * T, D_MODEL)
    news = []
    for l in range(depth):
        x2, nw = _layer(l, x2, B, T, offset, get_state(l), P, P['norm_final_w'] if l == depth - 1 else None)
        news.append(nw)
    return x2.reshape(B, T, D_MODEL), [jnp.stack([nw[i] for nw in news]) for i in range(8)]


def kernel(x_prompt, x_sample, cache_nsa_kv, cache_win_kv, state_mlstm_C, state_mlstm_n, state_mlstm_m, state_mlstm_conv, state_rwkv_S, state_rwkv_shift, page_table, norm_mix_w, w_in, ml_conv_w, ml_conv_b, ml_b_i, ml_b_f, ml_norm_w, rw_mu, rw_w0, rw_w2, rw_a0, rw_a2, rw_g2, rw_k_k, rw_k_a, rw_r_k, rw_ln_w, rw_ln_b, nsa_cmp_pos, nsa_cmp_w1, nsa_cmp_w2, w_out, norm_ffn_w, ffn_w1, ffn_w3, ffn_w2, moe_router, moe_w1, moe_w3, moe_w2, norm_final_w):
    depth = w_in.shape[0]
    Bp = x_prompt.shape[0]
    Bs = x_sample.shape[0]
    n_pool, page = cache_nsa_kv.shape[1], cache_nsa_kv.shape[2]
    past_len = page_table.shape[1] * page
    wbuf = min(WINDOW, past_len)
    P = dict(norm_mix_w=norm_mix_w, w_in=w_in, ml_conv_w=ml_conv_w, ml_conv_b=ml_conv_b, ml_b_i=ml_b_i, ml_b_f=ml_b_f,
             ml_norm_w=ml_norm_w, rw_mu=rw_mu, rw_w0=rw_w0, rw_w2=rw_w2, rw_a0=rw_a0, rw_a2=rw_a2, rw_g2=rw_g2,
             rw_k_k=rw_k_k, rw_k_a=rw_k_a, rw_r_k=rw_r_k, rw_ln_w=rw_ln_w, rw_ln_b=rw_ln_b, nsa_cmp_pos=nsa_cmp_pos,
             nsa_cmp_w1=nsa_cmp_w1, nsa_cmp_w2=nsa_cmp_w2, w_out=w_out, norm_ffn_w=norm_ffn_w, ffn_w1=ffn_w1,
             ffn_w3=ffn_w3, ffn_w2=ffn_w2, moe_router=moe_router, moe_w1=moe_w1, moe_w3=moe_w3, moe_w2=moe_w2,
             norm_final_w=norm_final_w, wbuf=wbuf)

    def prompt_state(l):
        z = lambda *s: jnp.zeros((Bp,) + s, F32)
        return (z(ML_HEADS, HEAD_DIM, HEAD_DIM), z(ML_HEADS, HEAD_DIM), z(ML_HEADS), z(ML_CONV - 1, 2 * ML_WIDTH),
                z(RW_HEADS, HEAD_DIM, HEAD_DIM), z(RW_COLS), None, None, None)

    cache_t = jnp.transpose(cache_nsa_kv, (0, 1, 3, 4, 5, 2)).reshape(depth * n_pool, 4, NSA_KV_HEADS, HEAD_DIM, page)
    wcache_t = jnp.transpose(cache_win_kv, (0, 1, 3, 4, 5, 2)).reshape(depth * Bs, 2, NSA_KV_HEADS, HEAD_DIM, -1)

    def sample_state(l):
        win = dict(key_minor=wcache_t, base=l * Bs)
        return (state_mlstm_C[l], state_mlstm_n[l], state_mlstm_m[l], state_mlstm_conv[l], state_rwkv_S[l],
                state_rwkv_shift[l], cache_t, page_table + l * n_pool, win)

    y_prompt, pst = _run(x_prompt, 0, prompt_state, P, depth)
    y_sample, sst = _run(x_sample, past_len, sample_state, P, depth)
    p_kv, p_win, p_C, p_n, p_m, p_conv, p_S, p_shift = pst
    s_kv, s_win, s_C, s_n, s_m, s_conv, s_S, s_shift = sst
    return (y_prompt, y_sample, p_kv, s_kv, p_win, s_win, p_C, s_C, p_n, s_n, p_m, s_m,
            p_conv, s_conv, p_S, s_S, p_shift, s_shift)
```

```python
import functools
import math

import numpy as np
import jax
import jax.numpy as jnp
from jax import lax
from jax.experimental import pallas as pl
from jax.experimental.pallas import tpu as pltpu

F32 = jnp.float32
BF16 = jnp.bfloat16
HIGHEST = lax.Precision.HIGHEST

D_MODEL = 1024
HEAD_DIM = 64
ML_HEADS = 4
RW_HEADS = 4
NSA_HEADS = 8
NSA_KV_HEADS = 2
NSA_GROUP = NSA_HEADS // NSA_KV_HEADS
ML_WIDTH = ML_HEADS * HEAD_DIM
RW_WIDTH = RW_HEADS * HEAD_DIM
NSA_WIDTH = NSA_HEADS * HEAD_DIM
KV_WIDTH = NSA_KV_HEADS * HEAD_DIM
MIX_WIDTH = ML_WIDTH + RW_WIDTH + NSA_WIDTH
ML_CHUNK = 64
ML_CONV = 4
ML_NORM_EPS = 1e-6
RW_LORA_W = 64
RW_LORA_A = 64
RW_LORA_G = 128
RW_GN_EPS = 64e-5
RW_CHUNK = 64
CMP_LEN = 32
CMP_STRIDE = 16
CMP_HID = 128
SEL_BLOCK = 64
SEL_TOPN = 8
WINDOW = 512
Q_BLOCK = 64
ROPE_THETA = 10000.0
N_EXPERTS = 8
TOP_K = 2
RMS_EPS = 1e-6
NEG = -1e30
SEL_NEG = -1e9
FORCE_BONUS = 1e3
ML_COLS = 4 * ML_WIDTH + 2 * ML_HEADS
RW_COLS = 3 * RW_WIDTH + RW_LORA_W + RW_LORA_A + RW_LORA_G
NSA_COLS = NSA_WIDTH + 6 * KV_WIDTH + 3 * NSA_HEADS

LANE = 128
KV_ROW = 4 * KV_WIDTH
GRP_ROWS = CMP_STRIDE
VMEM_LIMIT = 52 * 1024 * 1024

C_ML = 0
C_RW = 1024
C_Q = 2048
C_KV = 2560
C_WK = 3072
C_WV = 3200
C_SM = 3328
C_TOT = 3456


def _cp(sem, vmem=VMEM_LIMIT):
    return pltpu.CompilerParams(dimension_semantics=sem, vmem_limit_bytes=vmem)


def _dot(a, b, precision=None):
    return jnp.dot(a, b, preferred_element_type=F32, precision=precision)


def _dot_nt(a, b, precision=None):
    return lax.dot_general(a, b, (((1,), (1,)), ((), ())), preferred_element_type=F32, precision=precision)


def _dot_tn(a, b, precision=None):
    return lax.dot_general(a, b, (((0,), (0,)), ((), ())), preferred_element_type=F32, precision=precision)


def _sigmoid(x):
    return 1.0 / (1.0 + jnp.exp(-x))


def _silu(x):
    return x * _sigmoid(x)


def _iota(shape, dim):
    return lax.broadcasted_iota(jnp.int32, shape, dim)


def _head_ones(w):
    return jnp.where(_iota((w, w), 0) // HEAD_DIM == _iota((w, w), 1) // HEAD_DIM, 1.0, 0.0).astype(BF16)


def _head_sum(x, ones):
    hi = x.astype(BF16)
    lo = (x - hi.astype(F32)).astype(BF16)
    return _dot(hi, ones) + _dot(lo, ones)


def _masked_softmax(s, mask):
    s = jnp.where(mask, s, NEG)
    m = jnp.max(s, axis=-1, keepdims=True)
    p = jnp.where(mask, jnp.exp(s - m), 0.0)
    return p / jnp.maximum(jnp.sum(p, -1, keepdims=True), 1e-30)


def _rw_order():
    r0, wl0 = 0, RW_WIDTH
    k0 = wl0 + RW_LORA_W
    v0 = k0 + RW_WIDTH
    al0 = v0 + RW_WIDTH
    gl0 = al0 + RW_LORA_A
    return np.concatenate([np.arange(r0, r0 + RW_WIDTH), np.arange(k0, k0 + RW_WIDTH), np.arange(v0, v0 + RW_WIDTH),
                           np.arange(wl0, wl0 + RW_LORA_W), np.arange(al0, al0 + RW_LORA_A),
                           np.arange(gl0, gl0 + RW_LORA_G)])


def _runs(order):
    cuts = np.flatnonzero(np.diff(order) != 1) + 1
    return [(int(seg[0]), int(seg[-1]) + 1) for seg in np.split(np.asarray(order), cuts)]


def _take_cols(x, order):
    return jnp.concatenate([x[..., lo:hi] for lo, hi in _runs(order)], axis=-1)


def _pack_w_in(w):
    ml0, rw0, ns0 = 0, ML_COLS, ML_COLS + RW_COLS
    gt0 = ns0 + NSA_WIDTH + 6 * KV_WIDTH
    rw_runs = [(rw0 + lo, rw0 + hi) for lo, hi in _runs(_rw_order())]
    runs = [(ml0, ml0 + 4 * ML_WIDTH)] + rw_runs + [(ns0, gt0), (ml0 + 4 * ML_WIDTH, ml0 + ML_COLS), (gt0, gt0 + 3 * NSA_HEADS)]
    width = sum(hi - lo for lo, hi in runs)
    assert C_SM <= width <= C_TOT
    pieces = [w[:, lo:hi] for lo, hi in runs] + [jnp.zeros((w.shape[0], C_TOT - width), w.dtype)]
    return jnp.concatenate(pieces, axis=1).astype(BF16)


def _swap_rotary_halves(x):
    half = HEAD_DIM // 2
    w = x.shape[-1]
    first = _iota((1, w), 1) % HEAD_DIM < half
    return jnp.where(first, pltpu.roll(x, w - half, 1), pltpu.roll(x, half, 1))


def _rope_tables(pos):
    half = HEAD_DIM // 2
    inv = ROPE_THETA ** (-jnp.arange(half, dtype=F32) / half)
    ang = pos.astype(F32)[:, None] * inv[None, :]
    cos, sin = jnp.cos(ang), jnp.sin(ang)
    cos2 = jnp.concatenate([cos, cos, cos, cos], -1)
    sin2 = jnp.concatenate([-sin, sin, -sin, sin], -1)
    return cos2, sin2


def _inproj_kernel(x_ref, nw_ref, w_ref, cos_ref, sin_ref, ml_ref, rw_ref, q_ref, kv_ref, win_ref, sm_ref, ck_ref,
                   cv_ref):
    x = x_ref[...]
    xn = x * lax.rsqrt(jnp.mean(x * x, axis=-1, keepdims=True) + RMS_EPS) * nw_ref[...]
    acc = _dot(xn.astype(BF16), w_ref[...])
    cos = cos_ref[...]
    sin = sin_ref[...]
    cos4 = jnp.concatenate([cos] * 4, axis=-1)
    sin4 = jnp.concatenate([sin] * 4, axis=-1)
    ml_ref[...] = acc[:, C_ML:C_ML + 1024]
    rw_ref[...] = acc[:, C_RW:C_RW + 1024]
    rope = lambda v, c, s: v * c + _swap_rotary_halves(v) * s
    q = acc[:, C_Q:C_Q + NSA_WIDTH]
    q_ref[:, 0:NSA_WIDTH] = q
    q_ref[:, NSA_WIDTH:2 * NSA_WIDTH] = rope(q, cos4, sin4)
    kv_ref[:, 0:2 * KV_WIDTH] = acc[:, C_KV:C_KV + 2 * KV_WIDTH]
    kv_ref[:, 2 * KV_WIDTH:3 * KV_WIDTH] = rope(acc[:, C_KV + 2 * KV_WIDTH:C_KV + 3 * KV_WIDTH], cos, sin)
    kv_ref[:, 3 * KV_WIDTH:4 * KV_WIDTH] = acc[:, C_KV + 3 * KV_WIDTH:C_KV + 4 * KV_WIDTH]
    win_ref[:, 0:KV_WIDTH] = rope(acc[:, C_WK:C_WK + KV_WIDTH], cos, sin)
    win_ref[:, KV_WIDTH:2 * KV_WIDTH] = acc[:, C_WV:C_WV + KV_WIDTH]
    sm_ref[...] = acc[:, C_SM:C_SM + LANE]
    ck_ref[...] = acc[:, C_KV:C_KV + KV_WIDTH]
    cv_ref[...] = acc[:, C_KV + KV_WIDTH:C_KV + 2 * KV_WIDTH]


def _inproj(x2, norm_w, w_packed, cos2, sin2, T):
    N = x2.shape[0]
    tm = min(256, N)
    if T >= tm:
        assert T % tm == 0
        nt = T // tm
    else:
        assert tm % T == 0
        cos2 = jnp.tile(cos2, (tm // T, 1))
        sin2 = jnp.tile(sin2, (tm // T, 1))
        nt = 1
    row = lambda w: pl.BlockSpec((tm, w), lambda i: (i, 0))
    tab = pl.BlockSpec((tm, LANE), lambda i: (i % nt, 0))
    widths = (1024, 1024, 2 * NSA_WIDTH, KV_ROW, 2 * KV_WIDTH, LANE, KV_WIDTH, KV_WIDTH)
    return pl.pallas_call(
        _inproj_kernel,
        grid=(N // tm,),
        in_specs=[row(D_MODEL), pl.BlockSpec((1, D_MODEL), lambda i: (0, 0)),
                  pl.BlockSpec((D_MODEL, C_TOT), lambda i: (0, 0)), tab, tab],
        out_specs=[row(w) for w in widths],
        out_shape=[jax.ShapeDtypeStruct((N, w), F32) for w in widths],
        compiler_params=_cp(("parallel",)),
        name="inproj",
    )(x2, norm_w.reshape(1, D_MODEL), w_packed, cos2, sin2)


def _mlstm_kernel(ml_ref, sm_ref, conv0_ref, c0_ref, n0_ref, m0_ref, cw_ref, cb_ref, gb_ref, nw_ref,
                  h_ref, c_ref, n_ref, m_ref, conv_ref, xp_ref, *, Bb, L):
    c = pl.program_id(1)
    H, D = ML_HEADS, HEAD_DIM
    PAD = 8
    KC = ML_CONV - 1

    @pl.when(c == 0)
    def _():
        c_ref[...] = c0_ref[...]
        n_ref[...] = n0_ref[...]
        m_ref[...] = m0_ref[...]
        xp_ref[:, PAD - KC:PAD, :] = conv0_ref[...]

    tril = (_iota((L, L), 1) <= _iota((L, L), 0))
    eye = (_iota((L, L), 1) == _iota((L, L), 0))
    tril_b = jnp.where(tril, 1.0, 0.0).astype(BF16)

    bs = [dict(b=b) for b in range(Bb)]
    for s in bs:
        b = s['b']
        xp_ref[b, PAD:PAD + L, :] = ml_ref[b, :, 0:2 * ML_WIDTH]
    for s in bs:
        b = s['b']
        y = jnp.broadcast_to(cb_ref[...], (L, 2 * ML_WIDTH))
        for j in range(ML_CONV):
            y = y + xp_ref[b, PAD - KC + j:PAD - KC + j + L, :] * cw_ref[j:j + 1, :]
        s['qk'] = _silu(y)
        gates = sm_ref[b] + gb_ref[...]
        lf = jnp.minimum(gates, 0.0) - jnp.log1p(jnp.exp(-jnp.abs(gates)))
        lf_hi = lf.astype(BF16)
        lf_lo = (lf - lf_hi.astype(F32)).astype(BF16)
        s['gates'] = gates
        s['bcum'] = _dot(tril_b, lf_hi) + _dot(tril_b, lf_lo)
    for s in bs:
        b = s['b']
        tail = xp_ref[b, PAD + L - KC:PAD + L, :]
        xp_ref[b, PAD - KC:PAD, :] = tail
        conv_ref[b] = tail
    for s in bs:
        s['b_all'] = pltpu.roll(s['bcum'], LANE - H, 1)
        s['a_all'] = s['gates'] - s['b_all']
        s['cmax'] = s['a_all']
    step = 1
    while step < L:
        for s in bs:
            s['cmax'] = jnp.where(_iota((L, 1), 0) >= step, jnp.maximum(s['cmax'], pltpu.roll(s['cmax'], step, 0)),
                                  s['cmax'])
        step *= 2
    units = []
    for s in bs:
        b, b_all, a_all = s['b'], s['b_all'], s['a_all']
        m_prev = m_ref[b]
        m_t = b_all + jnp.maximum(m_prev, s['cmax'])
        m_new = m_t[L - 1:L, :]
        b_last = b_all[L - 1:L, :]
        m_ref[b] = m_new
        inter = jnp.exp(b_all + m_prev - m_t)
        wgt = jnp.exp(b_last + a_all - m_new)
        dec = jnp.exp(b_last + m_prev - m_new)
        ccol = b_all - m_t
        einv = jnp.exp(-m_t)
        qk = s['qk']
        for h in range(H):
            lo = h * D
            col = lambda x: x[:, h:h + 1]
            units.append(dict(b=b, h=h, lo=lo, q=qk[:, lo:lo + D], k=qk[:, ML_WIDTH + lo:ML_WIDTH + lo + D] * (D ** -0.5),
                              v=ml_ref[b, :, 2 * ML_WIDTH + lo:2 * ML_WIDTH + lo + D], acol=col(a_all),
                              ccol=col(ccol), inter=col(inter), wgt=col(wgt), dec=col(dec), einv=col(einv)))

    ones_col = jnp.ones((L, D), BF16)
    for u in units:
        u['arow'] = jnp.sum(jnp.where(eye, u['acol'], 0.0), axis=0, keepdims=True)
    for u in units:
        u['dm'] = jnp.exp(jnp.where(tril, u['ccol'] + u['arow'], -jnp.inf))
    for u in units:
        C = c_ref[u['b'], u['h']]
        qb, kb = u['q'].astype(BF16), u['k'].astype(BF16)
        if 2 * L == LANE:
            m1 = _dot_nt(qb, jnp.concatenate([kb, C.astype(BF16)], axis=0))
            u['s_raw'], u['qc'] = m1[:, 0:L], m1[:, L:]
        else:
            u['s_raw'], u['qc'] = _dot_nt(qb, kb), _dot_nt(qb, C.astype(BF16))
        c_ref[u['b'], u['h']] = u['dec'] * C + _dot_tn((u['v'] * u['wgt']).astype(BF16), kb)
        nrow = n_ref[u['b'], u['h']:u['h'] + 1, :]
        u['qn'] = jnp.sum(u['q'] * nrow, axis=-1, keepdims=True)
        n_ref[u['b'], u['h']:u['h'] + 1, :] = u['dec'] * nrow + jnp.sum(u['wgt'] * u['k'], axis=0, keepdims=True)
    for u in units:
        S = (u['s_raw'] * u['dm']).astype(BF16)
        sv = _dot(S, jnp.concatenate([u['v'].astype(BF16), ones_col], axis=1))
        den = u['inter'] * u['qn'] + sv[:, D:D + 1]
        u['hh'] = (u['inter'] * u['qc'] + sv[:, 0:D]) / jnp.maximum(jnp.abs(den), u['einv'])
    ones = _head_ones(ML_WIDTH).astype(F32) * (1.0 / D)
    for s in bs:
        s['hh'] = jnp.concatenate([u['hh'] for u in units if u['b'] == s['b']], axis=-1)
        s['mu'] = _dot(s['hh'], ones, HIGHEST)
    for s in bs:
        s['var'] = _dot(jnp.square(s['hh'] - s['mu']), ones, HIGHEST)
    for s in bs:
        b = s['b']
        hn = (s['hh'] - s['mu']) * lax.rsqrt(s['var'] + ML_NORM_EPS) * nw_ref[...]
        h_ref[b] = hn * _sigmoid(ml_ref[b, :, 3 * ML_WIDTH:4 * ML_WIDTH])


def _mlstm(ml, sm, conv0, C0, n0, m0, conv_w, conv_b, b_i, b_f, norm_w, B, T):
    L = math.gcd(T, ML_CHUNK)
    NC = T // L
    N = B * T
    Bb = 8 if B % 8 == 0 else B
    gate_bias = jnp.concatenate([b_i, b_f, jnp.zeros((LANE - 2 * ML_HEADS,), F32)]).reshape(1, LANE)
    m0p = jnp.pad(m0.reshape(B, 1, ML_HEADS), ((0, 0), (0, 0), (0, LANE - ML_HEADS)))
    per_b = lambda shape: pl.BlockSpec((Bb,) + shape, lambda b, c: (b,) + (0,) * len(shape))
    const = lambda shape: pl.BlockSpec(shape, lambda b, c: (0,) * len(shape))
    rows = lambda w: pl.BlockSpec((Bb, L, w), lambda b, c: (b, c, 0))
    state_shapes = [(ML_HEADS, HEAD_DIM, HEAD_DIM), (ML_HEADS, HEAD_DIM), (1, LANE), (ML_CONV - 1, 2 * ML_WIDTH)]
    h, C, n, m, conv = pl.pallas_call(
        functools.partial(_mlstm_kernel, Bb=Bb, L=L),
        grid=(B // Bb, NC),
        in_specs=[rows(1024), rows(LANE), per_b(state_shapes[3]), per_b(state_shapes[0]), per_b(state_shapes[1]),
                  per_b(state_shapes[2]), const((ML_CONV, 2 * ML_WIDTH)), const((1, 2 * ML_WIDTH)),
                  const((1, LANE)), const((1, ML_WIDTH))],
        out_specs=[rows(ML_WIDTH)] + [per_b(s) for s in state_shapes],
        out_shape=[jax.ShapeDtypeStruct((B, T, ML_WIDTH), F32)]
                  + [jax.ShapeDtypeStruct((B,) + s, F32) for s in state_shapes],
        scratch_shapes=[pltpu.VMEM((Bb, L + 8, 2 * ML_WIDTH), F32)],
        compiler_params=_cp(("parallel", "arbitrary")),
        name="mlstm",
    )(ml.reshape(B, T, 1024), sm.reshape(B, T, LANE), conv0, C0, n0, m0p, conv_w, conv_b.reshape(1, -1), gate_bias,
      norm_w.reshape(1, -1))
    return h.reshape(N, ML_WIDTH), C, n, m[:, 0, :ML_HEADS], conv


def _rwkv_prep(rw_ref, shift0_ref, mu_ref, w0_ref, w2_ref, a0_ref, a2_ref, g2_ref, kk_ref, ka_ref, rk_ref,
               gb_ref, shift_ref, carry_ref, *, Bb, tt):
    c = pl.program_id(1)
    W = RW_WIDTH
    C = rw_ref.shape[-1]

    @pl.when(c == 0)
    def _():
        carry_ref[...] = shift0_ref[...]

    x = rw_ref[...].reshape(Bb * tt, C)
    carry_rows = jnp.broadcast_to(carry_ref[...], (Bb, tt, C)).reshape(Bb * tt, C)
    prev = jnp.where(_iota((Bb * tt, 1), 0) % tt == 0, carry_rows, pltpu.roll(x, 1, 0))
    last = rw_ref[:, tt - 1:tt, :]
    carry_ref[...] = last
    shift_ref[...] = last
    xm = x + (prev - x) * mu_ref[...]
    r = xm[:, 0:W]
    k = xm[:, W:2 * W]
    v = xm[:, 2 * W:3 * W]
    wl = xm[:, 3 * W:3 * W + RW_LORA_W]
    al = xm[:, 3 * W + RW_LORA_W:3 * W + RW_LORA_W + RW_LORA_A]
    gl = xm[:, 3 * W + RW_LORA_W + RW_LORA_A:]
    wpre = w0_ref[...] + _dot(jnp.tanh(wl), w2_ref[...], HIGHEST)
    softplus_neg = jnp.maximum(-wpre, 0.0) + jnp.log1p(jnp.exp(-jnp.abs(wpre)))
    log_decay = -jnp.exp(-softplus_neg - 0.5)
    a = _sigmoid(a0_ref[...] + _dot(al, a2_ref[...], HIGHEST))
    g = _dot(_sigmoid(gl), g2_ref[...], HIGHEST)
    ones = _head_ones(W)
    kk = k * kk_ref[...]
    kk = kk / jnp.maximum(jnp.sqrt(_head_sum(kk * kk, ones)), 1e-12)
    k2 = k * (1.0 + (a - 1.0) * ka_ref[...])
    bonus = _head_sum(r * k2 * rk_ref[...], ones) * v
    gb_ref[...] = jnp.concatenate([g, bonus], axis=-1).reshape(Bb, tt, 2 * W)
    return r, log_decay, k2, v, kk, a * kk


def _rwkv_kernel(rw_ref, shift0_ref, mu_ref, w0_ref, w2_ref, a0_ref, a2_ref, g2_ref, kk_ref, ka_ref, rk_ref, s0_ref,
                 o_ref, gb_ref, shift_ref, s_ref, carry_ref, *, Bb, L):
    c = pl.program_id(1)
    W, D = RW_WIDTH, HEAD_DIM

    @pl.when(c == 0)
    def _():
        s_ref[...] = s0_ref[...]

    row, col = _iota((L, L), 0), _iota((L, L), 1)
    incl = col <= row
    strict = col < row
    tril_b = jnp.where(incl, 1.0, 0.0).astype(BF16)
    mm = lambda a, b: _dot(a.astype(BF16), b.astype(BF16))
    mm_nt = lambda a, b: _dot_nt(a.astype(BF16), b.astype(BF16))
    mm_tn = lambda a, b: _dot_tn(a.astype(BF16), b.astype(BF16))

    prepped = _rwkv_prep(rw_ref, shift0_ref, mu_ref, w0_ref, w2_ref, a0_ref, a2_ref, g2_ref, kk_ref, ka_ref, rk_ref,
                         gb_ref, shift_ref, carry_ref, Bb=Bb, tt=L)
    units = []
    for b in range(Bb):
        r, logw, k, v, kk, bb = (x[b * L:(b + 1) * L] for x in prepped)
        logw_hi = logw.astype(BF16)
        logw_lo = (logw - logw_hi.astype(F32)).astype(BF16)
        cum = _dot(tril_b, logw_hi) + _dot(tril_b, logw_lo)
        g = jnp.exp(cum)
        ginv = jnp.exp(-cum)
        khat = kk * jnp.exp(cum - logw)
        rhat = r * g
        kd = k * ginv
        bd = bb * ginv
        glast = g[L - 1:L, :]
        for h in range(RW_HEADS):
            sl = slice(h * D, (h + 1) * D)
            units.append(dict(b=b, h=h, sl=sl, khat=khat[:, sl].astype(BF16), rhat=rhat[:, sl].astype(BF16),
                              kd=kd[:, sl], bd=bd[:, sl], v=v[:, sl].astype(BF16), gl=glast[:, sl]))

    for u in units:
        S0 = s_ref[u['b'], u['h']]
        kdb, bdb = u['kd'].astype(BF16), u['bd'].astype(BF16)
        if 2 * L == LANE:
            lhs = jnp.concatenate([u['khat'], u['rhat']], axis=0)
            m1 = _dot_nt(lhs, jnp.concatenate([bdb, kdb], axis=0))
            m2 = _dot_nt(lhs, S0.astype(BF16))
            q_raw, p_raw, aub_raw, avk_raw = m1[0:L, 0:L], m1[0:L, L:], m1[L:, 0:L], m1[L:, L:]
            ks0, u['rs0'] = m2[0:L], m2[L:]
        else:
            q_raw, p_raw = _dot_nt(u['khat'], bdb), _dot_nt(u['khat'], kdb)
            aub_raw, avk_raw = _dot_nt(u['rhat'], bdb), _dot_nt(u['rhat'], kdb)
            ks0, u['rs0'] = _dot_nt(u['khat'], S0.astype(BF16)), _dot_nt(u['rhat'], S0.astype(BF16))
        u['Q'] = jnp.where(strict, q_raw, 0.0).astype(BF16)
        u['P'] = jnp.where(strict, p_raw, 0.0).astype(BF16)
        u['avk'] = jnp.where(incl, avk_raw, 0.0).astype(BF16)
        u['aub'] = jnp.where(incl, aub_raw, 0.0).astype(BF16)
        u['X'] = ks0
    for u in units:
        u['X'] = u['X'] + _dot(u['P'], u['v'])
    for u in units:
        u['X'] = u['X'] - mm(u['Q'], u['X'])
    n = 2
    while n < L:
        for u in units:
            u['Q'] = _dot(u['Q'], u['Q']).astype(BF16)
        for u in units:
            u['X'] = u['X'] + mm(u['Q'], u['X'])
        n *= 2
    for u in units:
        xb = u['X'].astype(BF16)
        o_ref[u['b'], :, u['sl']] = u['rs0'] + _dot(u['avk'], u['v']) - _dot(u['aub'], xb)
        gl = u['gl']
        S0 = s_ref[u['b'], u['h']]
        s_ref[u['b'], u['h']] = S0 * gl + _dot_tn(u['v'], (u['kd'] * gl).astype(BF16)) - _dot_tn(xb, (u['bd'] * gl).astype(BF16))


def _rwkv(rw, shift0p, S0, mu_p, w0, w2, a0, a2, g2, k_k, k_a, r_k, B, T):
    N = B * T
    W = RW_WIDTH
    Bb = 8 if B % 8 == 0 else B
    L = math.gcd(T, RW_CHUNK)
    const = lambda shape: pl.BlockSpec(shape, lambda b, c: (0,) * len(shape))
    blk = lambda w: pl.BlockSpec((Bb, L, w), lambda b, c: (b, c, 0))
    per_seq = pl.BlockSpec((Bb, 1, 1024), lambda b, c: (b, 0, 0))
    st = pl.BlockSpec((Bb, RW_HEADS, HEAD_DIM, HEAD_DIM), lambda b, c: (b, 0, 0, 0))
    vec = lambda a: a.reshape(1, -1)
    y, gb, shift, S = pl.pallas_call(
        functools.partial(_rwkv_kernel, Bb=Bb, L=L),
        grid=(B // Bb, T // L),
        in_specs=[blk(1024), per_seq, const((1, 1024)),
                  const((1, W)), const((RW_LORA_W, W)), const((1, W)), const((RW_LORA_A, W)), const((RW_LORA_G, W)),
                  const((1, W)), const((1, W)), const((1, W)), st],
        out_specs=[blk(W), blk(2 * W), per_seq, st],
        out_shape=[jax.ShapeDtypeStruct((B, T, W), F32), jax.ShapeDtypeStruct((B, T, 2 * W), F32),
                   jax.ShapeDtypeStruct((B, 1, 1024), F32), jax.ShapeDtypeStruct(S0.shape, F32)],
        scratch_shapes=[pltpu.VMEM((Bb, 1, 1024), F32)],
        compiler_params=_cp(("parallel", "arbitrary")),
        name="rwkv",
    )(rw.reshape(B, T, 1024), shift0p.reshape(B, 1, 1024), vec(mu_p), vec(w0), w2, vec(a0), a2, g2, vec(k_k),
      vec(k_a), vec(r_k), S0)
    return y.reshape(N, W), gb.reshape(N, 2 * W), S, shift[:, 0]


def _compress_slab(xg, j, posl_ref, posh_ref, wl_ref, wh_ref, w2_ref):
    G = xg.shape[0]
    lo = _dot((xg + posl_ref[j]).astype(BF16), wl_ref[j])
    hi = _dot((xg + posh_ref[j]).astype(BF16), wh_ref[j])
    hid = _silu(lo + pltpu.roll(hi, G - 1, 0))
    return _dot(hid.astype(BF16), w2_ref[j])


def _compress_kernel(xk_ref, xv_ref, posl_ref, posh_ref, wl_ref, wh_ref, w2_ref, o_ref, *, G):
    for j, x_ref in enumerate((xk_ref, xv_ref)):
        xg = jnp.concatenate([x_ref.at[0][pl.ds(r, G, stride=GRP_ROWS), :] for r in range(GRP_ROWS)], axis=-1)
        o_ref[0, :, j * KV_WIDTH:(j + 1) * KV_WIDTH] = _compress_slab(xg, j, posl_ref, posh_ref, wl_ref, wh_ref, w2_ref)


def _pack_cmp_weights(cmp_pos, cmp_w1, cmp_w2):
    eye = jnp.eye(NSA_KV_HEADS, dtype=F32)
    w1 = cmp_w1.reshape(2, CMP_LEN, HEAD_DIM, CMP_HID)
    wbig = jnp.einsum('jrde,bg->jrbdge', w1, eye).reshape(2, CMP_LEN, KV_WIDTH, NSA_KV_HEADS * CMP_HID).astype(BF16)
    w_lo = wbig[:, :GRP_ROWS].reshape(2, GRP_ROWS * KV_WIDTH, NSA_KV_HEADS * CMP_HID)
    w_hi = wbig[:, GRP_ROWS:].reshape(2, GRP_ROWS * KV_WIDTH, NSA_KV_HEADS * CMP_HID)
    w2big = jnp.einsum('jed,bg->jbegd', cmp_w2, eye).reshape(2, NSA_KV_HEADS * CMP_HID, KV_WIDTH).astype(BF16)
    pos = jnp.broadcast_to(cmp_pos[:, :, None, :], (2, CMP_LEN, NSA_KV_HEADS, HEAD_DIM))
    pos_lo = pos[:, :GRP_ROWS].reshape(2, 1, -1)
    pos_hi = pos[:, GRP_ROWS:].reshape(2, 1, -1)
    return w_lo, w_hi, w2big, pos_lo, pos_hi


_CMP_K = GRP_ROWS * KV_WIDTH
_CMP_SPECS = [(2, 1, _CMP_K), (2, 1, _CMP_K), (2, _CMP_K, NSA_KV_HEADS * CMP_HID), (2, _CMP_K, NSA_KV_HEADS * CMP_HID),
              (2, NSA_KV_HEADS * CMP_HID, KV_WIDTH)]


def _overlap_map(n_rows, n_cmp, n_sel):
    cs = np.arange(n_rows) * CMP_STRIDE
    ss = np.arange(LANE) * SEL_BLOCK
    ov = np.clip(np.minimum(cs[:, None] + CMP_LEN, ss[None, :] + SEL_BLOCK) - np.maximum(cs[:, None], ss[None, :]),
                 0, None) / CMP_LEN
    ov[n_cmp:, :] = 0.0
    ov[:, n_sel:] = 0.0
    return ov


def _compress(xk, xv, cmp_pos, cmp_w1, cmp_w2, B, G):
    rows = xk.shape[1]
    w_lo, w_hi, w2big, pos_lo, pos_hi = _pack_cmp_weights(cmp_pos, cmp_w1, cmp_w2)
    const = lambda shape: pl.BlockSpec(shape, lambda b: (0,) * len(shape))
    slab = pl.BlockSpec((1, rows, KV_WIDTH), lambda b: (b, 0, 0))
    return pl.pallas_call(
        functools.partial(_compress_kernel, G=G),
        grid=(B,),
        in_specs=[slab, slab] + [const(s) for s in _CMP_SPECS],
        out_specs=pl.BlockSpec((1, G, 2 * KV_WIDTH), lambda b: (b, 0, 0)),
        out_shape=jax.ShapeDtypeStruct((B, G, 2 * KV_WIDTH), F32),
        compiler_params=_cp(("parallel",)),
        name="nsa_compress",
    )(xk, xv, pos_lo, pos_hi, w_lo, w_hi, w2big)


def _softmax_pv(s, bias, v_aug, *, QB, v_keys_minor):
    G, D = NSA_GROUP, HEAD_DIM
    K = s.shape[-1]
    s3 = s.reshape(G, QB, K) + bias[None]
    p = jnp.exp((s3 - jnp.max(s3, axis=-1, keepdims=True)).astype(BF16)).reshape(G * QB, K)
    oa = _dot_nt(p, v_aug) if v_keys_minor else _dot(p, v_aug)
    return oa[:, 0:D] / jnp.maximum(oa[:, D:D + 1], 1e-30)


def _cmp_select(units, ov, QB, NS, n_rank):
    G = NSA_GROUP
    blk = _iota((1, LANE), 1)
    for u in units:
        u['s_c'] = _dot_nt(u['q_raw'], u['ck'].astype(BF16))
    for u in units:
        cmp_end = _iota((1, u['ck'].shape[0]), 1) * CMP_STRIDE + (CMP_LEN - 1)
        u['p_c'] = _masked_softmax(u['s_c'], cmp_end <= u['tq_r'])
    for u in units:
        p_c = u['p_c']
        u['o_c'] = _dot(p_c.astype(BF16), u['cv'].astype(BF16))
        psum = p_c[0:QB]
        for h in range(1, G):
            psum = psum + p_c[h * QB:(h + 1) * QB]
        imp = _dot(psum, ov, HIGHEST)
        tq = u['tq']
        cur = tq // SEL_BLOCK
        avail = blk * SEL_BLOCK <= tq
        forced = (blk == 0) | (blk == cur) | (blk == cur - 1)
        u['score'] = jnp.where(avail, imp + jnp.where(forced, FORCE_BONUS, 0.0), SEL_NEG)
        u['cnt'] = jnp.zeros((QB, LANE), F32)
    for i in range(n_rank):
        for u in units:
            score = u['score']
            col = score[:, i:i + 1]
            u['cnt'] = u['cnt'] + jnp.where((col > score) | ((col >= score) & (blk > i)), 1.0, 0.0)
    for u in units:
        u['sel'] = ((u['cnt'] < min(SEL_TOPN, NS)) & (u['score'] > 0.5 * SEL_NEG)).astype(BF16)


def _head_stack(x, base, g):
    G, D = NSA_GROUP, HEAD_DIM
    return jnp.concatenate([x[:, base + (g * G + h) * D:base + (g * G + h + 1) * D] for h in range(G)], 0)


def _gate_stack(gates, g, j):
    c0 = 2 * ML_HEADS + 3 * g * NSA_GROUP + j
    return jnp.concatenate([gates[:, c0 + 3 * h:c0 + 3 * h + 1] for h in range(NSA_GROUP)], 0)


def _softmax_rows(s_ref, b_ref, p_ref, K, QB, R):
    s3 = s_ref[:, 0:K].reshape(R // QB, QB, K) + b_ref[:, 0:K][None]
    p_ref[:, 0:K] = jnp.exp((s3 - jnp.max(s3, axis=-1, keepdims=True)).astype(BF16)).reshape(R, K)


def _pv_normalised(p, v_aug):
    D = HEAD_DIM
    oa = _dot(p, v_aug)
    return oa[:, 0:D] / jnp.maximum(oa[:, D:D + 1], 1e-30)


def _nsa_kernel(q_ref, sm_ref, ckv_ref, kv_ref, win_ref, ov_ref, ex_ref, o_ref, skb, svb, wkb, wvb,
                s_scr, p_scr, bs_scr, sw_scr, pw_scr, bw_scr, *, QB, NQ, NB, NS, P, WN, nvar):
    nb = pl.program_id(1)
    D = HEAD_DIM
    G = NSA_GROUP
    R = G * QB
    LW = wkb.shape[1]

    @pl.when(nb == 0)
    def _():
        for g in range(NSA_KV_HEADS):
            skb[g] = kv_ref[0, :, g * D:(g + 1) * D].astype(BF16)
            svb[g, :, 0:D] = kv_ref[0, :, KV_WIDTH + g * D:KV_WIDTH + (g + 1) * D].astype(BF16)
            svb[g, :, D:2 * D] = jnp.ones((P, D), BF16)
            wkb[g] = win_ref[0, :, g * D:(g + 1) * D].astype(BF16)
            wvb[g, :, 0:D] = win_ref[0, :, KV_WIDTH + g * D:KV_WIDTH + (g + 1) * D].astype(BF16)
            wvb[g, :, D:2 * D] = jnp.ones((LW, D), BF16)

    def body(PS):
        kpos = _iota((1, PS), 1)
        units = []
        for qi in range(NQ):
            rows = slice(qi * QB, (qi + 1) * QB)
            qs = (nb * NQ + qi) * QB
            tq = qs + _iota((QB, 1), 0)
            tq_r = qs + (_iota((R, 1), 0) % QB)
            gates = _sigmoid(sm_ref[rows, :])
            wstart = pl.multiple_of(jnp.clip(qs - WINDOW, 0, LW - WN), 16)
            dlt = tq - (wstart + _iota((1, WN), 1))
            bw_scr[qi] = jnp.where((dlt >= 0) & (dlt < WINDOW), 0.0, NEG)
            qv = q_ref[rows, :] * (D ** -0.5)
            for g in range(NSA_KV_HEADS):
                units.append(dict(i=qi * NSA_KV_HEADS + g, qi=qi, g=g, tq=tq, tq_r=tq_r, gates=gates, wstart=wstart,
                                  q_raw=_head_stack(qv, 0, g).astype(BF16),
                                  q_rot=_head_stack(qv, NSA_WIDTH, g).astype(BF16), ck=ckv_ref[0, :, g * D:(g + 1) * D],
                                  cv=ckv_ref[0, :, KV_WIDTH + g * D:KV_WIDTH + (g + 1) * D]))
        for u in units:
            i, g = u['i'], u['g']
            s_scr[i, :, 0:PS] = _dot_nt(u['q_rot'], skb[g, 0:PS, :])
            sw_scr[i] = _dot_nt(u['q_rot'], wkb[g, pl.ds(u['wstart'], WN), :])
        for u in units:
            i, g = u['i'], u['g']
            _softmax_rows(sw_scr.at[i], bw_scr.at[u['qi']], pw_scr.at[i], WN, QB, R)
            u['o_w'] = _pv_normalised(pw_scr[i], wvb[g, pl.ds(u['wstart'], WN), :])
        _cmp_select(units, ov_ref[...], QB, NS, min(NS, PS // SEL_BLOCK))
        for u in units:
            selk = _dot(u['sel'], ex_ref[:, 0:PS])
            bs_scr[u['i'], :, 0:PS] = jnp.where((selk > 0.5) & (kpos <= u['tq']), 0.0, NEG)
        for u in units:
            i, g, gates = u['i'], u['g'], u['gates']
            _softmax_rows(s_scr.at[i], bs_scr.at[i], p_scr.at[i], PS, QB, R)
            o_s = _pv_normalised(p_scr[i, :, 0:PS], svb[g, 0:PS, :])
            u['o'] = _gate_stack(gates, g, 0) * u['o_c'] + _gate_stack(gates, g, 1) * o_s + _gate_stack(gates, g, 2) * u['o_w']
        for qi in range(NQ):
            outs = [u['o'][h * QB:(h + 1) * QB] for u in units if u['qi'] == qi for h in range(G)]
            o_ref[qi * QB:(qi + 1) * QB, :] = jnp.concatenate(outs, axis=-1)

    for v in range(nvar):
        pl.when((nb * NQ + NQ - 1) // (NB // nvar) == v)(functools.partial(body, P * (v + 1) // nvar))


def _nsa_attend(qq, sm, ckv, kv_rows, win_rows, B, T):
    N = B * T
    QB = math.gcd(T, Q_BLOCK)
    NB = T // QB
    P = LW = T
    NS = -(-T // SEL_BLOCK)
    NC = (T - CMP_LEN) // CMP_STRIDE + 1
    NCP = ckv.shape[1]
    WN = -(-(WINDOW + QB) // LANE) * LANE
    assert NS <= LANE and NC <= NCP and P % LANE == 0 and QB % 32 == 0 and LW >= WN and (LW - WN) % 16 == 0
    ov = _overlap_map(NCP, NC, NS)
    expand = (np.arange(P)[None, :] // SEL_BLOCK == np.arange(LANE)[:, None]).astype(np.float32)
    const = lambda shape: pl.BlockSpec(shape, lambda b, nb: (0,) * len(shape))
    nvar = next((v for v in (4, 2) if NB % v == 0 and (P // v) % (2 * LANE) == 0), 1)
    NQ = 2 if (NB // nvar) % 2 == 0 else 1
    rows = lambda w: pl.BlockSpec((NQ * QB, w), lambda b, nb: (b * (NB // NQ) + nb, 0))
    KVH, R = NSA_KV_HEADS, NSA_GROUP * QB
    NU = NQ * KVH
    return pl.pallas_call(
        functools.partial(_nsa_kernel, QB=QB, NQ=NQ, NB=NB, NS=NS, P=P, WN=WN, nvar=nvar),
        grid=(B, NB // NQ),
        in_specs=[rows(2 * NSA_WIDTH), rows(LANE),
                  pl.BlockSpec((1, NCP, 2 * KV_WIDTH), lambda b, nb: (b, 0, 0)),
                  pl.BlockSpec((1, P, 2 * KV_WIDTH), lambda b, nb: (b, 0, 1)),
                  pl.BlockSpec((1, LW, 2 * KV_WIDTH), lambda b, nb: (b, 0, 0)),
                  const((NCP, LANE)), const((LANE, P))],
        out_specs=rows(NSA_WIDTH),
        out_shape=jax.ShapeDtypeStruct((N, NSA_WIDTH), F32),
        scratch_shapes=[pltpu.VMEM((KVH, P, HEAD_DIM), BF16), pltpu.VMEM((KVH, P, 2 * HEAD_DIM), BF16),
                        pltpu.VMEM((KVH, LW, HEAD_DIM), BF16), pltpu.VMEM((KVH, LW, 2 * HEAD_DIM), BF16),
                        pltpu.VMEM((NU, R, P), F32), pltpu.VMEM((NU, R, P), BF16), pltpu.VMEM((NU, QB, P), F32),
                        pltpu.VMEM((NU, R, WN), F32), pltpu.VMEM((NU, R, WN), BF16), pltpu.VMEM((NQ, QB, WN), F32)],
        compiler_params=_cp(("parallel", "arbitrary")),
        name="nsa_attend",
    )(qq, sm, ckv, kv_rows, win_rows, jnp.asarray(ov, F32), jnp.asarray(expand, BF16))


def _nsa_paged_kernel(pt_ref, *refs, SEQ, n_pages, page, T, NS, offset, lwin):
    del pt_ref
    page_refs = refs[:SEQ * n_pages]
    (q_ref, sm_ref, new_ref, wcache_ref, wnew_ref, posl_ref, posh_ref, wl_ref, wh_ref, w2_ref, ov_ref, ex_ref,
     o_ref, wout_ref, x_ref) = refs[SEQ * n_pages:]
    D, G = HEAD_DIM, NSA_GROUP
    QB = T
    R = G * QB
    P = (n_pages + 1) * page
    NG = n_pages * page // GRP_ROWS
    pages = lambda s: page_refs[s * n_pages:(s + 1) * n_pages]

    def new_tile(x, c0):
        return x[:, c0:c0 + D].T.astype(BF16)

    for s in range(SEQ):
        for i in range(n_pages):
            for j in range(2):
                for g in range(NSA_KV_HEADS):
                    x_ref[s, j, i * page:(i + 1) * page, g * D:(g + 1) * D] = pages(s)[i][0, j, g].T
    ckv = {}
    for s in range(SEQ):
        for j in range(2):
            xg = jnp.concatenate([x_ref.at[s, j][pl.ds(r, NG, stride=GRP_ROWS), :] for r in range(GRP_ROWS)], axis=-1)
            ckv[s, j] = _compress_slab(xg, j, posl_ref, posh_ref, wl_ref, wh_ref, w2_ref)

    tq = offset + _iota((QB, 1), 0)
    tq_r = offset + (_iota((R, 1), 0) % QB)
    kpos = _iota((1, P), 1)
    dlt = tq - (offset - lwin + _iota((1, lwin + page), 1))
    bias_w = jnp.where((dlt >= 0) & (dlt < WINDOW), 0.0, NEG)
    zrows = lambda w: jnp.zeros((page - T, w), F32)
    softmax_pv = functools.partial(_softmax_pv, QB=QB, v_keys_minor=True)

    units = []
    for s in range(SEQ):
        rows = slice(s * T, (s + 1) * T)
        qv = q_ref[rows, :] * (D ** -0.5)
        gates = _sigmoid(sm_ref[rows, :])
        new_t = jnp.concatenate([new_ref[rows, :], zrows(KV_ROW)], axis=0)
        wnew_t = jnp.concatenate([wnew_ref[rows, :], zrows(2 * KV_WIDTH)], axis=0)
        for g in range(NSA_KV_HEADS):
            kt = jnp.concatenate([pg[0, 2, g].astype(BF16) for pg in pages(s)]
                                 + [new_tile(new_t, 2 * KV_WIDTH + g * D)], axis=1)
            vt = jnp.concatenate([pg[0, 3, g].astype(BF16) for pg in pages(s)]
                                 + [new_tile(new_t, 3 * KV_WIDTH + g * D)], axis=1)
            wkt = jnp.concatenate([wcache_ref[s, 0, g].astype(BF16), new_tile(wnew_t, g * D)], axis=1)
            wvt = jnp.concatenate([wcache_ref[s, 1, g].astype(BF16), new_tile(wnew_t, KV_WIDTH + g * D)], axis=1)
            units.append(dict(s=s, g=g, tq=tq, tq_r=tq_r, gates=gates, q_raw=_head_stack(qv, 0, g).astype(BF16),
                              q_rot=_head_stack(qv, NSA_WIDTH, g).astype(BF16), kt=kt, vt=vt, wkt=wkt, wvt=wvt,
                              ck=ckv[s, 0][:, g * D:(g + 1) * D], cv=ckv[s, 1][:, g * D:(g + 1) * D]))
    for u in units:
        u['s_s'] = _dot(u['q_rot'], u['kt'])
        u['s_w'] = _dot(u['q_rot'], u['wkt'])
    for u in units:
        u['o_w'] = softmax_pv(u['s_w'], bias_w, jnp.concatenate([u['wvt'], jnp.ones_like(u['wvt'])], axis=0))
    _cmp_select(units, ov_ref[...], QB, NS, NS)
    for u in units:
        bias_s = jnp.where((_dot(u['sel'], ex_ref[...]) > 0.5) & (kpos <= tq), 0.0, NEG)
        o_s = softmax_pv(u['s_s'], bias_s, jnp.concatenate([u['vt'], jnp.ones_like(u['vt'])], axis=0))
        gates, g = u['gates'], u['g']
        u['o'] = _gate_stack(gates, g, 0) * u['o_c'] + _gate_stack(gates, g, 1) * o_s + _gate_stack(gates, g, 2) * u['o_w']
    for s in range(SEQ):
        outs = [u['o'][h * QB:(h + 1) * QB] for u in units if u['s'] == s for h in range(G)]
        o_ref[s * T:(s + 1) * T, :] = jnp.concatenate(outs, axis=-1)
    for s in range(SEQ):
        wnew_t = jnp.concatenate([wnew_ref[s * T:(s + 1) * T, :], zrows(2 * KV_WIDTH)], axis=0)
        for kv in range(2):
            for g in range(NSA_KV_HEADS):
                c0 = kv * KV_WIDTH + g * D
                wout_ref[s, kv, g] = jnp.concatenate([wcache_ref[s, kv, g][:, T:], wnew_t[:, c0:c0 + D].T[:, 0:T]], axis=1)


def _nsa_paged(qq, sm, kv_new, win_new, cache_t, page_ids, wcache_t, win_base, cmp_pos, cmp_w1, cmp_w2, B, T, offset):
    N = B * T
    n_pages = page_ids.shape[1]
    page = cache_t.shape[-1]
    lwin = wcache_t.shape[-1]
    p_len = n_pages * page + T
    P = (n_pages + 1) * page
    NS = -(-p_len // SEL_BLOCK)
    n_cmp = (p_len - CMP_LEN) // CMP_STRIDE + 1
    NG = n_pages * page // GRP_ROWS
    SEQ = 2 if (B % 2 == 0 and win_base % 2 == 0) else 1
    assert T % 8 == 0 and T <= page and NS <= LANE and n_cmp + 1 == NG and lwin % LANE == 0 and NG % 8 == 0
    w_lo, w_hi, w2big, pos_lo, pos_hi = _pack_cmp_weights(cmp_pos, cmp_w1, cmp_w2)
    ov = _overlap_map(NG, n_cmp, NS)
    expand = (np.arange(P)[None, :] // SEL_BLOCK == np.arange(LANE)[:, None]).astype(np.float32)

    def page_map(k, b, pt):
        return (pt[b * SEQ * n_pages + k], 0, 0, 0, 0)

    const = lambda shape: pl.BlockSpec(shape, lambda b, pt: (0,) * len(shape))
    rows = lambda w: pl.BlockSpec((SEQ * T, w), lambda b, pt: (b, 0))
    in_specs = [pl.BlockSpec((1, 4, NSA_KV_HEADS, HEAD_DIM, page), functools.partial(page_map, k))
                for k in range(SEQ * n_pages)]
    wshape = (SEQ, 2, NSA_KV_HEADS, HEAD_DIM, lwin)
    in_specs += [rows(2 * NSA_WIDTH), rows(LANE), rows(KV_ROW),
                 pl.BlockSpec(wshape, lambda b, pt: (win_base // SEQ + b, 0, 0, 0, 0)),
                 rows(2 * KV_WIDTH)] + [const(s) for s in _CMP_SPECS] + [const((NG, LANE)), const((LANE, P))]
    return pl.pallas_call(
        functools.partial(_nsa_paged_kernel, SEQ=SEQ, n_pages=n_pages, page=page, T=T, NS=NS, offset=offset, lwin=lwin),
        grid_spec=pltpu.PrefetchScalarGridSpec(
            num_scalar_prefetch=1, grid=(B // SEQ,), in_specs=in_specs,
            out_specs=[rows(NSA_WIDTH), pl.BlockSpec(wshape, lambda b, pt: (b, 0, 0, 0, 0))],
            scratch_shapes=[pltpu.VMEM((SEQ, 2, n_pages * page, KV_WIDTH), F32)]),
        out_shape=[jax.ShapeDtypeStruct((N, NSA_WIDTH), F32), jax.ShapeDtypeStruct((B,) + wshape[1:], F32)],
        compiler_params=_cp(("parallel",)),
        name="nsa_paged",
    )(page_ids.reshape(-1), *([cache_t] * (SEQ * n_pages)), qq, sm, kv_new, wcache_t, win_new, pos_lo, pos_hi, w_lo,
      w_hi, w2big, jnp.asarray(ov, F32), jnp.asarray(expand, BF16))


N_MIX = 8


def _mix_residual(x_ref, hml_ref, yrw_ref, gb_ref, hns_ref, lnw_ref, lnb_ref, w_ref):
    W = RW_WIDTH
    y = yrw_ref[...]
    ones = _head_ones(W)
    mu = _head_sum(y, ones) * (1.0 / HEAD_DIM)
    var = _head_sum(jnp.square(y - mu), ones) * (1.0 / HEAD_DIM)
    yn = (y - mu) * lax.rsqrt(var + RW_GN_EPS) * lnw_ref[...]
    h_rw = (yn + lnb_ref[...] + gb_ref[:, W:2 * W]) * gb_ref[:, 0:W]
    acc = _dot(hml_ref[...].astype(BF16), w_ref[0:ML_WIDTH, :])
    acc = acc + _dot(h_rw.astype(BF16), w_ref[ML_WIDTH:ML_WIDTH + W, :])
    acc = acc + _dot(hns_ref[...].astype(BF16), w_ref[ML_WIDTH + W:, :])
    return x_ref[...] + acc


def _mix_specs(tm):
    row = lambda w: pl.BlockSpec((tm, w), lambda i, j: (i, 0))
    const = lambda shape: pl.BlockSpec(shape, lambda i, j: (0,) * len(shape))
    return [row(D_MODEL), row(ML_WIDTH), row(RW_WIDTH), row(2 * RW_WIDTH), row(NSA_WIDTH),
            const((1, RW_WIDTH)), const((1, RW_WIDTH)), const((MIX_WIDTH, D_MODEL))]


def _mix_args(x2, h_ml, y_rw, gb, h_ns, ln_w, ln_b, w_out):
    return (x2, h_ml, y_rw, gb, h_ns, ln_w.reshape(1, -1), ln_b.reshape(1, -1), w_out.astype(BF16))


def _rms(x, w):
    return x * lax.rsqrt(jnp.mean(x * x, axis=-1, keepdims=True) + RMS_EPS) * w


def _ffn_kernel(*refs, final_norm):
    nw_ref, w1_ref, w3_ref, w2_ref, fw_ref, o_ref, xn_ref = refs[N_MIX:]
    j = pl.program_id(1)

    @pl.when(j == 0)
    def _():
        x = _mix_residual(*refs[:N_MIX])
        xn_ref[...] = _rms(x, nw_ref[...]).astype(BF16)
        o_ref[...] = x

    xn = xn_ref[...]
    hid = _silu(_dot(xn, w1_ref[...])) * _dot(xn, w3_ref[...])
    o_ref[...] += _dot(hid.astype(BF16), w2_ref[...])

    if final_norm:
        @pl.when(j == pl.num_programs(1) - 1)
        def _():
            o_ref[...] = _rms(o_ref[...], fw_ref[...])


def _ffn(mix, norm_w, w1, w3, w2, final_w):
    N = mix[0].shape[0]
    d_ff = w1.shape[1]
    tm = min(512, N)
    nj = 2 if (d_ff // 2) % LANE == 0 else 1
    tf = d_ff // nj
    final_norm = final_w is not None
    fw = (final_w if final_norm else jnp.ones((D_MODEL,), F32)).reshape(1, D_MODEL)
    return pl.pallas_call(
        functools.partial(_ffn_kernel, final_norm=final_norm),
        grid=(N // tm, nj),
        in_specs=_mix_specs(tm) + [
            pl.BlockSpec((1, D_MODEL), lambda i, j: (0, 0)),
            pl.BlockSpec((D_MODEL, tf), lambda i, j: (0, j)), pl.BlockSpec((D_MODEL, tf), lambda i, j: (0, j)),
            pl.BlockSpec((tf, D_MODEL), lambda i, j: (j, 0)), pl.BlockSpec((1, D_MODEL), lambda i, j: (0, 0))],
        out_specs=pl.BlockSpec((tm, D_MODEL), lambda i, j: (i, 0)),
        out_shape=jax.ShapeDtypeStruct((N, D_MODEL), F32),
        scratch_shapes=[pltpu.VMEM((tm, D_MODEL), BF16)],
        compiler_params=_cp(("parallel", "arbitrary")),
        name="ffn",
    )(*_mix_args(*mix), norm_w.reshape(1, D_MODEL), w1.astype(BF16), w3.astype(BF16), w2.astype(BF16), fw)


def _moe_kernel(*refs, final_norm):
    nw_ref, rt_ref, w1_ref, w3_ref, w2_ref, fw_ref, o_ref, xn_ref, comb_ref = refs[N_MIX:]
    e = pl.program_id(1)

    @pl.when(e == 0)
    def _():
        x = _mix_residual(*refs[:N_MIX])
        xn = _rms(x, nw_ref[...])
        xn_ref[...] = xn.astype(BF16)
        o_ref[...] = x
        lane = _iota((1, LANE), 1).astype(F32)
        logits = jnp.where(lane < N_EXPERTS, _dot(xn, rt_ref[...], HIGHEST), -jnp.inf)
        m1 = jnp.max(logits, axis=-1, keepdims=True)
        i1 = jnp.min(jnp.where(logits == m1, lane, float(LANE)), axis=-1, keepdims=True)
        rest = jnp.where(lane == i1, -jnp.inf, logits)
        m2 = jnp.max(rest, axis=-1, keepdims=True)
        i2 = jnp.min(jnp.where(rest == m2, lane, float(LANE)), axis=-1, keepdims=True)
        e2 = jnp.exp(m2 - m1)
        den = 1.0 + e2
        comb_ref[...] = jnp.where(lane == i1, 1.0 / den, 0.0) + jnp.where(lane == i2, e2 / den, 0.0)

    xn = xn_ref[...]
    comb = comb_ref[...]
    ce = jnp.sum(jnp.where(_iota((1, LANE), 1) == e, comb, 0.0), axis=-1, keepdims=True)
    hid = _silu(_dot(xn, w1_ref[0])) * _dot(xn, w3_ref[0])
    o_ref[...] += ce * _dot(hid.astype(BF16), w2_ref[0])

    if final_norm:
        @pl.when(e == pl.num_programs(1) - 1)
        def _():
            o_ref[...] = _rms(o_ref[...], fw_ref[...])


def _moe(mix, norm_w, router, w1, w3, w2, final_w):
    N = mix[0].shape[0]
    E, _, d_exp = w1.shape
    assert E == N_EXPERTS and TOP_K == 2
    tm = min(512, N)
    final_norm = final_w is not None
    fw = (final_w if final_norm else jnp.ones((D_MODEL,), F32)).reshape(1, D_MODEL)
    rt = jnp.pad(router, ((0, 0), (0, LANE - E)))
    return pl.pallas_call(
        functools.partial(_moe_kernel, final_norm=final_norm),
        grid=(N // tm, E),
        in_specs=_mix_specs(tm) + [
            pl.BlockSpec((1, D_MODEL), lambda i, e: (0, 0)),
            pl.BlockSpec((D_MODEL, LANE), lambda i, e: (0, 0)),
            pl.BlockSpec((1, D_MODEL, d_exp), lambda i, e: (e, 0, 0)),
            pl.BlockSpec((1, D_MODEL, d_exp), lambda i, e: (e, 0, 0)),
            pl.BlockSpec((1, d_exp, D_MODEL), lambda i, e: (e, 0, 0)),
            pl.BlockSpec((1, D_MODEL), lambda i, e: (0, 0))],
        out_specs=pl.BlockSpec((tm, D_MODEL), lambda i, e: (i, 0)),
        out_shape=jax.ShapeDtypeStruct((N, D_MODEL), F32),
        scratch_shapes=[pltpu.VMEM((tm, D_MODEL), BF16), pltpu.VMEM((tm, LANE), F32)],
        compiler_params=_cp(("parallel", "arbitrary")),
        name="moe",
    )(*_mix_args(*mix), norm_w.reshape(1, D_MODEL), rt, w1.astype(BF16), w3.astype(BF16), w2.astype(BF16), fw)


def _layer(l, x2, B, T, offset, st, P, final_w):
    C0, n0, m0, conv0, S0, shift0, cache, page_ids, win0 = st
    rw_perm = _rw_order()
    pos = offset + jnp.arange(T)
    cos2, sin2 = _rope_tables(pos)
    ml, rw, qq, kv4, wrow, sm, cks, cvs = _inproj(x2, P['norm_mix_w'][l], _pack_w_in(P['w_in'][l]), cos2, sin2, T)

    h_ml, C, n, m, conv = _mlstm(ml, sm, conv0, C0, n0, m0, P['ml_conv_w'][l], P['ml_conv_b'][l],
                                 P['ml_b_i'][l], P['ml_b_f'][l], P['ml_norm_w'][l], B, T)

    y_rw, gb, S, shift_p = _rwkv(rw, _take_cols(shift0, rw_perm), S0, _take_cols(P['rw_mu'][l], rw_perm),
                                 P['rw_w0'][l], P['rw_w2'][l], P['rw_a0'][l], P['rw_a2'][l], P['rw_g2'][l],
                                 P['rw_k_k'][l], P['rw_k_a'][l], P['rw_r_k'][l].reshape(-1), B, T)
    shift_new = _take_cols(shift_p, np.argsort(rw_perm))

    kv3 = kv4.reshape(B, T, KV_ROW)
    w3 = wrow.reshape(B, T, 2 * KV_WIDTH)
    cmp_w = (P['nsa_cmp_pos'][l], P['nsa_cmp_w1'][l], P['nsa_cmp_w2'][l])
    wbuf = P['wbuf']
    if cache is None:
        n_cmp = (T - CMP_LEN) // CMP_STRIDE + 1
        G = -(-(n_cmp + 1) // 8) * 8
        assert G * GRP_ROWS <= T
        ckv = _compress(cks.reshape(B, T, KV_WIDTH), cvs.reshape(B, T, KV_WIDTH), *cmp_w, B, G)
        assert offset == 0
        h_ns = _nsa_attend(qq, sm, ckv, kv3, w3, B, T)
        win_new = w3[:, T - wbuf:] if T >= wbuf else jnp.pad(w3, ((0, 0), (wbuf - T, 0), (0, 0)))
        win_new = win_new.reshape(B, wbuf, 2, NSA_KV_HEADS, HEAD_DIM)
    else:
        assert win0['key_minor'].shape[-1] == wbuf and T <= wbuf
        h_ns, wout = _nsa_paged(qq, sm, kv4, wrow, cache, page_ids, win0['key_minor'], win0['base'], *cmp_w, B, T, offset)
        win_new = jnp.transpose(wout, (0, 4, 1, 2, 3))

    mix = (x2, h_ml, y_rw, gb, h_ns, P['rw_ln_w'][l], P['rw_ln_b'][l], P['w_out'][l])
    if l % 2 == 0:
        x2 = _ffn(mix, P['norm_ffn_w'][l], P['ffn_w1'][l // 2], P['ffn_w3'][l // 2], P['ffn_w2'][l // 2], final_w)
    else:
        x2 = _moe(mix, P['norm_ffn_w'][l], P['moe_router'][l // 2], P['moe_w1'][l // 2], P['moe_w3'][l // 2],
                  P['moe_w2'][l // 2], final_w)
    new = (kv3.reshape(B, T, 4, NSA_KV_HEADS, HEAD_DIM), win_new, C, n, m, conv, S, shift_new)
    return x2, new


def _run(x, offset, get_state, P, depth):
    B, T, _ = x.shape
    x2 = x.reshape(B * T, D_MODEL)
    news = []
    for l in range(depth):
        x2, nw = _layer(l, x2, B, T, offset, get_state(l), P, P['norm_final_w'] if l == depth - 1 else None)
        news.append(nw)
    return x2.reshape(B, T, D_MODEL), [jnp.stack([nw[i] for nw in news]) for i in range(8)]


def kernel(x_prompt, x_sample, cache_nsa_kv, cache_win_kv, state_mlstm_C, state_mlstm_n, state_mlstm_m, state_mlstm_conv, state_rwkv_S, state_rwkv_shift, page_table, norm_mix_w, w_in, ml_conv_w, ml_conv_b, ml_b_i, ml_b_f, ml_norm_w, rw_mu, rw_w0, rw_w2, rw_a0, rw_a2, rw_g2, rw_k_k, rw_k_a, rw_r_k, rw_ln_w, rw_ln_b, nsa_cmp_pos, nsa_cmp_w1, nsa_cmp_w2, w_out, norm_ffn_w, ffn_w1, ffn_w3, ffn_w2, moe_router, moe_w1, moe_w3, moe_w2, norm_final_w):
    depth = w_in.shape[0]
    Bp = x_prompt.shape[0]
    Bs = x_sample.shape[0]
    n_pool, page = cache_nsa_kv.shape[1], cache_nsa_kv.shape[2]
    past_len = page_table.shape[1] * page
    wbuf = min(WINDOW, past_len)
    P = dict(norm_mix_w=norm_mix_w, w_in=w_in, ml_conv_w=ml_conv_w, ml_conv_b=ml_conv_b, ml_b_i=ml_b_i, ml_b_f=ml_b_f,
             ml_norm_w=ml_norm_w, rw_mu=rw_mu, rw_w0=rw_w0, rw_w2=rw_w2, rw_a0=rw_a0, rw_a2=rw_a2, rw_g2=rw_g2,
             rw_k_k=rw_k_k, rw_k_a=rw_k_a, rw_r_k=rw_r_k, rw_ln_w=rw_ln_w, rw_ln_b=rw_ln_b, nsa_cmp_pos=nsa_cmp_pos,
             nsa_cmp_w1=nsa_cmp_w1, nsa_cmp_w2=nsa_cmp_w2, w_out=w_out, norm_ffn_w=norm_ffn_w, ffn_w1=ffn_w1,
             ffn_w3=ffn_w3, ffn_w2=ffn_w2, moe_router=moe_router, moe_w1=moe_w1, moe_w3=moe_w3, moe_w2=moe_w2,
             norm_final_w=norm_final_w, wbuf=wbuf)

    def prompt_state(l):
        z = lambda *s: jnp.zeros((Bp,) + s, F32)
        return (z(ML_HEADS, HEAD_DIM, HEAD_DIM), z(ML_HEADS, HEAD_DIM), z(ML_HEADS), z(ML_CONV - 1, 2 * ML_WIDTH),
                z(RW_HEADS, HEAD_DIM, HEAD_DIM), z(RW_COLS), None, None, None)

    cache_t = jnp.transpose(cache_nsa_kv, (0, 1, 3, 4, 5, 2)).reshape(depth * n_pool, 4, NSA_KV_HEADS, HEAD_DIM, page)
    wcache_t = jnp.transpose(cache_win_kv, (0, 1, 3, 4, 5, 2)).reshape(depth * Bs, 2, NSA_KV_HEADS, HEAD_DIM, -1)

    def sample_state(l):
        win = dict(key_minor=wcache_t, base=l * Bs)
        return (state_mlstm_C[l], state_mlstm_n[l], state_mlstm_m[l], state_mlstm_conv[l], state_rwkv_S[l],
                state_rwkv_shift[l], cache_t, page_table + l * n_pool, win)

    y_prompt, pst = _run(x_prompt, 0, prompt_state, P, depth)
    y_sample, sst = _run(x_sample, past_len, sample_state, P, depth)
    p_kv, p_win, p_C, p_n, p_m, p_conv, p_S, p_shift = pst
    s_kv, s_win, s_C, s_n, s_m, s_conv, s_S, s_shift = sst
    return (y_prompt, y_sample, p_kv, s_kv, p_win, s_win, p_C, s_C, p_n, s_n, p_m, s_m,
            p_conv, s_conv, p_S, s_S, p_shift, s_shift)
```
